```python
import math
import jax, jax.numpy as jnp
from jax import lax
import numpy as np

D_MODEL = 1024
BATCH = 2
SEQ = 16384
DEPTH = 2

N_EVEN = (DEPTH + 1) // 2
N_ODD = DEPTH // 2

S5_WIDTH = D_MODEL // 2
S5_GROUP = 16
S5_GROUPS = S5_WIDTH // S5_GROUP
S5_STATE = 64
S5_DT_MIN = 1e-3
S5_DT_MAX = 1e-1
CONV_WIDTH = D_MODEL // 2
CONV_K = 31
MIX_IN = S5_WIDTH + 2 * CONV_WIDTH
MIX_WIDTH = S5_WIDTH + CONV_WIDTH

HEAD_DIM = 64
N_Q_HEADS = D_MODEL // HEAD_DIM
N_KV_HEADS = 2
Q_PER_KV = N_Q_HEADS // N_KV_HEADS
QKV_WIDTH = (N_Q_HEADS + 2 * N_KV_HEADS) * HEAD_DIM
WINDOW = 128
ROPE_THETA = 500000.0
ROPE_DIM = HEAD_DIM // 4

PEER_HEADS = 8
PEER_N_KEYS = 128
PEER_N_EXPERTS = PEER_N_KEYS * PEER_N_KEYS
PEER_QUERY_DIM = 256
PEER_HALF = PEER_QUERY_DIM // 2
PEER_TOPK = 16
PEER_TOKEN_BLOCK = 128

NORM_EPS = 1e-6

kernel_name = "hybrid_s5_conv_swa_peer"


def rmsnorm(x, g):
    xf = x.astype(jnp.float32)
    y = xf * lax.rsqrt(jnp.mean(xf * xf, axis=-1, keepdims=True) + NORM_EPS) * g.astype(jnp.float32)
    return y.astype(x.dtype)


def s5_mixer(u, lam_re, lam_im, log_step, b_re, b_im, c_re, c_im, d, w_glu):
    bsz, seq, _ = u.shape
    uf = u.astype(jnp.float32).reshape(bsz, seq, S5_GROUPS, S5_GROUP)
    lam = lax.complex(lam_re.astype(jnp.float32), lam_im.astype(jnp.float32))
    dt = jnp.exp(log_step.astype(jnp.float32))[:, None]
    a_bar = jnp.exp(lam * dt)
    b = lax.complex(b_re.astype(jnp.float32), b_im.astype(jnp.float32))
    b_bar = ((a_bar - 1.0) / lam)[..., None] * b
    bu = lax.complex(jnp.einsum('blgh,gph->blgp', uf, jnp.real(b_bar)),
                     jnp.einsum('blgh,gph->blgp', uf, jnp.imag(b_bar)))
    a_seq = jnp.broadcast_to(a_bar, (1, seq) + a_bar.shape)

    def combine(left, right):
        a_l, b_l = left
        a_r, b_r = right
        return a_r * a_l, a_r * b_l + b_r

    _, h = lax.associative_scan(combine, (a_seq, bu), axis=1)
    y = (jnp.einsum('blgp,ghp->blgh', jnp.real(h), c_re.astype(jnp.float32))
         - jnp.einsum('blgp,ghp->blgh', jnp.imag(h), c_im.astype(jnp.float32))
         + d.astype(jnp.float32) * uf)
    y = jax.nn.gelu(y, approximate=False).reshape(bsz, seq, S5_WIDTH)
    out = y * jax.nn.sigmoid(y @ w_glu.astype(jnp.float32))
    return out.astype(u.dtype)


def conv_module(z, dw_w, dw_b, ln_g, ln_b):
    val, gate = jnp.split(z, 2, axis=-1)
    v = val * jax.nn.sigmoid(gate)
    y = lax.conv_general_dilated(v, dw_w[:, None, :], window_strides=(1,),
                                 padding=[(CONV_K - 1, 0)],
                                 dimension_numbers=('NWC', 'WIO', 'NWC'),
                                 feature_group_count=CONV_WIDTH) + dw_b
    yf = y.astype(jnp.float32)
    mu = jnp.mean(yf, axis=-1, keepdims=True)
    var = jnp.mean(jnp.square(yf - mu), axis=-1, keepdims=True)
    yn = (yf - mu) * lax.rsqrt(var + NORM_EPS) * ln_g.astype(jnp.float32) + ln_b.astype(jnp.float32)
    return jax.nn.silu(yn).astype(z.dtype)


def partial_rope(t, pos):
    inv = jnp.power(ROPE_THETA, -jnp.arange(0, ROPE_DIM, 2, dtype=jnp.float32) / ROPE_DIM)
    ang = pos[:, None] * inv[None, :]
    cos = jnp.cos(ang)[None, :, None, :]
    sin = jnp.sin(ang)[None, :, None, :]
    tr = t[..., :ROPE_DIM].astype(jnp.float32)
    x1, x2 = tr[..., :ROPE_DIM // 2], tr[..., ROPE_DIM // 2:]
    rot = jnp.concatenate([x1 * cos - x2 * sin, x2 * cos + x1 * sin], axis=-1).astype(t.dtype)
    return jnp.concatenate([rot, t[..., ROPE_DIM:]], axis=-1)


def sliding_window_attention(h, w_qkv, b_qkv, sinks, w_o):
    bsz, seq, _ = h.shape
    qkv = h @ w_qkv + b_qkv
    q, k, v = jnp.split(qkv, [N_Q_HEADS * HEAD_DIM, (N_Q_HEADS + N_KV_HEADS) * HEAD_DIM], axis=-1)
    pos = jnp.arange(seq, dtype=jnp.float32)
    q = partial_rope(q.reshape(bsz, seq, N_Q_HEADS, HEAD_DIM), pos)
    k = partial_rope(k.reshape(bsz, seq, N_KV_HEADS, HEAD_DIM), pos)
    v = v.reshape(bsz, seq, N_KV_HEADS, HEAD_DIM)
    nb = seq // WINDOW
    q = q.reshape(bsz, nb, WINDOW, N_KV_HEADS, Q_PER_KV, HEAD_DIM)
    k = k.reshape(bsz, nb, WINDOW, N_KV_HEADS, HEAD_DIM)
    v = v.reshape(bsz, nb, WINDOW, N_KV_HEADS, HEAD_DIM)

    def with_prev(t):
        prev = jnp.pad(t[:, :-1], ((0, 0), (1, 0), (0, 0), (0, 0), (0, 0)))
        return jnp.concatenate([prev, t], axis=2)

    kb, vb = with_prev(k), with_prev(v)
    s = jnp.einsum('bnqkgd,bnskd->bnkgqs', q, kb,
                   preferred_element_type=jnp.float32) * (HEAD_DIM ** -0.5)
    qi = jnp.arange(WINDOW)[:, None]
    si = jnp.arange(2 * WINDOW)[None, :]
    diff = WINDOW + qi - si
    band = (diff >= 0) & (diff < WINDOW)
    has_prev = (jnp.arange(nb)[:, None, None] > 0) | (si[None] >= WINDOW)
    valid = band[None] & has_prev
    s = jnp.where(valid[None, :, None, None], s, -jnp.inf)
    sink = sinks.astype(jnp.float32).reshape(N_KV_HEADS, Q_PER_KV)[None, None, :, :, None, None]
    m = jnp.maximum(jnp.max(s, axis=-1, keepdims=True), sink)
    p = jnp.exp(s - m)
    denom = jnp.sum(p, axis=-1, keepdims=True) + jnp.exp(sink - m)
    probs = (p / denom).astype(vb.dtype)
    o = jnp.einsum('bnkgqs,bnskd->bnqkgd', probs, vb)
    o = o.reshape(bsz, seq, N_Q_HEADS * HEAD_DIM).astype(h.dtype)
    return o @ w_o


def peer(h, w_q, sub_keys, u_tab, v_tab):
    bsz, seq, dm = h.shape
    t = bsz * seq
    hf = h.reshape(t, dm)
    q = (hf @ w_q).reshape(t, PEER_HEADS, 2, PEER_HALF)
    scores = jnp.einsum('thcd,hckd->thck', q, sub_keys,
                        preferred_element_type=jnp.float32)
    top_s, top_i = lax.top_k(scores, PEER_TOPK)
    cand_s = top_s[:, :, 0, :, None] + top_s[:, :, 1, None, :]
    cand_i = top_i[:, :, 0, :, None] * PEER_N_KEYS + top_i[:, :, 1, None, :]
    best_s, best_c = lax.top_k(cand_s.reshape(t, PEER_HEADS, PEER_TOPK * PEER_TOPK), PEER_TOPK)
    ids = jnp.take_along_axis(cand_i.reshape(t, PEER_HEADS, PEER_TOPK * PEER_TOPK), best_c, axis=-1)
    gates = jax.nn.softmax(best_s, axis=-1)
    n_exp = PEER_HEADS * PEER_TOPK
    nblk = t // PEER_TOKEN_BLOCK

    def token_block(args):
        xb, ib, gb = args
        u = jnp.take(u_tab, ib, axis=0)
        a = jnp.einsum('cd,ced->ce', xb, u, preferred_element_type=jnp.float32)
        w = (jax.nn.gelu(a, approximate=False) * gb).astype(v_tab.dtype)
        v = jnp.take(v_tab, ib, axis=0)
        return jnp.einsum('ce,ced->cd', w, v).astype(h.dtype)

    out = lax.map(token_block, (hf.reshape(nblk, PEER_TOKEN_BLOCK, dm),
                                ids.reshape(nblk, PEER_TOKEN_BLOCK, n_exp),
                                gates.reshape(nblk, PEER_TOKEN_BLOCK, n_exp)))
    return out.reshape(bsz, seq, dm)


def setup_inputs(seed: int = 0) -> dict:
    key = jax.random.key(seed)
    ks = iter(jax.random.split(key, 40))
    f32 = jnp.float32

    def nrm(shape, scale):
        return jax.random.normal(next(ks), shape, f32) * scale

    x = nrm((BATCH, SEQ, D_MODEL), 1.0)
    norm_mix = 1.0 + nrm((DEPTH, D_MODEL), 0.02)
    norm_ffn = 1.0 + nrm((DEPTH, D_MODEL), 0.02)
    norm_final = 1.0 + nrm((D_MODEL,), 0.02)
    mix_w_in = nrm((N_EVEN, D_MODEL, MIX_IN), D_MODEL ** -0.5)
    s5_lambda_re = -0.5 + nrm((N_EVEN, S5_GROUPS, S5_STATE), 0.01)
    s5_lambda_im = jnp.pi * jnp.arange(S5_STATE, dtype=f32) + nrm((N_EVEN, S5_GROUPS, S5_STATE), 0.01)
    s5_log_step = math.log(S5_DT_MIN) + jax.random.uniform(next(ks), (N_EVEN, S5_GROUPS), f32) * (
        math.log(S5_DT_MAX) - math.log(S5_DT_MIN))
    s5_b_re = nrm((N_EVEN, S5_GROUPS, S5_STATE, S5_GROUP), (2.0 * S5_GROUP) ** -0.5)
    s5_b_im = nrm((N_EVEN, S5_GROUPS, S5_STATE, S5_GROUP), (2.0 * S5_GROUP) ** -0.5)
    s5_c_re = nrm((N_EVEN, S5_GROUPS, S5_GROUP, S5_STATE), 0.5)
    s5_c_im = nrm((N_EVEN, S5_GROUPS, S5_GROUP, S5_STATE), 0.5)
    s5_d = nrm((N_EVEN, S5_GROUPS, S5_GROUP), 1.0)
    s5_w_glu = nrm((N_EVEN, S5_WIDTH, S5_WIDTH), S5_WIDTH ** -0.5)
    conv_dw_w = nrm((N_EVEN, CONV_K, CONV_WIDTH), CONV_K ** -0.5)
    conv_dw_b = nrm((N_EVEN, CONV_WIDTH), 0.01)
    conv_ln_g = 1.0 + nrm((N_EVEN, CONV_WIDTH), 0.02)
    conv_ln_b = nrm((N_EVEN, CONV_WIDTH), 0.01)
    mix_w_out = nrm((N_EVEN, MIX_WIDTH, D_MODEL), MIX_WIDTH ** -0.5)
    attn_w_qkv = nrm((N_ODD, D_MODEL, QKV_WIDTH), D_MODEL ** -0.5)
    attn_b_qkv = nrm((N_ODD, QKV_WIDTH), 0.01)
    attn_sinks = nrm((N_ODD, N_Q_HEADS), 0.5)
    attn_w_o = nrm((N_ODD, N_Q_HEADS * HEAD_DIM, D_MODEL), (N_Q_HEADS * HEAD_DIM) ** -0.5)
    peer_w_q = nrm((DEPTH, D_MODEL, PEER_HEADS * PEER_QUERY_DIM), D_MODEL ** -0.5)
    peer_sub_keys = nrm((DEPTH, PEER_HEADS, 2, PEER_N_KEYS, PEER_HALF), PEER_HALF ** -0.5)
    peer_u = nrm((DEPTH, PEER_N_EXPERTS, D_MODEL), D_MODEL ** -0.5)
    peer_v = nrm((DEPTH, PEER_N_EXPERTS, D_MODEL), 0.3)
    return {
        'x': x, 'norm_mix': norm_mix, 'norm_ffn': norm_ffn, 'norm_final': norm_final,
        'mix_w_in': mix_w_in, 's5_lambda_re': s5_lambda_re, 's5_lambda_im': s5_lambda_im,
        's5_log_step': s5_log_step, 's5_b_re': s5_b_re, 's5_b_im': s5_b_im,
        's5_c_re': s5_c_re, 's5_c_im': s5_c_im, 's5_d': s5_d, 's5_w_glu': s5_w_glu,
        'conv_dw_w': conv_dw_w, 'conv_dw_b': conv_dw_b, 'conv_ln_g': conv_ln_g, 'conv_ln_b': conv_ln_b,
        'mix_w_out': mix_w_out, 'attn_w_qkv': attn_w_qkv, 'attn_b_qkv': attn_b_qkv,
        'attn_sinks': attn_sinks, 'attn_w_o': attn_w_o, 'peer_w_q': peer_w_q,
        'peer_sub_keys': peer_sub_keys, 'peer_u': peer_u, 'peer_v': peer_v,
    }


def reference(x, norm_mix, norm_ffn, norm_final, mix_w_in, s5_lambda_re, s5_lambda_im,
              s5_log_step, s5_b_re, s5_b_im, s5_c_re, s5_c_im, s5_d, s5_w_glu,
              conv_dw_w, conv_dw_b, conv_ln_g, conv_ln_b, mix_w_out, attn_w_qkv, attn_b_qkv,
              attn_sinks, attn_w_o, peer_w_q, peer_sub_keys, peer_u, peer_v):
    for layer in range(DEPTH):
        i = layer // 2
        h = rmsnorm(x, norm_mix[layer])
        if layer % 2 == 0:
            z = h @ mix_w_in[i]
            y_ssm = s5_mixer(z[..., :S5_WIDTH], s5_lambda_re[i], s5_lambda_im[i], s5_log_step[i],
                             s5_b_re[i], s5_b_im[i], s5_c_re[i], s5_c_im[i], s5_d[i], s5_w_glu[i])
            y_conv = conv_module(z[..., S5_WIDTH:], conv_dw_w[i], conv_dw_b[i], conv_ln_g[i], conv_ln_b[i])
            x = x + jnp.concatenate([y_ssm, y_conv], axis=-1) @ mix_w_out[i]
        else:
            x = x + sliding_window_attention(h, attn_w_qkv[i], attn_b_qkv[i], attn_sinks[i], attn_w_o[i])
        x = x + peer(rmsnorm(x, norm_ffn[layer]), peer_w_q[layer], peer_sub_keys[layer],
                     peer_u[layer], peer_v[layer])
    return rmsnorm(x, norm_final)
```

```python
import functools
import math

import jax
import jax.numpy as jnp
from jax import lax
from jax.experimental import pallas as pl
from jax.experimental.pallas import tpu as pltpu
from jax.experimental.pallas import tpu_sc as plsc

f32 = jnp.float32

D_MODEL = 1024
S5_WIDTH = 512
S5_GROUP = 16
S5_GROUPS = 32
S5_STATE = 64
CONV_WIDTH = 512
CONV_K = 31
HEAD_DIM = 64
N_Q_HEADS = 16
N_KV_HEADS = 2
Q_PER_KV = 8
WINDOW = 128
ROPE_THETA = 500000.0
ROPE_DIM = 16
PEER_HEADS = 8
PEER_N_KEYS = 128
PEER_HALF = 128
PEER_TOPK = 16
NORM_EPS = 1e-6

SC_CORES = 2
SC_SUBCORES = 16
SC_LANES = 16
SC_WORKERS = SC_CORES * SC_SUBCORES
N_SLOTS = PEER_HEADS * PEER_TOPK
ROW_BATCH = 32
N_BATCH = N_SLOTS // ROW_BATCH


def _sc_params():
    return pltpu.CompilerParams(needs_layout_passes=False)


def _peer_u_stage(x, ids, u_tab):
    t_total, d = x.shape
    tpw = t_total // SC_WORKERS
    mesh = plsc.VectorSubcoreMesh(core_axis_name="c", subcore_axis_name="s")

    @functools.partial(
        pl.kernel, mesh=mesh,
        out_type=jax.ShapeDtypeStruct((t_total, N_SLOTS), f32),
        scratch_types=[
            pltpu.VMEM((N_SLOTS,), jnp.int32),
            pltpu.VMEM((d,), f32),
            pltpu.VMEM((2, ROW_BATCH, d), f32),
            pltpu.VMEM((N_SLOTS,), f32),
            pltpu.VMEM((SC_LANES, SC_LANES), f32),
            pltpu.SemaphoreType.DMA((2,)),
        ],
        compiler_params=_sc_params())
    def k(x_hbm, ids_hbm, u_hbm, a_hbm, idx_v, x_v, rows_v, a_v, tmp_v, sems):
        wid = lax.axis_index("s") * SC_CORES + lax.axis_index("c")
        base = wid * tpw
        iota = lax.iota(jnp.int32, SC_LANES)

        def gather(b, slot):
            return pltpu.make_async_copy(
                u_hbm.at[idx_v.at[pl.ds(b * ROW_BATCH, ROW_BATCH)]], rows_v.at[slot], sems.at[slot])

        def token_body(i, carry):
            t = base + i
            pltpu.sync_copy(ids_hbm.at[t], idx_v)
            pltpu.sync_copy(x_hbm.at[t], x_v)
            gather(0, 0).start()
            for b in range(N_BATCH):
                slot = b % 2
                if b + 1 < N_BATCH:
                    gather(b + 1, 1 - slot).start()
                gather(b, slot).wait()
                for g in range(ROW_BATCH // SC_LANES):
                    def cbody(c, accs):
                        off = pl.multiple_of(c * SC_LANES, SC_LANES)
                        xc = x_v[pl.ds(off, SC_LANES)]
                        return tuple(accs[j] + xc * rows_v[slot, g * SC_LANES + j, pl.ds(off, SC_LANES)]
                                     for j in range(SC_LANES))
                    accs = lax.fori_loop(0, d // SC_LANES, cbody,
                                         tuple(jnp.zeros((SC_LANES,), f32) for _ in range(SC_LANES)))
                    for j in range(SC_LANES):
                        tmp_v[j, :] = accs[j]
                    r = jnp.zeros((SC_LANES,), f32)
                    for l in range(SC_LANES):
                        r = r + plsc.load_gather(tmp_v, [iota, jnp.full((SC_LANES,), l, jnp.int32)])
                    a_v[pl.ds(b * ROW_BATCH + g * SC_LANES, SC_LANES)] = r
            pltpu.sync_copy(a_v, a_hbm.at[t])
            return carry

        lax.fori_loop(0, tpw, token_body, 0)

    return k(x, ids, u_tab)


def _peer_v_stage(w, ids, v_tab):
    t_total = w.shape[0]
    d = v_tab.shape[1]
    tpw = t_total // SC_WORKERS
    grp = 8
    mesh = plsc.VectorSubcoreMesh(core_axis_name="c", subcore_axis_name="s")

    @functools.partial(
        pl.kernel, mesh=mesh,
        out_type=jax.ShapeDtypeStruct((t_total, d), f32),
        scratch_types=[
            pltpu.VMEM((N_SLOTS,), jnp.int32),
            pltpu.VMEM((N_SLOTS,), f32),
            pltpu.VMEM((2, ROW_BATCH, d), f32),
            pltpu.VMEM((d,), f32),
            pltpu.SemaphoreType.DMA((2,)),
        ],
        compiler_params=_sc_params())
    def k(w_hbm, ids_hbm, v_hbm, o_hbm, idx_v, w_v, rows_v, o_v, sems):
        wid = lax.axis_index("s") * SC_CORES + lax.axis_index("c")
        base = wid * tpw

        def gather(b, slot):
            return pltpu.make_async_copy(
                v_hbm.at[idx_v.at[pl.ds(b * ROW_BATCH, ROW_BATCH)]], rows_v.at[slot], sems.at[slot])

        def token_body(i, carry):
            t = base + i
            pltpu.sync_copy(ids_hbm.at[t], idx_v)
            pltpu.sync_copy(w_hbm.at[t], w_v)
            gather(0, 0).start()
            for b in range(N_BATCH):
                slot = b % 2
                if b + 1 < N_BATCH:
                    gather(b + 1, 1 - slot).start()
                gather(b, slot).wait()
                for g in range(ROW_BATCH // grp):
                    sp = [plsc.load_gather(w_v, [jnp.full((SC_LANES,), b * ROW_BATCH + g * grp + j, jnp.int32)])
                          for j in range(grp)]
                    first = (b == 0 and g == 0)

                    def cbody(c, carry2):
                        off = pl.multiple_of(c * SC_LANES, SC_LANES)
                        if first:
                            acc = sp[0] * rows_v[slot, g * grp, pl.ds(off, SC_LANES)]
                            js = range(1, grp)
                        else:
                            acc = o_v[pl.ds(off, SC_LANES)]
                            js = range(grp)
                        for j in js:
                            acc = acc + sp[j] * rows_v[slot, g * grp + j, pl.ds(off, SC_LANES)]
                        o_v[pl.ds(off, SC_LANES)] = acc
                        return carry2
                    lax.fori_loop(0, d // SC_LANES, cbody, 0)
            pltpu.sync_copy(o_v, o_hbm.at[t])
            return carry

        lax.fori_loop(0, tpw, token_body, 0)

    return k(w, ids, v_tab)


def _rmsnorm_kernel(x_ref, g_ref, o_ref):
    x = x_ref[...]
    ms = jnp.mean(x * x, axis=-1, keepdims=True)
    o_ref[...] = x * lax.rsqrt(ms + NORM_EPS) * g_ref[...]


def _rmsnorm_tc(x2d, g):
    t, d = x2d.shape
    tm = 1024
    return pl.pallas_call(
        _rmsnorm_kernel,
        grid=(t // tm,),
        in_specs=[pl.BlockSpec((tm, d), lambda i: (i, 0)), pl.BlockSpec((1, d), lambda i: (0, 0))],
        out_specs=pl.BlockSpec((tm, d), lambda i: (i, 0)),
        out_shape=jax.ShapeDtypeStruct((t, d), f32),
    )(x2d, g.reshape(1, d))


def _rmsnorm(x, g):
    return x * lax.rsqrt(jnp.mean(x * x, axis=-1, keepdims=True) + NORM_EPS) * g


def _s5(u, lam_re, lam_im, log_step, b_re, b_im, c_re, c_im, d, w_glu):
    bsz, seq, _ = u.shape
    uf = u.reshape(bsz, seq, S5_GROUPS, S5_GROUP)
    lam = lax.complex(lam_re, lam_im)
    dt = jnp.exp(log_step)[:, None]
    a_bar = jnp.exp(lam * dt)
    b = lax.complex(b_re, b_im)
    b_bar = ((a_bar - 1.0) / lam)[..., None] * b
    bu = lax.complex(jnp.einsum('blgh,gph->blgp', uf, jnp.real(b_bar)),
                     jnp.einsum('blgh,gph->blgp', uf, jnp.imag(b_bar)))
    a_seq = jnp.broadcast_to(a_bar, (1, seq) + a_bar.shape)

    def combine(left, right):
        a_l, b_l = left
        a_r, b_r = right
        return a_r * a_l, a_r * b_l + b_r

    _, h = lax.associative_scan(combine, (a_seq, bu), axis=1)
    y = (jnp.einsum('blgp,ghp->blgh', jnp.real(h), c_re)
         - jnp.einsum('blgp,ghp->blgh', jnp.imag(h), c_im) + d * uf)
    y = jax.nn.gelu(y, approximate=False).reshape(bsz, seq, S5_WIDTH)
    return y * jax.nn.sigmoid(y @ w_glu)


def _conv(z, dw_w, dw_b, ln_g, ln_b):
    val, gate = jnp.split(z, 2, axis=-1)
    v = val * jax.nn.sigmoid(gate)
    y = lax.conv_general_dilated(v, dw_w[:, None, :], window_strides=(1,),
                                 padding=[(CONV_K - 1, 0)],
                                 dimension_numbers=('NWC', 'WIO', 'NWC'),
                                 feature_group_count=CONV_WIDTH) + dw_b
    mu = jnp.mean(y, axis=-1, keepdims=True)
    var = jnp.mean(jnp.square(y - mu), axis=-1, keepdims=True)
    yn = (y - mu) * lax.rsqrt(var + NORM_EPS) * ln_g + ln_b
    return jax.nn.silu(yn)


def _rope(t, pos):
    inv = jnp.power(ROPE_THETA, -jnp.arange(0, ROPE_DIM, 2, dtype=f32) / ROPE_DIM)
    ang = pos[:, None] * inv[None, :]
    cos = jnp.cos(ang)[None, :, None, :]
    sin = jnp.sin(ang)[None, :, None, :]
    tr = t[..., :ROPE_DIM]
    x1, x2 = tr[..., :ROPE_DIM // 2], tr[..., ROPE_DIM // 2:]
    rot = jnp.concatenate([x1 * cos - x2 * sin, x2 * cos + x1 * sin], axis=-1)
    return jnp.concatenate([rot, t[..., ROPE_DIM:]], axis=-1)


def _swa(h, w_qkv, b_qkv, sinks, w_o):
    bsz, seq, _ = h.shape
    qkv = h @ w_qkv + b_qkv
    q, k, v = jnp.split(qkv, [N_Q_HEADS * HEAD_DIM, (N_Q_HEADS + N_KV_HEADS) * HEAD_DIM], axis=-1)
    pos = jnp.arange(seq, dtype=f32)
    q = _rope(q.reshape(bsz, seq, N_Q_HEADS, HEAD_DIM), pos)
    k = _rope(k.reshape(bsz, seq, N_KV_HEADS, HEAD_DIM), pos)
    v = v.reshape(bsz, seq, N_KV_HEADS, HEAD_DIM)
    nb = seq // WINDOW
    q = q.reshape(bsz, nb, WINDOW, N_KV_HEADS, Q_PER_KV, HEAD_DIM)
    k = k.reshape(bsz, nb, WINDOW, N_KV_HEADS, HEAD_DIM)
    v = v.reshape(bsz, nb, WINDOW, N_KV_HEADS, HEAD_DIM)

    def with_prev(t):
        prev = jnp.pad(t[:, :-1], ((0, 0), (1, 0), (0, 0), (0, 0), (0, 0)))
        return jnp.concatenate([prev, t], axis=2)

    kb, vb = with_prev(k), with_prev(v)
    s = jnp.einsum('bnqkgd,bnskd->bnkgqs', q, kb, preferred_element_type=f32) * (HEAD_DIM ** -0.5)
    qi = jnp.arange(WINDOW)[:, None]
    si = jnp.arange(2 * WINDOW)[None, :]
    diff = WINDOW + qi - si
    band = (diff >= 0) & (diff < WINDOW)
    has_prev = (jnp.arange(nb)[:, None, None] > 0) | (si[None] >= WINDOW)
    valid = band[None] & has_prev
    s = jnp.where(valid[None, :, None, None], s, -jnp.inf)
    sink = sinks.reshape(N_KV_HEADS, Q_PER_KV)[None, None, :, :, None, None]
    m = jnp.maximum(jnp.max(s, axis=-1, keepdims=True), sink)
    p = jnp.exp(s - m)
    denom = jnp.sum(p, axis=-1, keepdims=True) + jnp.exp(sink - m)
    probs = p / denom
    o = jnp.einsum('bnkgqs,bnskd->bnqkgd', probs, vb)
    o = o.reshape(bsz, seq, N_Q_HEADS * HEAD_DIM)
    return o @ w_o


def _peer(h, w_q, sub_keys, u_tab, v_tab):
    bsz, seq, dm = h.shape
    t = bsz * seq
    hf = h.reshape(t, dm)
    q = (hf @ w_q).reshape(t, PEER_HEADS, 2, PEER_HALF)
    scores = jnp.einsum('thcd,hckd->thck', q, sub_keys, preferred_element_type=f32)
    top_s, top_i = lax.top_k(scores, PEER_TOPK)
    cand_s = top_s[:, :, 0, :, None] + top_s[:, :, 1, None, :]
    cand_i = top_i[:, :, 0, :, None] * PEER_N_KEYS + top_i[:, :, 1, None, :]
    best_s, best_c = lax.top_k(cand_s.reshape(t, PEER_HEADS, PEER_TOPK * PEER_TOPK), PEER_TOPK)
    ids = jnp.take_along_axis(cand_i.reshape(t, PEER_HEADS, PEER_TOPK * PEER_TOPK), best_c, axis=-1)
    gates = jax.nn.softmax(best_s, axis=-1)
    ids = ids.reshape(t, N_SLOTS).astype(jnp.int32)
    gates = gates.reshape(t, N_SLOTS)
    a = _peer_u_stage(hf, ids, u_tab)
    w = jax.nn.gelu(a, approximate=False) * gates
    out = _peer_v_stage(w, ids, v_tab)
    return out.reshape(bsz, seq, dm)


def kernel(x, norm_mix, norm_ffn, norm_final, mix_w_in, s5_lambda_re, s5_lambda_im, s5_log_step, s5_b_re, s5_b_im, s5_c_re, s5_c_im, s5_d, s5_w_glu, conv_dw_w, conv_dw_b, conv_ln_g, conv_ln_b, mix_w_out, attn_w_qkv, attn_b_qkv, attn_sinks, attn_w_o, peer_w_q, peer_sub_keys, peer_u, peer_v):
    bsz, seq, dm = x.shape
    for layer in range(2):
        h = _rmsnorm(x, norm_mix[layer])
        if layer == 0:
            z = h @ mix_w_in[0]
            y_ssm = _s5(z[..., :S5_WIDTH], s5_lambda_re[0], s5_lambda_im[0], s5_log_step[0],
                        s5_b_re[0], s5_b_im[0], s5_c_re[0], s5_c_im[0], s5_d[0], s5_w_glu[0])
            y_conv = _conv(z[..., S5_WIDTH:], conv_dw_w[0], conv_dw_b[0], conv_ln_g[0], conv_ln_b[0])
            x = x + jnp.concatenate([y_ssm, y_conv], axis=-1) @ mix_w_out[0]
        else:
            x = x + _swa(h, attn_w_qkv[0], attn_b_qkv[0], attn_sinks[0], attn_w_o[0])
        x = x + _peer(_rmsnorm(x, norm_ffn[layer]), peer_w_q[layer], peer_sub_keys[layer],
                      peer_u[layer], peer_v[layer])
    return _rmsnorm_tc(x.reshape(bsz * seq, dm), norm_final).reshape(bsz, seq, dm)
```

```python
import functools

import jax
import jax.numpy as jnp
from jax import lax
from jax.experimental import pallas as pl
from jax.experimental.pallas import tpu as pltpu
from jax.experimental.pallas import tpu_sc as plsc

f32 = jnp.float32
bf16 = jnp.bfloat16
i32 = jnp.int32

D_MODEL = 1024
S5_WIDTH = 512
S5_GROUP = 16
S5_GROUPS = 32
S5_STATE = 64
S5_LANES = S5_GROUPS * S5_STATE
CONV_WIDTH = 512
CONV_K = 31
HEAD_DIM = 64
N_Q_HEADS = 16
N_KV_HEADS = 2
Q_PER_KV = 8
Q_WIDTH = N_Q_HEADS * HEAD_DIM
KV_WIDTH = 2 * N_KV_HEADS * HEAD_DIM
QKV_WIDTH = Q_WIDTH + KV_WIDTH
WINDOW = 128
ROPE_THETA = 500000.0
ROPE_DIM = 16
PEER_HEADS = 8
PEER_N_KEYS = 128
PEER_HALF = 128
PEER_TOPK = 16
N_SLOTS = PEER_HEADS * PEER_TOPK
NORM_EPS = 1e-6

LANES = 128
SUBLANES = 8
TC_VMEM_LIMIT = 48 * 1024 * 1024

SC_CORES = 2
SC_SUBCORES = 16
SC_LANES = 16
SC_WORKERS = SC_CORES * SC_SUBCORES
ROW_BATCH = 32
N_BATCH = N_SLOTS // ROW_BATCH

ROW_TILE = 512
MIX_TILE = 256
SCAN_TILE = 128
TOPK_TILE = 128
CONV_HALO = 32


def _tc_params(*sem):
    return pltpu.CompilerParams(dimension_semantics=sem, vmem_limit_bytes=TC_VMEM_LIMIT)


def _rms(x, g):
    return x * lax.rsqrt(jnp.mean(x * x, axis=-1, keepdims=True) + NORM_EPS) * g


def _sigmoid(x):
    return 1.0 / (1.0 + jnp.exp(-x))


def _gelu(x):
    return 0.5 * x * (1.0 + lax.erf(x * 0.7071067811865476))


def _dot(a, b):
    return jnp.dot(a, b, preferred_element_type=f32)


def _mix_in_kernel(x_ref, g_ref, w_ref, z_ref):
    h = _rms(x_ref[...], g_ref[...])
    z_ref[...] = _dot(h.astype(bf16), w_ref[...])


def _mix_in(x2d, g, w):
    t, d = x2d.shape
    n = w.shape[1]
    return pl.pallas_call(
        _mix_in_kernel,
        grid=(t // ROW_TILE,),
        in_specs=[pl.BlockSpec((ROW_TILE, d), lambda i: (i, 0)),
                  pl.BlockSpec((1, d), lambda i: (0, 0)),
                  pl.BlockSpec((d, n), lambda i: (0, 0))],
        out_specs=pl.BlockSpec((ROW_TILE, n), lambda i: (i, 0)),
        out_shape=jax.ShapeDtypeStruct((t, n), f32),
        compiler_params=_tc_params("parallel"),
        name="mix_in",
    )(x2d, g.reshape(1, d), w.astype(bf16))


def _peer_q_kernel(x_ref, g_ref, w_ref, h_ref, q_ref):
    h = _rms(x_ref[...], g_ref[...])
    h_ref[...] = h
    q = _dot(h.astype(bf16), w_ref[...])
    for c in range(2 * PEER_HEADS):
        q_ref[c] = q[:, c * PEER_HALF:(c + 1) * PEER_HALF]


def _peer_q(x2d, g, w_q):
    t, d = x2d.shape
    nq = 2 * PEER_HEADS
    return pl.pallas_call(
        _peer_q_kernel,
        grid=(t // ROW_TILE,),
        in_specs=[pl.BlockSpec((ROW_TILE, d), lambda i: (i, 0)),
                  pl.BlockSpec((1, d), lambda i: (0, 0)),
                  pl.BlockSpec((d, nq * PEER_HALF), lambda i: (0, 0))],
        out_specs=[pl.BlockSpec((ROW_TILE, d), lambda i: (i, 0)),
                   pl.BlockSpec((nq, ROW_TILE, PEER_HALF), lambda i: (0, i, 0))],
        out_shape=[jax.ShapeDtypeStruct((t, d), f32),
                   jax.ShapeDtypeStruct((nq, t, PEER_HALF), f32)],
        compiler_params=_tc_params("parallel"),
        name="peer_q",
    )(x2d, g.reshape(1, d), w_q.astype(bf16))


def _qkv_kernel(x_ref, r_ref, g_ref, w_ref, b_ref, cos_ref, sin_ref, xo_ref, qkv_ref):
    x = x_ref[...] + r_ref[...]
    xo_ref[...] = x
    h = _rms(x, g_ref[...])
    qkv = _dot(h.astype(bf16), w_ref[...]) + b_ref[...]
    cos = cos_ref[...]
    sin = sin_ref[...]
    lane = lax.broadcasted_iota(i32, cos.shape, 1)
    low = (lane % HEAD_DIM) < (ROPE_DIM // 2)
    n_rot = (Q_WIDTH + N_KV_HEADS * HEAD_DIM) // LANES
    for c in range(QKV_WIDTH // LANES):
        t = qkv[:, c * LANES:(c + 1) * LANES]
        if c < n_rot:
            partner = jnp.where(low, pltpu.roll(t, LANES - ROPE_DIM // 2, axis=1),
                                pltpu.roll(t, ROPE_DIM // 2, axis=1))
            t = t * cos + partner * sin
        if c < Q_WIDTH // LANES:
            t = t * (HEAD_DIM ** -0.5)
        qkv_ref[:, c * LANES:(c + 1) * LANES] = t


def _rope_tables(seq):
    inv = jnp.power(ROPE_THETA, -jnp.arange(0, ROPE_DIM, 2, dtype=f32) / ROPE_DIM)
    ang = jnp.arange(seq, dtype=f32)[:, None] * inv[None, :]
    cos, sin = jnp.cos(ang), jnp.sin(ang)
    pad = HEAD_DIM - ROPE_DIM
    cos_h = jnp.concatenate([cos, cos, jnp.ones((seq, pad), f32)], axis=1)
    sin_h = jnp.concatenate([-sin, sin, jnp.zeros((seq, pad), f32)], axis=1)
    return jnp.tile(cos_h, (1, LANES // HEAD_DIM)), jnp.tile(sin_h, (1, LANES // HEAD_DIM))


def _qkv(x2d, r2d, g, w, b, seq):
    t, d = x2d.shape
    n = w.shape[1]
    cos, sin = _rope_tables(seq)
    blocks_per_seq = seq // ROW_TILE
    return pl.pallas_call(
        _qkv_kernel,
        grid=(t // ROW_TILE,),
        in_specs=[pl.BlockSpec((ROW_TILE, d), lambda i: (i, 0)),
                  pl.BlockSpec((ROW_TILE, d), lambda i: (i, 0)),
                  pl.BlockSpec((1, d), lambda i: (0, 0)),
                  pl.BlockSpec((d, n), lambda i: (0, 0)),
                  pl.BlockSpec((1, n), lambda i: (0, 0)),
                  pl.BlockSpec((ROW_TILE, LANES), lambda i: (i % blocks_per_seq, 0)),
                  pl.BlockSpec((ROW_TILE, LANES), lambda i: (i % blocks_per_seq, 0))],
        out_specs=[pl.BlockSpec((ROW_TILE, d), lambda i: (i, 0)),
                   pl.BlockSpec((ROW_TILE, n), lambda i: (i, 0))],
        out_shape=[jax.ShapeDtypeStruct((t, d), f32), jax.ShapeDtypeStruct((t, n), f32)],
        compiler_params=_tc_params("parallel"),
        name="qkv",
    )(x2d, r2d, g.reshape(1, d), w.astype(bf16), b.reshape(1, n), cos, sin)


def _final_norm_kernel(x_ref, r_ref, g_ref, o_ref):
    o_ref[...] = _rms(x_ref[...] + r_ref[...], g_ref[...])


def _final_norm(x2d, r2d, g):
    t, d = x2d.shape
    spec = pl.BlockSpec((ROW_TILE, d), lambda i: (i, 0))
    return pl.pallas_call(
        _final_norm_kernel,
        grid=(t // ROW_TILE,),
        in_specs=[spec, spec, pl.BlockSpec((1, d), lambda i: (0, 0))],
        out_specs=spec,
        out_shape=jax.ShapeDtypeStruct((t, d), f32),
        compiler_params=_tc_params("parallel"),
        name="final_norm",
    )(x2d, r2d, g.reshape(1, d))


def _shift_rows(v, d, row):
    if d % SUBLANES == 0:
        return jnp.concatenate([jnp.zeros((d, v.shape[1]), v.dtype), v[:v.shape[0] - d]], axis=0)
    return jnp.where(row >= d, pltpu.roll(v, d, axis=0), 0.0)


def _mixer_kernel(z_ref, x_ref, apr_ref, api_ref, bre_ref, bim_ref, cre_ref, cim_ref, dvec_ref, wglu_ref,
                  dww_ref, dwb_ref, lng_ref, lnb_ref, wout_ref, o_ref,
                  carry_re, carry_im, y_s, vbuf):
    tb = z_ref.shape[0]
    n_chunks = S5_LANES // LANES
    steps = [1 << s for s in range(SCAN_TILE.bit_length() - 1)]

    @pl.when(pl.program_id(1) == 0)
    def _():
        carry_re[...] = jnp.zeros_like(carry_re)
        carry_im[...] = jnp.zeros_like(carry_im)
        vbuf[0:CONV_HALO, :] = jnp.zeros((CONV_HALO, CONV_WIDTH), f32)

    u = z_ref[:, 0:S5_WIDTH]
    ub = u.astype(bf16)
    y_s[...] = u * dvec_ref[...]
    row = lax.broadcasted_iota(i32, (SCAN_TILE, LANES), 0)

    def chunk_body(j, carry):
        bu_re = _dot(ub, bre_ref[j])
        bu_im = _dot(ub, bim_ref[j])
        apr = apr_ref[j]
        api = api_ref[j]
        c_re = carry_re[j]
        c_im = carry_im[j]
        h_re_tiles, h_im_tiles = [], []
        for r in range(tb // SCAN_TILE):
            hr = bu_re[r * SCAN_TILE:(r + 1) * SCAN_TILE]
            hi = bu_im[r * SCAN_TILE:(r + 1) * SCAN_TILE]
            a_re, a_im = apr[0:1], api[0:1]
            first = row == 0
            hr = hr + jnp.where(first, a_re * c_re - a_im * c_im, 0.0)
            hi = hi + jnp.where(first, a_re * c_im + a_im * c_re, 0.0)
            for s, d in enumerate(steps):
                a_re, a_im = apr[s:s + 1], api[s:s + 1]
                sr = _shift_rows(hr, d, row)
                si = _shift_rows(hi, d, row)
                hr, hi = hr + (a_re * sr - a_im * si), hi + (a_re * si + a_im * sr)
            c_re = hr[SCAN_TILE - 1:SCAN_TILE]
            c_im = hi[SCAN_TILE - 1:SCAN_TILE]
            h_re_tiles.append(hr.astype(bf16))
            h_im_tiles.append(hi.astype(bf16))
        carry_re[j] = c_re
        carry_im[j] = c_im
        h_re = jnp.concatenate(h_re_tiles, axis=0)
        h_im = jnp.concatenate(h_im_tiles, axis=0)
        y_s[...] += _dot(h_re, cre_ref[j]) - _dot(h_im, cim_ref[j])
        return carry

    lax.fori_loop(0, n_chunks, chunk_body, 0)
    y = _gelu(y_s[...])
    y_ssm = y * _sigmoid(_dot(y.astype(bf16), wglu_ref[...]))

    v = z_ref[:, S5_WIDTH:S5_WIDTH + CONV_WIDTH] * _sigmoid(z_ref[:, S5_WIDTH + CONV_WIDTH:S5_WIDTH + 2 * CONV_WIDTH])
    vbuf[CONV_HALO:CONV_HALO + tb, :] = v
    acc = jnp.broadcast_to(dwb_ref[...], (tb, CONV_WIDTH))
    for k in range(CONV_K):
        acc = acc + dww_ref[k:k + 1, :] * vbuf[pl.ds(CONV_HALO - (CONV_K - 1) + k, tb), :]
    vbuf[0:CONV_HALO, :] = vbuf[tb:tb + CONV_HALO, :]
    mu = jnp.mean(acc, axis=-1, keepdims=True)
    cen = acc - mu
    var = jnp.mean(cen * cen, axis=-1, keepdims=True)
    yn = cen * lax.rsqrt(var + NORM_EPS) * lng_ref[...] + lnb_ref[...]
    y_conv = yn * _sigmoid(yn)

    o_ref[...] = (x_ref[...] + _dot(y_ssm.astype(bf16), wout_ref[0:S5_WIDTH, :])
                  + _dot(y_conv.astype(bf16), wout_ref[S5_WIDTH:S5_WIDTH + CONV_WIDTH, :]))


def _s5_tables(lam_re, lam_im, log_step, b_re, b_im, c_re, c_im):
    n_chunks = S5_LANES // LANES
    g_per_chunk = LANES // S5_STATE
    lam = lax.complex(lam_re, lam_im)
    dt = jnp.exp(log_step)[:, None]
    a_bar = jnp.exp(lam * dt)
    b_bar = ((a_bar - 1.0) / lam)[..., None] * lax.complex(b_re, b_im)
    n_steps = SCAN_TILE.bit_length() - 1
    powers = jnp.stack([jnp.exp(lam * dt * float(1 << s)) for s in range(n_steps)]
                       + [jnp.ones_like(a_bar)] * (SUBLANES - n_steps))
    powers = powers.reshape(SUBLANES, n_chunks, LANES).transpose(1, 0, 2)
    eye = jnp.eye(S5_GROUPS, dtype=f32)

    def in_mat(b):
        m = jnp.einsum('gph,gk->ghkp', b, eye).reshape(S5_WIDTH, S5_LANES)
        return m.reshape(S5_WIDTH, n_chunks, LANES).transpose(1, 0, 2).astype(bf16)

    def out_mat(c):
        m = jnp.einsum('ghp,gk->gpkh', c, eye).reshape(S5_LANES, S5_WIDTH)
        return m.reshape(n_chunks, LANES, S5_WIDTH).astype(bf16)

    del g_per_chunk
    return (jnp.real(powers), jnp.imag(powers), in_mat(jnp.real(b_bar)), in_mat(jnp.imag(b_bar)),
            out_mat(c_re), out_mat(c_im))


def _mixer(z, x, s5_tabs, d_vec, w_glu, dw_w, dw_b, ln_g, ln_b, w_out):
    bsz, seq, zw = z.shape
    d = x.shape[-1]
    apr, api, bre, bim, cre, cim = s5_tabs
    n_chunks = S5_LANES // LANES
    tb = MIX_TILE
    const2 = lambda b, n: (0, 0)
    const3 = lambda b, n: (0, 0, 0)
    return pl.pallas_call(
        _mixer_kernel,
        grid=(bsz, seq // tb),
        in_specs=[pl.BlockSpec((None, tb, zw), lambda b, n: (b, n, 0)),
                  pl.BlockSpec((None, tb, d), lambda b, n: (b, n, 0)),
                  pl.BlockSpec((n_chunks, SUBLANES, LANES), const3),
                  pl.BlockSpec((n_chunks, SUBLANES, LANES), const3),
                  pl.BlockSpec((n_chunks, S5_WIDTH, LANES), const3),
                  pl.BlockSpec((n_chunks, S5_WIDTH, LANES), const3),
                  pl.BlockSpec((n_chunks, LANES, S5_WIDTH), const3),
                  pl.BlockSpec((n_chunks, LANES, S5_WIDTH), const3),
                  pl.BlockSpec((1, S5_WIDTH), const2),
                  pl.BlockSpec((S5_WIDTH, S5_WIDTH), const2),
                  pl.BlockSpec((CONV_K, CONV_WIDTH), const2),
                  pl.BlockSpec((1, CONV_WIDTH), const2),
                  pl.BlockSpec((1, CONV_WIDTH), const2),
                  pl.BlockSpec((1, CONV_WIDTH), const2),
                  pl.BlockSpec((S5_WIDTH + CONV_WIDTH, d), const2)],
        out_specs=pl.BlockSpec((None, tb, d), lambda b, n: (b, n, 0)),
        out_shape=jax.ShapeDtypeStruct((bsz, seq, d), f32),
        scratch_shapes=[pltpu.VMEM((n_chunks, 1, LANES), f32),
                        pltpu.VMEM((n_chunks, 1, LANES), f32),
                        pltpu.VMEM((tb, S5_WIDTH), f32),
                        pltpu.VMEM((tb + CONV_HALO, CONV_WIDTH), f32)],
        compiler_params=_tc_params("parallel", "arbitrary"),
        name="mixer",
    )(z, x, apr, api, bre, bim, cre, cim, d_vec.reshape(1, S5_WIDTH), w_glu.astype(bf16),
      dw_w, dw_b.reshape(1, -1), ln_g.reshape(1, -1), ln_b.reshape(1, -1), w_out.astype(bf16))


def _attn_kernel(q_ref, kvc_ref, kvp_ref, x_ref, sink_ref, wo_ref, o_ref, o_s):
    n = pl.program_id(1)
    qi = lax.broadcasted_iota(i32, (WINDOW, 2 * WINDOW), 0)
    si = lax.broadcasted_iota(i32, (WINDOW, 2 * WINDOW), 1)
    first_key = jnp.where(n > 0, 0, WINDOW)
    valid = (si > qi) & (si <= qi + WINDOW) & (si >= first_key)
    kv = jnp.concatenate([kvp_ref[...], kvc_ref[...]], axis=0).astype(bf16)
    for kh in range(N_KV_HEADS):
        k = kv[:, kh * HEAD_DIM:(kh + 1) * HEAD_DIM]
        v = kv[:, (N_KV_HEADS + kh) * HEAD_DIM:(N_KV_HEADS + kh + 1) * HEAD_DIM]
        for g in range(Q_PER_KV):
            h = kh * Q_PER_KV + g
            q = q_ref[:, h * HEAD_DIM:(h + 1) * HEAD_DIM].astype(bf16)
            s = lax.dot_general(q, k, (((1,), (1,)), ((), ())), preferred_element_type=f32)
            s = jnp.where(valid, s, -jnp.inf)
            sink = sink_ref[0:1, h:h + 1]
            m = jnp.maximum(jnp.max(s, axis=-1, keepdims=True), sink)
            p = jnp.exp(s - m)
            denom = jnp.sum(p, axis=-1, keepdims=True) + jnp.exp(sink - m)
            probs = (p / denom).astype(bf16)
            o_s[:, h * HEAD_DIM:(h + 1) * HEAD_DIM] = _dot(probs, v)
    o_ref[...] = x_ref[...] + _dot(o_s[...].astype(bf16), wo_ref[...])


def _attention(qkv, x, sinks, w_o):
    bsz, seq, _ = qkv.shape
    d = x.shape[-1]
    kv_block = Q_WIDTH // KV_WIDTH
    return pl.pallas_call(
        _attn_kernel,
        grid=(bsz, seq // WINDOW),
        in_specs=[pl.BlockSpec((None, WINDOW, Q_WIDTH), lambda b, n: (b, n, 0)),
                  pl.BlockSpec((None, WINDOW, KV_WIDTH), lambda b, n: (b, n, kv_block)),
                  pl.BlockSpec((None, WINDOW, KV_WIDTH), lambda b, n: (b, jnp.maximum(n - 1, 0), kv_block)),
                  pl.BlockSpec((None, WINDOW, d), lambda b, n: (b, n, 0)),
                  pl.BlockSpec((1, N_Q_HEADS), lambda b, n: (0, 0)),
                  pl.BlockSpec((Q_WIDTH, d), lambda b, n: (0, 0))],
        out_specs=pl.BlockSpec((None, WINDOW, d), lambda b, n: (b, n, 0)),
        out_shape=jax.ShapeDtypeStruct((bsz, seq, d), f32),
        scratch_shapes=[pltpu.VMEM((WINDOW, Q_WIDTH), f32)],
        compiler_params=_tc_params("parallel", "parallel"),
        name="attention",
    )(qkv, qkv, qkv, x, sinks.reshape(1, N_Q_HEADS), w_o.astype(bf16))


def _top16(s, ids):
    m_rows = s.shape[0]
    pos = lax.broadcasted_iota(i32, s.shape, 0).astype(f32)
    vals, outs = [], []
    for _ in range(PEER_TOPK):
        mx = jnp.max(s, axis=0, keepdims=True)
        first = jnp.min(jnp.where(s == mx, pos, float(m_rows)), axis=0, keepdims=True)
        hit = pos == first
        vals.append(mx)
        outs.append(jnp.sum(jnp.where(hit, ids, 0.0), axis=0, keepdims=True))
        s = jnp.where(hit, -jnp.inf, s)
    return jnp.concatenate(vals, axis=0), jnp.concatenate(outs, axis=0)


def _topk_kernel(q_ref, keys_ref, ids_ref, gates_ref, ids_s, gates_s):
    tb = q_ref.shape[1]
    key_pos = lax.broadcasted_iota(i32, (PEER_N_KEYS, tb), 0).astype(f32)

    def head_body(h, carry):
        tops = []
        for c in range(2):
            q = q_ref[2 * h + c].astype(bf16)
            s = lax.dot_general(keys_ref[2 * h + c], q, (((1,), (1,)), ((), ())),
                                preferred_element_type=f32)
            tops.append(_top16(s, key_pos))
        (s0, i0), (s1, i1) = tops
        cand_s = jnp.concatenate([s0[i:i + 1] + s1 for i in range(PEER_TOPK)], axis=0)
        cand_i = jnp.concatenate([i0[i:i + 1] * float(PEER_N_KEYS) + i1 for i in range(PEER_TOPK)], axis=0)
        best_s, best_i = _top16(cand_s, cand_i)
        e = jnp.exp(best_s - best_s[0:1])
        off = pl.multiple_of(h * PEER_TOPK, PEER_TOPK)
        gates_s[pl.ds(off, PEER_TOPK), :] = e / jnp.sum(e, axis=0, keepdims=True)
        ids_s[pl.ds(off, PEER_TOPK), :] = best_i.astype(i32)
        return carry

    lax.fori_loop(0, PEER_HEADS, head_body, 0)
    ids_ref[...] = ids_s[...].T
    gates_ref[...] = gates_s[...].T


def _peer_topk(q3, sub_keys):
    nq, t, half = q3.shape
    keys = sub_keys.reshape(nq, PEER_N_KEYS, half).astype(bf16)
    tb = TOPK_TILE
    return pl.pallas_call(
        _topk_kernel,
        grid=(t // tb,),
        in_specs=[pl.BlockSpec((nq, tb, half), lambda i: (0, i, 0)),
                  pl.BlockSpec((nq, PEER_N_KEYS, half), lambda i: (0, 0, 0))],
        out_specs=[pl.BlockSpec((tb, N_SLOTS), lambda i: (i, 0)),
                   pl.BlockSpec((tb, N_SLOTS), lambda i: (i, 0))],
        out_shape=[jax.ShapeDtypeStruct((t, N_SLOTS), i32), jax.ShapeDtypeStruct((t, N_SLOTS), f32)],
        scratch_shapes=[pltpu.VMEM((N_SLOTS, tb), i32), pltpu.VMEM((N_SLOTS, tb), f32)],
        compiler_params=_tc_params("parallel"),
        name="peer_topk",
    )(q3, keys)


def _gate_kernel(a_ref, g_ref, w_ref):
    w_ref[...] = _gelu(a_ref[...]) * g_ref[...]


def _peer_gate(a, gates):
    t, n = a.shape
    tm = 2048
    spec = pl.BlockSpec((tm, n), lambda i: (i, 0))
    return pl.pallas_call(
        _gate_kernel, grid=(t // tm,), in_specs=[spec, spec], out_specs=spec,
        out_shape=jax.ShapeDtypeStruct((t, n), f32),
        compiler_params=_tc_params("parallel"), name="peer_gate",
    )(a, gates)


def _sc_params():
    return pltpu.CompilerParams(needs_layout_passes=False)


def _peer_u_stage(x, ids, u_tab):
    t_total, d = x.shape
    tpw = t_total // SC_WORKERS
    mesh = plsc.VectorSubcoreMesh(core_axis_name="c", subcore_axis_name="s")

    @functools.partial(
        pl.kernel, mesh=mesh,
        out_type=jax.ShapeDtypeStruct((t_total, N_SLOTS), f32),
        scratch_types=[
            pltpu.VMEM((N_SLOTS,), i32),
            pltpu.VMEM((d,), f32),
            pltpu.VMEM((2, ROW_BATCH, d), f32),
            pltpu.VMEM((N_SLOTS,), f32),
            pltpu.VMEM((SC_LANES, SC_LANES), f32),
            pltpu.SemaphoreType.DMA((2,)),
        ],
        compiler_params=_sc_params())
    def k(x_hbm, ids_hbm, u_hbm, a_hbm, idx_v, x_v, rows_v, a_v, tmp_v, sems):
        wid = lax.axis_index("s") * SC_CORES + lax.axis_index("c")
        base = wid * tpw
        iota = lax.iota(i32, SC_LANES)

        def gather(b, slot):
            return pltpu.make_async_copy(
                u_hbm.at[idx_v.at[pl.ds(b * ROW_BATCH, ROW_BATCH)]], rows_v.at[slot], sems.at[slot])

        def token_body(i, carry):
            t = base + i
            pltpu.sync_copy(ids_hbm.at[t], idx_v)
            pltpu.sync_copy(x_hbm.at[t], x_v)
            gather(0, 0).start()
            for b in range(N_BATCH):
                slot = b % 2
                if b + 1 < N_BATCH:
                    gather(b + 1, 1 - slot).start()
                gather(b, slot).wait()
                for g in range(ROW_BATCH // SC_LANES):
                    def cbody(c, accs):
                        off = pl.multiple_of(c * SC_LANES, SC_LANES)
                        xc = x_v[pl.ds(off, SC_LANES)]
                        return tuple(accs[j] + xc * rows_v[slot, g * SC_LANES + j, pl.ds(off, SC_LANES)]
                                     for j in range(SC_LANES))
                    accs = lax.fori_loop(0, d // SC_LANES, cbody,
                                         tuple(jnp.zeros((SC_LANES,), f32) for _ in range(SC_LANES)))
                    for j in range(SC_LANES):
                        tmp_v[j, :] = accs[j]
                    r = jnp.zeros((SC_LANES,), f32)
                    for l in range(SC_LANES):
                        r = r + plsc.load_gather(tmp_v, [iota, jnp.full((SC_LANES,), l, i32)])
                    a_v[pl.ds(b * ROW_BATCH + g * SC_LANES, SC_LANES)] = r
            pltpu.sync_copy(a_v, a_hbm.at[t])
            return carry

        lax.fori_loop(0, tpw, token_body, 0)

    return k(x, ids, u_tab)


def _peer_v_stage(w, ids, v_tab):
    t_total = w.shape[0]
    d = v_tab.shape[1]
    tpw = t_total // SC_WORKERS
    grp = 8
    mesh = plsc.VectorSubcoreMesh(core_axis_name="c", subcore_axis_name="s")

    @functools.partial(
        pl.kernel, mesh=mesh,
        out_type=jax.ShapeDtypeStruct((t_total, d), f32),
        scratch_types=[
            pltpu.VMEM((N_SLOTS,), i32),
            pltpu.VMEM((N_SLOTS,), f32),
            pltpu.VMEM((2, ROW_BATCH, d), f32),
            pltpu.VMEM((d,), f32),
            pltpu.SemaphoreType.DMA((2,)),
        ],
        compiler_params=_sc_params())
    def k(w_hbm, ids_hbm, v_hbm, o_hbm, idx_v, w_v, rows_v, o_v, sems):
        wid = lax.axis_index("s") * SC_CORES + lax.axis_index("c")
        base = wid * tpw

        def gather(b, slot):
            return pltpu.make_async_copy(
                v_hbm.at[idx_v.at[pl.ds(b * ROW_BATCH, ROW_BATCH)]], rows_v.at[slot], sems.at[slot])

        def token_body(i, carry):
            t = base + i
            pltpu.sync_copy(ids_hbm.at[t], idx_v)
            pltpu.sync_copy(w_hbm.at[t], w_v)
            gather(0, 0).start()
            for b in range(N_BATCH):
                slot = b % 2
                if b + 1 < N_BATCH:
                    gather(b + 1, 1 - slot).start()
                gather(b, slot).wait()
                for g in range(ROW_BATCH // grp):
                    sp = [plsc.load_gather(w_v, [jnp.full((SC_LANES,), b * ROW_BATCH + g * grp + j, i32)])
                          for j in range(grp)]
                    first = (b == 0 and g == 0)

                    def cbody(c, carry2):
                        off = pl.multiple_of(c * SC_LANES, SC_LANES)
                        if first:
                            acc = sp[0] * rows_v[slot, g * grp, pl.ds(off, SC_LANES)]
                            js = range(1, grp)
                        else:
                            acc = o_v[pl.ds(off, SC_LANES)]
                            js = range(grp)
                        for j in js:
                            acc = acc + sp[j] * rows_v[slot, g * grp + j, pl.ds(off, SC_LANES)]
                        o_v[pl.ds(off, SC_LANES)] = acc
                        return carry2
                    lax.fori_loop(0, d // SC_LANES, cbody, 0)
            pltpu.sync_copy(o_v, o_hbm.at[t])
            return carry

        lax.fori_loop(0, tpw, token_body, 0)

    return k(w, ids, v_tab)


def _peer(x2d, g, w_q, sub_keys, u_tab, v_tab):
    h, q3 = _peer_q(x2d, g, w_q)
    ids, gates = _peer_topk(q3, sub_keys)
    a = _peer_u_stage(h, ids, u_tab)
    w = _peer_gate(a, gates)
    return _peer_v_stage(w, ids, v_tab)


def kernel(x, norm_mix, norm_ffn, norm_final, mix_w_in, s5_lambda_re, s5_lambda_im, s5_log_step, s5_b_re, s5_b_im, s5_c_re, s5_c_im, s5_d, s5_w_glu, conv_dw_w, conv_dw_b, conv_ln_g, conv_ln_b, mix_w_out, attn_w_qkv, attn_b_qkv, attn_sinks, attn_w_o, peer_w_q, peer_sub_keys, peer_u, peer_v):
    bsz, seq, d = x.shape
    t = bsz * seq
    z = _mix_in(x.reshape(t, d), norm_mix[0], mix_w_in[0])
    tabs = _s5_tables(s5_lambda_re[0], s5_lambda_im[0], s5_log_step[0], s5_b_re[0], s5_b_im[0],
                      s5_c_re[0], s5_c_im[0])
    x1 = _mixer(z.reshape(bsz, seq, -1), x, tabs, s5_d[0], s5_w_glu[0], conv_dw_w[0], conv_dw_b[0],
                conv_ln_g[0], conv_ln_b[0], mix_w_out[0]).reshape(t, d)
    p0 = _peer(x1, norm_ffn[0], peer_w_q[0], peer_sub_keys[0], peer_u[0], peer_v[0])
    x2, qkv = _qkv(x1, p0, norm_mix[1], attn_w_qkv[0], attn_b_qkv[0], seq)
    x3 = _attention(qkv.reshape(bsz, seq, -1), x2.reshape(bsz, seq, d), attn_sinks[0], attn_w_o[0]).reshape(t, d)
    p1 = _peer(x3, norm_ffn[1], peer_w_q[1], peer_sub_keys[1], peer_u[1], peer_v[1])
    return _final_norm(x3, p1, norm_final).reshape(bsz, seq, d)
```

```python
import functools

import jax
import jax.numpy as jnp
from jax import lax
from jax.experimental import pallas as pl
from jax.experimental.pallas import tpu as pltpu
from jax.experimental.pallas import tpu_sc as plsc

f32 = jnp.float32
bf16 = jnp.bfloat16
i32 = jnp.int32

D_MODEL = 1024
S5_WIDTH = 512
S5_GROUP = 16
S5_GROUPS = 32
S5_STATE = 64
S5_LANES = S5_GROUPS * S5_STATE
CONV_WIDTH = 512
CONV_K = 31
HEAD_DIM = 64
N_Q_HEADS = 16
N_KV_HEADS = 2
Q_PER_KV = 8
Q_WIDTH = N_Q_HEADS * HEAD_DIM
KV_WIDTH = 2 * N_KV_HEADS * HEAD_DIM
QKV_WIDTH = Q_WIDTH + KV_WIDTH
WINDOW = 128
ROPE_THETA = 500000.0
ROPE_DIM = 16
PEER_HEADS = 8
PEER_N_KEYS = 128
PEER_HALF = 128
PEER_TOPK = 16
N_SLOTS = PEER_HEADS * PEER_TOPK
NORM_EPS = 1e-6

LANES = 128
SUBLANES = 8
TC_VMEM_LIMIT = 48 * 1024 * 1024

SC_CORES = 2
SC_SUBCORES = 16
SC_LANES = 16
SC_WORKERS = SC_CORES * SC_SUBCORES
ROW_BATCH = 32
N_BATCH = N_SLOTS // ROW_BATCH

ROW_TILE = 512
MIX_TILE = 256
SCAN_TILE = 128
TOPK_TILE = 128
CONV_HALO = 32


def _tc_params(*sem):
    return pltpu.CompilerParams(dimension_semantics=sem, vmem_limit_bytes=TC_VMEM_LIMIT)


def _rms(x, g):
    return x * lax.rsqrt(jnp.mean(x * x, axis=-1, keepdims=True) + NORM_EPS) * g


def _sigmoid(x):
    return 1.0 / (1.0 + jnp.exp(-x))


def _gelu(x):
    return 0.5 * x * (1.0 + lax.erf(x * 0.7071067811865476))


def _dot(a, b):
    return jnp.dot(a, b, preferred_element_type=f32)


def _mix_in_kernel(x_ref, g_ref, w_ref, z_ref):
    h = _rms(x_ref[...], g_ref[...])
    z_ref[...] = _dot(h.astype(bf16), w_ref[...])


def _mix_in(x2d, g, w):
    t, d = x2d.shape
    n = w.shape[1]
    return pl.pallas_call(
        _mix_in_kernel,
        grid=(t // ROW_TILE,),
        in_specs=[pl.BlockSpec((ROW_TILE, d), lambda i: (i, 0)),
                  pl.BlockSpec((1, d), lambda i: (0, 0)),
                  pl.BlockSpec((d, n), lambda i: (0, 0))],
        out_specs=pl.BlockSpec((ROW_TILE, n), lambda i: (i, 0)),
        out_shape=jax.ShapeDtypeStruct((t, n), f32),
        compiler_params=_tc_params("parallel"),
        name="mix_in",
    )(x2d, g.reshape(1, d), w.astype(bf16))


def _peer_q_kernel(x_ref, g_ref, w_ref, h_ref, q_ref):
    h = _rms(x_ref[...], g_ref[...])
    h_ref[...] = h
    q = _dot(h.astype(bf16), w_ref[...])
    for c in range(2 * PEER_HEADS):
        q_ref[c] = q[:, c * PEER_HALF:(c + 1) * PEER_HALF]


def _peer_q(x2d, g, w_q):
    t, d = x2d.shape
    nq = 2 * PEER_HEADS
    return pl.pallas_call(
        _peer_q_kernel,
        grid=(t // ROW_TILE,),
        in_specs=[pl.BlockSpec((ROW_TILE, d), lambda i: (i, 0)),
                  pl.BlockSpec((1, d), lambda i: (0, 0)),
                  pl.BlockSpec((d, nq * PEER_HALF), lambda i: (0, 0))],
        out_specs=[pl.BlockSpec((ROW_TILE, d), lambda i: (i, 0)),
                   pl.BlockSpec((nq, ROW_TILE, PEER_HALF), lambda i: (0, i, 0))],
        out_shape=[jax.ShapeDtypeStruct((t, d), f32),
                   jax.ShapeDtypeStruct((nq, t, PEER_HALF), f32)],
        compiler_params=_tc_params("parallel"),
        name="peer_q",
    )(x2d, g.reshape(1, d), w_q.astype(bf16))


def _qkv_kernel(x_ref, r_ref, g_ref, w_ref, b_ref, cos_ref, sin_ref, xo_ref, qkv_ref):
    x = x_ref[...] + r_ref[...]
    xo_ref[...] = x
    h = _rms(x, g_ref[...])
    qkv = _dot(h.astype(bf16), w_ref[...]) + b_ref[...]
    cos = cos_ref[...]
    sin = sin_ref[...]
    lane = lax.broadcasted_iota(i32, cos.shape, 1)
    low = (lane % HEAD_DIM) < (ROPE_DIM // 2)
    n_rot = (Q_WIDTH + N_KV_HEADS * HEAD_DIM) // LANES
    for c in range(QKV_WIDTH // LANES):
        t = qkv[:, c * LANES:(c + 1) * LANES]
        if c < n_rot:
            partner = jnp.where(low, pltpu.roll(t, LANES - ROPE_DIM // 2, axis=1),
                                pltpu.roll(t, ROPE_DIM // 2, axis=1))
            t = t * cos + partner * sin
        if c < Q_WIDTH // LANES:
            t = t * (HEAD_DIM ** -0.5)
        qkv_ref[:, c * LANES:(c + 1) * LANES] = t


def _rope_tables(seq):
    inv = jnp.power(ROPE_THETA, -jnp.arange(0, ROPE_DIM, 2, dtype=f32) / ROPE_DIM)
    ang = jnp.arange(seq, dtype=f32)[:, None] * inv[None, :]
    cos, sin = jnp.cos(ang), jnp.sin(ang)
    pad = HEAD_DIM - ROPE_DIM
    cos_h = jnp.concatenate([cos, cos, jnp.ones((seq, pad), f32)], axis=1)
    sin_h = jnp.concatenate([-sin, sin, jnp.zeros((seq, pad), f32)], axis=1)
    return jnp.tile(cos_h, (1, LANES // HEAD_DIM)), jnp.tile(sin_h, (1, LANES // HEAD_DIM))


def _qkv(x2d, r2d, g, w, b, seq):
    t, d = x2d.shape
    n = w.shape[1]
    cos, sin = _rope_tables(seq)
    blocks_per_seq = seq // ROW_TILE
    return pl.pallas_call(
        _qkv_kernel,
        grid=(t // ROW_TILE,),
        in_specs=[pl.BlockSpec((ROW_TILE, d), lambda i: (i, 0)),
                  pl.BlockSpec((ROW_TILE, d), lambda i: (i, 0)),
                  pl.BlockSpec((1, d), lambda i: (0, 0)),
                  pl.BlockSpec((d, n), lambda i: (0, 0)),
                  pl.BlockSpec((1, n), lambda i: (0, 0)),
                  pl.BlockSpec((ROW_TILE, LANES), lambda i: (i % blocks_per_seq, 0)),
                  pl.BlockSpec((ROW_TILE, LANES), lambda i: (i % blocks_per_seq, 0))],
        out_specs=[pl.BlockSpec((ROW_TILE, d), lambda i: (i, 0)),
                   pl.BlockSpec((ROW_TILE, n), lambda i: (i, 0))],
        out_shape=[jax.ShapeDtypeStruct((t, d), f32), jax.ShapeDtypeStruct((t, n), f32)],
        compiler_params=_tc_params("parallel"),
        name="qkv",
    )(x2d, r2d, g.reshape(1, d), w.astype(bf16), b.reshape(1, n), cos, sin)


def _final_norm_kernel(x_ref, r_ref, g_ref, o_ref):
    o_ref[...] = _rms(x_ref[...] + r_ref[...], g_ref[...])


def _final_norm(x2d, r2d, g):
    t, d = x2d.shape
    spec = pl.BlockSpec((ROW_TILE, d), lambda i: (i, 0))
    return pl.pallas_call(
        _final_norm_kernel,
        grid=(t // ROW_TILE,),
        in_specs=[spec, spec, pl.BlockSpec((1, d), lambda i: (0, 0))],
        out_specs=spec,
        out_shape=jax.ShapeDtypeStruct((t, d), f32),
        compiler_params=_tc_params("parallel"),
        name="final_norm",
    )(x2d, r2d, g.reshape(1, d))


def _shift_rows(v, d, row):
    if d % SUBLANES == 0:
        return jnp.concatenate([jnp.zeros((d, v.shape[1]), v.dtype), v[:v.shape[0] - d]], axis=0)
    return jnp.where(row >= d, pltpu.roll(v, d, axis=0), 0.0)


def _mixer_kernel(z_ref, x_ref, apr_ref, api_ref, bre_ref, bim_ref, cre_ref, cim_ref, dvec_ref, wglu_ref,
                  dww_ref, dwb_ref, lng_ref, lnb_ref, wout_ref, o_ref,
                  carry_re, carry_im, y_s, vbuf):
    tb = z_ref.shape[0]
    n_chunks = S5_LANES // LANES
    steps = [1 << s for s in range(SCAN_TILE.bit_length() - 1)]

    @pl.when(pl.program_id(1) == 0)
    def _():
        carry_re[...] = jnp.zeros_like(carry_re)
        carry_im[...] = jnp.zeros_like(carry_im)
        vbuf[0:CONV_HALO, :] = jnp.zeros((CONV_HALO, CONV_WIDTH), f32)

    u = z_ref[:, 0:S5_WIDTH]
    ub = u.astype(bf16)
    y_s[...] = u * dvec_ref[...]
    row = lax.broadcasted_iota(i32, (SCAN_TILE, LANES), 0)

    def chunk_body(j, carry):
        bu_re = _dot(ub, bre_ref[j])
        bu_im = _dot(ub, bim_ref[j])
        apr = apr_ref[j]
        api = api_ref[j]
        c_re = carry_re[j]
        c_im = carry_im[j]
        h_re_tiles, h_im_tiles = [], []
        for r in range(tb // SCAN_TILE):
            hr = bu_re[r * SCAN_TILE:(r + 1) * SCAN_TILE]
            hi = bu_im[r * SCAN_TILE:(r + 1) * SCAN_TILE]
            a_re, a_im = apr[0:1], api[0:1]
            first = row == 0
            hr = hr + jnp.where(first, a_re * c_re - a_im * c_im, 0.0)
            hi = hi + jnp.where(first, a_re * c_im + a_im * c_re, 0.0)
            for s, d in enumerate(steps):
                a_re, a_im = apr[s:s + 1], api[s:s + 1]
                sr = _shift_rows(hr, d, row)
                si = _shift_rows(hi, d, row)
                hr, hi = hr + (a_re * sr - a_im * si), hi + (a_re * si + a_im * sr)
            c_re = hr[SCAN_TILE - 1:SCAN_TILE]
            c_im = hi[SCAN_TILE - 1:SCAN_TILE]
            h_re_tiles.append(hr.astype(bf16))
            h_im_tiles.append(hi.astype(bf16))
        carry_re[j] = c_re
        carry_im[j] = c_im
        h_re = jnp.concatenate(h_re_tiles, axis=0)
        h_im = jnp.concatenate(h_im_tiles, axis=0)
        y_s[...] += _dot(h_re, cre_ref[j]) - _dot(h_im, cim_ref[j])
        return carry

    lax.fori_loop(0, n_chunks, chunk_body, 0)
    y = _gelu(y_s[...])
    y_ssm = y * _sigmoid(_dot(y.astype(bf16), wglu_ref[...]))

    v = z_ref[:, S5_WIDTH:S5_WIDTH + CONV_WIDTH] * _sigmoid(z_ref[:, S5_WIDTH + CONV_WIDTH:S5_WIDTH + 2 * CONV_WIDTH])
    vbuf[CONV_HALO:CONV_HALO + tb, :] = v
    acc = jnp.broadcast_to(dwb_ref[...], (tb, CONV_WIDTH))
    for k in range(CONV_K):
        acc = acc + dww_ref[k:k + 1, :] * vbuf[pl.ds(CONV_HALO - (CONV_K - 1) + k, tb), :]
    vbuf[0:CONV_HALO, :] = vbuf[tb:tb + CONV_HALO, :]
    mu = jnp.mean(acc, axis=-1, keepdims=True)
    cen = acc - mu
    var = jnp.mean(cen * cen, axis=-1, keepdims=True)
    yn = cen * lax.rsqrt(var + NORM_EPS) * lng_ref[...] + lnb_ref[...]
    y_conv = yn * _sigmoid(yn)

    o_ref[...] = (x_ref[...] + _dot(y_ssm.astype(bf16), wout_ref[0:S5_WIDTH, :])
                  + _dot(y_conv.astype(bf16), wout_ref[S5_WIDTH:S5_WIDTH + CONV_WIDTH, :]))


def _s5_tables(lam_re, lam_im, log_step, b_re, b_im, c_re, c_im):
    n_chunks = S5_LANES // LANES
    g_per_chunk = LANES // S5_STATE
    lam = lax.complex(lam_re, lam_im)
    dt = jnp.exp(log_step)[:, None]
    a_bar = jnp.exp(lam * dt)
    b_bar = ((a_bar - 1.0) / lam)[..., None] * lax.complex(b_re, b_im)
    n_steps = SCAN_TILE.bit_length() - 1
    powers = jnp.stack([jnp.exp(lam * dt * float(1 << s)) for s in range(n_steps)]
                       + [jnp.ones_like(a_bar)] * (SUBLANES - n_steps))
    powers = powers.reshape(SUBLANES, n_chunks, LANES).transpose(1, 0, 2)
    eye = jnp.eye(S5_GROUPS, dtype=f32)

    def in_mat(b):
        m = jnp.einsum('gph,gk->ghkp', b, eye).reshape(S5_WIDTH, S5_LANES)
        return m.reshape(S5_WIDTH, n_chunks, LANES).transpose(1, 0, 2).astype(bf16)

    def out_mat(c):
        m = jnp.einsum('ghp,gk->gpkh', c, eye).reshape(S5_LANES, S5_WIDTH)
        return m.reshape(n_chunks, LANES, S5_WIDTH).astype(bf16)

    del g_per_chunk
    return (jnp.real(powers), jnp.imag(powers), in_mat(jnp.real(b_bar)), in_mat(jnp.imag(b_bar)),
            out_mat(c_re), out_mat(c_im))


def _mixer(z, x, s5_tabs, d_vec, w_glu, dw_w, dw_b, ln_g, ln_b, w_out):
    bsz, seq, zw = z.shape
    d = x.shape[-1]
    apr, api, bre, bim, cre, cim = s5_tabs
    n_chunks = S5_LANES // LANES
    tb = MIX_TILE
    const2 = lambda b, n: (0, 0)
    const3 = lambda b, n: (0, 0, 0)
    return pl.pallas_call(
        _mixer_kernel,
        grid=(bsz, seq // tb),
        in_specs=[pl.BlockSpec((None, tb, zw), lambda b, n: (b, n, 0)),
                  pl.BlockSpec((None, tb, d), lambda b, n: (b, n, 0)),
                  pl.BlockSpec((n_chunks, SUBLANES, LANES), const3),
                  pl.BlockSpec((n_chunks, SUBLANES, LANES), const3),
                  pl.BlockSpec((n_chunks, S5_WIDTH, LANES), const3),
                  pl.BlockSpec((n_chunks, S5_WIDTH, LANES), const3),
                  pl.BlockSpec((n_chunks, LANES, S5_WIDTH), const3),
                  pl.BlockSpec((n_chunks, LANES, S5_WIDTH), const3),
                  pl.BlockSpec((1, S5_WIDTH), const2),
                  pl.BlockSpec((S5_WIDTH, S5_WIDTH), const2),
                  pl.BlockSpec((CONV_K, CONV_WIDTH), const2),
                  pl.BlockSpec((1, CONV_WIDTH), const2),
                  pl.BlockSpec((1, CONV_WIDTH), const2),
                  pl.BlockSpec((1, CONV_WIDTH), const2),
                  pl.BlockSpec((S5_WIDTH + CONV_WIDTH, d), const2)],
        out_specs=pl.BlockSpec((None, tb, d), lambda b, n: (b, n, 0)),
        out_shape=jax.ShapeDtypeStruct((bsz, seq, d), f32),
        scratch_shapes=[pltpu.VMEM((n_chunks, 1, LANES), f32),
                        pltpu.VMEM((n_chunks, 1, LANES), f32),
                        pltpu.VMEM((tb, S5_WIDTH), f32),
                        pltpu.VMEM((tb + CONV_HALO, CONV_WIDTH), f32)],
        compiler_params=_tc_params("parallel", "arbitrary"),
        name="mixer",
    )(z, x, apr, api, bre, bim, cre, cim, d_vec.reshape(1, S5_WIDTH), w_glu.astype(bf16),
      dw_w, dw_b.reshape(1, -1), ln_g.reshape(1, -1), ln_b.reshape(1, -1), w_out.astype(bf16))


def _attn_kernel(q_ref, kvc_ref, kvp_ref, x_ref, sink_ref, wo_ref, o_ref, o_s):
    n = pl.program_id(1)
    qi = lax.broadcasted_iota(i32, (WINDOW, 2 * WINDOW), 0)
    si = lax.broadcasted_iota(i32, (WINDOW, 2 * WINDOW), 1)
    first_key = jnp.where(n > 0, 0, WINDOW)
    valid = (si > qi) & (si <= qi + WINDOW) & (si >= first_key)
    kv = jnp.concatenate([kvp_ref[...], kvc_ref[...]], axis=0).astype(bf16)
    for kh in range(N_KV_HEADS):
        k = kv[:, kh * HEAD_DIM:(kh + 1) * HEAD_DIM]
        v = kv[:, (N_KV_HEADS + kh) * HEAD_DIM:(N_KV_HEADS + kh + 1) * HEAD_DIM]
        for g in range(Q_PER_KV):
            h = kh * Q_PER_KV + g
            q = q_ref[:, h * HEAD_DIM:(h + 1) * HEAD_DIM].astype(bf16)
            s = lax.dot_general(q, k, (((1,), (1,)), ((), ())), preferred_element_type=f32)
            s = jnp.where(valid, s, -jnp.inf)
            sink = sink_ref[0:1, h:h + 1]
            m = jnp.maximum(jnp.max(s, axis=-1, keepdims=True), sink)
            p = jnp.exp(s - m)
            denom = jnp.sum(p, axis=-1, keepdims=True) + jnp.exp(sink - m)
            probs = (p / denom).astype(bf16)
            o_s[:, h * HEAD_DIM:(h + 1) * HEAD_DIM] = _dot(probs, v)
    o_ref[...] = x_ref[...] + _dot(o_s[...].astype(bf16), wo_ref[...])


def _attention(qkv, x, sinks, w_o):
    bsz, seq, _ = qkv.shape
    d = x.shape[-1]
    kv_block = Q_WIDTH // KV_WIDTH
    return pl.pallas_call(
        _attn_kernel,
        grid=(bsz, seq // WINDOW),
        in_specs=[pl.BlockSpec((None, WINDOW, Q_WIDTH), lambda b, n: (b, n, 0)),
                  pl.BlockSpec((None, WINDOW, KV_WIDTH), lambda b, n: (b, n, kv_block)),
                  pl.BlockSpec((None, WINDOW, KV_WIDTH), lambda b, n: (b, jnp.maximum(n - 1, 0), kv_block)),
                  pl.BlockSpec((None, WINDOW, d), lambda b, n: (b, n, 0)),
                  pl.BlockSpec((1, N_Q_HEADS), lambda b, n: (0, 0)),
                  pl.BlockSpec((Q_WIDTH, d), lambda b, n: (0, 0))],
        out_specs=pl.BlockSpec((None, WINDOW, d), lambda b, n: (b, n, 0)),
        out_shape=jax.ShapeDtypeStruct((bsz, seq, d), f32),
        scratch_shapes=[pltpu.VMEM((WINDOW, Q_WIDTH), f32)],
        compiler_params=_tc_params("parallel", "parallel"),
        name="attention",
    )(qkv, qkv, qkv, x, sinks.reshape(1, N_Q_HEADS), w_o.astype(bf16))


def _top16(s, ids=None):
    m_rows = s.shape[0]
    pos = lax.broadcasted_iota(i32, s.shape, 0).astype(f32)
    vals, outs = [], []
    for _ in range(PEER_TOPK):
        mx = jnp.max(s, axis=0, keepdims=True)
        first = jnp.min(jnp.where(s == mx, pos, float(m_rows)), axis=0, keepdims=True)
        hit = pos == first
        vals.append(mx)
        outs.append(first if ids is None else jnp.sum(jnp.where(hit, ids, 0.0), axis=0, keepdims=True))
        s = jnp.where(hit, -jnp.inf, s)
    return jnp.concatenate(vals, axis=0), jnp.concatenate(outs, axis=0)


def _pair_candidates(a0, a1, combine, fill):
    n = a0.shape[1]
    sub = lax.broadcasted_iota(i32, (SUBLANES, n), 0)
    pieces = [combine(a0[0:1], a1)]
    for i in range(1, SUBLANES):
        keep = PEER_TOPK // (i + 1)
        pieces.append(jnp.where(sub < keep, combine(a0[i:i + 1], a1[0:SUBLANES]), fill))
    pieces.append(combine(a0[SUBLANES:PEER_TOPK], a1[0:1]))
    return jnp.concatenate(pieces, axis=0)


def _topk_kernel(q_ref, keys_ref, ids_ref, gates_ref, ids_s, gates_s):
    def head_body(h, carry):
        tops = []
        for c in range(2):
            q = q_ref[2 * h + c].astype(bf16)
            s = lax.dot_general(keys_ref[2 * h + c], q, (((1,), (1,)), ((), ())),
                                preferred_element_type=f32)
            tops.append(_top16(s))
        (s0, i0), (s1, i1) = tops
        cand_s = _pair_candidates(s0, s1, lambda a, b: a + b, -jnp.inf)
        cand_i = _pair_candidates(i0, i1, lambda a, b: a * float(PEER_N_KEYS) + b, 0.0)
        best_s, best_i = _top16(cand_s, cand_i)
        e = jnp.exp(best_s - best_s[0:1])
        off = pl.multiple_of(h * PEER_TOPK, PEER_TOPK)
        gates_s[pl.ds(off, PEER_TOPK), :] = e / jnp.sum(e, axis=0, keepdims=True)
        ids_s[pl.ds(off, PEER_TOPK), :] = best_i.astype(i32)
        return carry

    lax.fori_loop(0, PEER_HEADS, head_body, 0)
    ids_ref[...] = ids_s[...].T
    gates_ref[...] = gates_s[...].T


def _peer_topk(q3, sub_keys):
    nq, t, half = q3.shape
    keys = sub_keys.reshape(nq, PEER_N_KEYS, half).astype(bf16)
    tb = TOPK_TILE
    return pl.pallas_call(
        _topk_kernel,
        grid=(t // tb,),
        in_specs=[pl.BlockSpec((nq, tb, half), lambda i: (0, i, 0)),
                  pl.BlockSpec((nq, PEER_N_KEYS, half), lambda i: (0, 0, 0))],
        out_specs=[pl.BlockSpec((tb, N_SLOTS), lambda i: (i, 0)),
                   pl.BlockSpec((tb, N_SLOTS), lambda i: (i, 0))],
        out_shape=[jax.ShapeDtypeStruct((t, N_SLOTS), i32), jax.ShapeDtypeStruct((t, N_SLOTS), f32)],
        scratch_shapes=[pltpu.VMEM((N_SLOTS, tb), i32), pltpu.VMEM((N_SLOTS, tb), f32)],
        compiler_params=_tc_params("parallel"),
        name="peer_topk",
    )(q3, keys)


def _gate_kernel(a_ref, g_ref, w_ref):
    w_ref[...] = _gelu(a_ref[...]) * g_ref[...]


def _peer_gate(a, gates):
    t, n = a.shape
    tm = 2048
    spec = pl.BlockSpec((tm, n), lambda i: (i, 0))
    return pl.pallas_call(
        _gate_kernel, grid=(t // tm,), in_specs=[spec, spec], out_specs=spec,
        out_shape=jax.ShapeDtypeStruct((t, n), f32),
        compiler_params=_tc_params("parallel"), name="peer_gate",
    )(a, gates)


def _sc_params():
    return pltpu.CompilerParams(needs_layout_passes=False)


def _peer_sc_stage(side, ids, tab, mode):
    t_total, side_w = side.shape
    d = tab.shape[1]
    out_w = N_SLOTS if mode == "dot" else d
    tpw = t_total // SC_WORKERS
    n_pairs = tpw // 2
    assert tpw * SC_WORKERS == t_total and n_pairs * 2 == tpw and N_BATCH % 2 == 0
    n_grp = ROW_BATCH // SC_LANES
    mesh = plsc.VectorSubcoreMesh(core_axis_name="c", subcore_axis_name="s")

    @functools.partial(
        pl.kernel, mesh=mesh,
        out_type=jax.ShapeDtypeStruct((t_total, out_w), f32),
        scratch_types=[
            pltpu.VMEM((N_SLOTS,), i32), pltpu.VMEM((N_SLOTS,), i32),
            pltpu.VMEM((side_w,), f32), pltpu.VMEM((side_w,), f32),
            pltpu.VMEM((out_w,), f32), pltpu.VMEM((out_w,), f32),
            pltpu.VMEM((2, ROW_BATCH, d), f32),
            pltpu.VMEM((SC_LANES, SC_LANES), f32),
            pltpu.SemaphoreType.DMA((2,)), pltpu.SemaphoreType.DMA((2,)),
            pltpu.SemaphoreType.DMA((2,)), pltpu.SemaphoreType.DMA((2,)),
        ],
        compiler_params=_sc_params())
    def k(side_hbm, ids_hbm, tab_hbm, out_hbm, idx0, idx1, side0, side1, out0, out1, rows_v, tmp_v,
          sem_g, sem_i, sem_s, sem_o):
        idx_v, side_v, out_v = (idx0, idx1), (side0, side1), (out0, out1)
        wid = lax.axis_index("s") * SC_CORES + lax.axis_index("c")
        base = wid * tpw
        iota = lax.iota(i32, SC_LANES)

        def gather(p, b):
            return pltpu.make_async_copy(
                tab_hbm.at[idx_v[p].at[pl.ds(b * ROW_BATCH, ROW_BATCH)]], rows_v.at[b % 2], sem_g.at[b % 2])

        def loads(t, p):
            return (pltpu.make_async_copy(ids_hbm.at[t], idx_v[p], sem_i.at[p]),
                    pltpu.make_async_copy(side_hbm.at[t], side_v[p], sem_s.at[p]))

        def store(t, p):
            return pltpu.make_async_copy(out_v[p], out_hbm.at[t], sem_o.at[p])

        def compute_dot(p, b):
            slot = b % 2
            for g in range(n_grp):
                def cbody(c, accs):
                    off = pl.multiple_of(c * SC_LANES, SC_LANES)
                    xc = side_v[p][pl.ds(off, SC_LANES)]
                    return tuple(accs[j] + xc * rows_v[slot, g * SC_LANES + j, pl.ds(off, SC_LANES)]
                                 for j in range(SC_LANES))
                accs = lax.fori_loop(0, d // SC_LANES, cbody,
                                     tuple(jnp.zeros((SC_LANES,), f32) for _ in range(SC_LANES)))
                for j in range(SC_LANES):
                    tmp_v[j, :] = accs[j]
                r = jnp.zeros((SC_LANES,), f32)
                for l in range(SC_LANES):
                    r = r + plsc.load_gather(tmp_v, [iota, jnp.full((SC_LANES,), l, i32)])
                out_v[p][pl.ds(b * ROW_BATCH + g * SC_LANES, SC_LANES)] = r

        def compute_wsum(p, b):
            slot = b % 2
            for g in range(n_grp):
                sp = [plsc.load_gather(side_v[p], [jnp.full((SC_LANES,), b * ROW_BATCH + g * SC_LANES + j, i32)])
                      for j in range(SC_LANES)]
                first = b == 0 and g == 0

                @plsc.parallel_loop(0, d, step=SC_LANES)
                def _(c):
                    off = pl.multiple_of(c, SC_LANES)
                    terms = [sp[j] * rows_v[slot, g * SC_LANES + j, pl.ds(off, SC_LANES)] for j in range(SC_LANES)]
                    while len(terms) > 1:
                        terms = [terms[i] + terms[i + 1] for i in range(0, len(terms), 2)]
                    if first:
                        out_v[p][pl.ds(off, SC_LANES)] = terms[0]
                    else:
                        plsc.addupdate(out_v[p].at[pl.ds(off, SC_LANES)], terms[0])

        compute = compute_dot if mode == "dot" else compute_wsum

        for c in loads(base, 0):
            c.start()
        for c in loads(base, 0):
            c.wait()
        gather(0, 0).start()
        for c in loads(base + 1, 1):
            c.start()

        def pair_body(i2, carry):
            not_last = i2 < n_pairs - 1
            for p in (0, 1):
                t = base + 2 * i2 + p

                @pl.when(i2 > 0)
                def _():
                    store(t - 2, p).wait()

                for b in range(N_BATCH):
                    if b + 1 < N_BATCH:
                        gather(p, b + 1).start()
                    else:
                        def start_next():
                            for c in loads(t + 1, 1 - p):
                                c.wait()
                            gather(1 - p, 0).start()
                        if p == 0:
                            start_next()
                        else:
                            pl.when(not_last)(start_next)
                    gather(p, b).wait()
                    compute(p, b)
                store(t, p).start()

                @pl.when(not_last)
                def _():
                    for c in loads(t + 2, p):
                        c.start()
            return carry

        lax.fori_loop(0, n_pairs, pair_body, 0)
        store(base + tpw - 2, 0).wait()
        store(base + tpw - 1, 1).wait()

    return k(side, ids, tab)


def _peer(x2d, g, w_q, sub_keys, u_tab, v_tab):
    h, q3 = _peer_q(x2d, g, w_q)
    ids, gates = _peer_topk(q3, sub_keys)
    a = _peer_sc_stage(h, ids, u_tab, "dot")
    w = _peer_gate(a, gates)
    return _peer_sc_stage(w, ids, v_tab, "wsum")


def kernel(x, norm_mix, norm_ffn, norm_final, mix_w_in, s5_lambda_re, s5_lambda_im, s5_log_step, s5_b_re, s5_b_im, s5_c_re, s5_c_im, s5_d, s5_w_glu, conv_dw_w, conv_dw_b, conv_ln_g, conv_ln_b, mix_w_out, attn_w_qkv, attn_b_qkv, attn_sinks, attn_w_o, peer_w_q, peer_sub_keys, peer_u, peer_v):
    bsz, seq, d = x.shape
    t = bsz * seq
    z = _mix_in(x.reshape(t, d), norm_mix[0], mix_w_in[0])
    tabs = _s5_tables(s5_lambda_re[0], s5_lambda_im[0], s5_log_step[0], s5_b_re[0], s5_b_im[0],
                      s5_c_re[0], s5_c_im[0])
    x1 = _mixer(z.reshape(bsz, seq, -1), x, tabs, s5_d[0], s5_w_glu[0], conv_dw_w[0], conv_dw_b[0],
                conv_ln_g[0], conv_ln_b[0], mix_w_out[0]).reshape(t, d)
    p0 = _peer(x1, norm_ffn[0], peer_w_q[0], peer_sub_keys[0], peer_u[0], peer_v[0])
    x2, qkv = _qkv(x1, p0, norm_mix[1], attn_w_qkv[0], attn_b_qkv[0], seq)
    x3 = _attention(qkv.reshape(bsz, seq, -1), x2.reshape(bsz, seq, d), attn_sinks[0], attn_w_o[0]).reshape(t, d)
    p1 = _peer(x3, norm_ffn[1], peer_w_q[1], peer_sub_keys[1], peer_u[1], peer_v[1])
    return _final_norm(x3, p1, norm_final).reshape(bsz, seq, d)
```

```python
import functools

import jax
import jax.numpy as jnp
from jax import lax
from jax.experimental import pallas as pl
from jax.experimental.pallas import tpu as pltpu
from jax.experimental.pallas import tpu_sc as plsc

f32 = jnp.float32
bf16 = jnp.bfloat16
i32 = jnp.int32
u32 = jnp.uint32

D_MODEL = 1024
S5_WIDTH = 512
S5_GROUP = 16
S5_GROUPS = 32
S5_STATE = 64
S5_LANES = S5_GROUPS * S5_STATE
CONV_WIDTH = 512
CONV_K = 31
HEAD_DIM = 64
N_Q_HEADS = 16
N_KV_HEADS = 2
Q_PER_KV = 8
Q_WIDTH = N_Q_HEADS * HEAD_DIM
KV_WIDTH = 2 * N_KV_HEADS * HEAD_DIM
QKV_WIDTH = Q_WIDTH + KV_WIDTH
WINDOW = 128
ROPE_THETA = 500000.0
ROPE_DIM = 16
PEER_HEADS = 8
PEER_N_KEYS = 128
PEER_HALF = 128
PEER_TOPK = 16
N_SLOTS = PEER_HEADS * PEER_TOPK
NORM_EPS = 1e-6

LANES = 128
SUBLANES = 8
TC_VMEM_LIMIT = 48 * 1024 * 1024

SC_CORES = 2
SC_SUBCORES = 16
SC_LANES = 16
SC_WORKERS = SC_CORES * SC_SUBCORES
ROW_BATCH = 32
N_BATCH = N_SLOTS // ROW_BATCH

ROW_TILE = 512
MIX_TILE = 256
SCAN_TILE = 128
TOPK_TILE = 128
CONV_HALO = 32


def _tc_params(*sem):
    return pltpu.CompilerParams(dimension_semantics=sem, vmem_limit_bytes=TC_VMEM_LIMIT)


def _rms(x, g):
    return x * lax.rsqrt(jnp.mean(x * x, axis=-1, keepdims=True) + NORM_EPS) * g


def _sigmoid(x):
    return 1.0 / (1.0 + jnp.exp(-x))


def _gelu(x):
    return 0.5 * x * (1.0 + lax.erf(x * 0.7071067811865476))


def _dot(a, b):
    return jnp.dot(a, b, preferred_element_type=f32)


def _mix_in_kernel(x_ref, g_ref, w_ref, z_ref):
    h = _rms(x_ref[...], g_ref[...])
    z_ref[...] = _dot(h.astype(bf16), w_ref[...])


def _mix_in(x2d, g, w):
    t, d = x2d.shape
    n = w.shape[1]
    return pl.pallas_call(
        _mix_in_kernel,
        grid=(t // ROW_TILE,),
        in_specs=[pl.BlockSpec((ROW_TILE, d), lambda i: (i, 0)),
                  pl.BlockSpec((1, d), lambda i: (0, 0)),
                  pl.BlockSpec((d, n), lambda i: (0, 0))],
        out_specs=pl.BlockSpec((ROW_TILE, n), lambda i: (i, 0)),
        out_shape=jax.ShapeDtypeStruct((t, n), f32),
        compiler_params=_tc_params("parallel"),
        name="mix_in",
    )(x2d, g.reshape(1, d), w.astype(bf16))


def _peer_q_kernel(x_ref, g_ref, w_ref, h_ref, q_ref):
    h = _rms(x_ref[...], g_ref[...])
    h_ref[...] = h
    q = _dot(h.astype(bf16), w_ref[...])
    for c in range(2 * PEER_HEADS):
        q_ref[c] = q[:, c * PEER_HALF:(c + 1) * PEER_HALF]


def _peer_q(x2d, g, w_q):
    t, d = x2d.shape
    nq = 2 * PEER_HEADS
    return pl.pallas_call(
        _peer_q_kernel,
        grid=(t // ROW_TILE,),
        in_specs=[pl.BlockSpec((ROW_TILE, d), lambda i: (i, 0)),
                  pl.BlockSpec((1, d), lambda i: (0, 0)),
                  pl.BlockSpec((d, nq * PEER_HALF), lambda i: (0, 0))],
        out_specs=[pl.BlockSpec((ROW_TILE, d), lambda i: (i, 0)),
                   pl.BlockSpec((nq, ROW_TILE, PEER_HALF), lambda i: (0, i, 0))],
        out_shape=[jax.ShapeDtypeStruct((t, d), f32),
                   jax.ShapeDtypeStruct((nq, t, PEER_HALF), f32)],
        compiler_params=_tc_params("parallel"),
        name="peer_q",
    )(x2d, g.reshape(1, d), w_q.astype(bf16))


def _qkv_kernel(x_ref, r_ref, g_ref, w_ref, b_ref, cos_ref, sin_ref, xo_ref, qkv_ref):
    x = x_ref[...] + r_ref[...]
    xo_ref[...] = x
    h = _rms(x, g_ref[...])
    qkv = _dot(h.astype(bf16), w_ref[...]) + b_ref[...]
    cos = cos_ref[...]
    sin = sin_ref[...]
    lane = lax.broadcasted_iota(i32, cos.shape, 1)
    low = (lane % HEAD_DIM) < (ROPE_DIM // 2)
    n_rot = (Q_WIDTH + N_KV_HEADS * HEAD_DIM) // LANES
    for c in range(QKV_WIDTH // LANES):
        t = qkv[:, c * LANES:(c + 1) * LANES]
        if c < n_rot:
            partner = jnp.where(low, pltpu.roll(t, LANES - ROPE_DIM // 2, axis=1),
                                pltpu.roll(t, ROPE_DIM // 2, axis=1))
            t = t * cos + partner * sin
        if c < Q_WIDTH // LANES:
            t = t * (HEAD_DIM ** -0.5)
        qkv_ref[:, c * LANES:(c + 1) * LANES] = t


def _rope_tables(seq):
    inv = jnp.power(ROPE_THETA, -jnp.arange(0, ROPE_DIM, 2, dtype=f32) / ROPE_DIM)
    ang = jnp.arange(seq, dtype=f32)[:, None] * inv[None, :]
    cos, sin = jnp.cos(ang), jnp.sin(ang)
    pad = HEAD_DIM - ROPE_DIM
    cos_h = jnp.concatenate([cos, cos, jnp.ones((seq, pad), f32)], axis=1)
    sin_h = jnp.concatenate([-sin, sin, jnp.zeros((seq, pad), f32)], axis=1)
    return jnp.tile(cos_h, (1, LANES // HEAD_DIM)), jnp.tile(sin_h, (1, LANES // HEAD_DIM))


def _qkv(x2d, r2d, g, w, b, seq):
    t, d = x2d.shape
    n = w.shape[1]
    cos, sin = _rope_tables(seq)
    blocks_per_seq = seq // ROW_TILE
    return pl.pallas_call(
        _qkv_kernel,
        grid=(t // ROW_TILE,),
        in_specs=[pl.BlockSpec((ROW_TILE, d), lambda i: (i, 0)),
                  pl.BlockSpec((ROW_TILE, d), lambda i: (i, 0)),
                  pl.BlockSpec((1, d), lambda i: (0, 0)),
                  pl.BlockSpec((d, n), lambda i: (0, 0)),
                  pl.BlockSpec((1, n), lambda i: (0, 0)),
                  pl.BlockSpec((ROW_TILE, LANES), lambda i: (i % blocks_per_seq, 0)),
                  pl.BlockSpec((ROW_TILE, LANES), lambda i: (i % blocks_per_seq, 0))],
        out_specs=[pl.BlockSpec((ROW_TILE, d), lambda i: (i, 0)),
                   pl.BlockSpec((ROW_TILE, n), lambda i: (i, 0))],
        out_shape=[jax.ShapeDtypeStruct((t, d), f32), jax.ShapeDtypeStruct((t, n), f32)],
        compiler_params=_tc_params("parallel"),
        name="qkv",
    )(x2d, r2d, g.reshape(1, d), w.astype(bf16), b.reshape(1, n), cos, sin)


def _final_norm_kernel(x_ref, r_ref, g_ref, o_ref):
    o_ref[...] = _rms(x_ref[...] + r_ref[...], g_ref[...])


def _final_norm(x2d, r2d, g):
    t, d = x2d.shape
    spec = pl.BlockSpec((ROW_TILE, d), lambda i: (i, 0))
    return pl.pallas_call(
        _final_norm_kernel,
        grid=(t // ROW_TILE,),
        in_specs=[spec, spec, pl.BlockSpec((1, d), lambda i: (0, 0))],
        out_specs=spec,
        out_shape=jax.ShapeDtypeStruct((t, d), f32),
        compiler_params=_tc_params("parallel"),
        name="final_norm",
    )(x2d, r2d, g.reshape(1, d))


def _shift_rows(v, d, row):
    if d % SUBLANES == 0:
        return jnp.concatenate([jnp.zeros((d, v.shape[1]), v.dtype), v[:v.shape[0] - d]], axis=0)
    return jnp.where(row >= d, pltpu.roll(v, d, axis=0), 0.0)


def _mixer_kernel(z_ref, x_ref, apr_ref, api_ref, bre_ref, bim_ref, cre_ref, cim_ref, dvec_ref, wglu_ref,
                  dww_ref, dwb_ref, lng_ref, lnb_ref, wout_ref, o_ref,
                  carry_re, carry_im, y_s, vbuf):
    tb = z_ref.shape[0]
    n_chunks = S5_LANES // LANES
    steps = [1 << s for s in range(SCAN_TILE.bit_length() - 1)]

    @pl.when(pl.program_id(1) == 0)
    def _():
        carry_re[...] = jnp.zeros_like(carry_re)
        carry_im[...] = jnp.zeros_like(carry_im)
        vbuf[0:CONV_HALO, :] = jnp.zeros((CONV_HALO, CONV_WIDTH), f32)

    u = z_ref[:, 0:S5_WIDTH]
    ub = u.astype(bf16)
    y_s[...] = u * dvec_ref[...]
    row = lax.broadcasted_iota(i32, (SCAN_TILE, LANES), 0)

    def chunk_body(j, carry):
        bu_re = _dot(ub, bre_ref[j])
        bu_im = _dot(ub, bim_ref[j])
        apr = apr_ref[j]
        api = api_ref[j]
        c_re = carry_re[j]
        c_im = carry_im[j]
        h_re_tiles, h_im_tiles = [], []
        for r in range(tb // SCAN_TILE):
            hr = bu_re[r * SCAN_TILE:(r + 1) * SCAN_TILE]
            hi = bu_im[r * SCAN_TILE:(r + 1) * SCAN_TILE]
            a_re, a_im = apr[0:1], api[0:1]
            first = row == 0
            hr = hr + jnp.where(first, a_re * c_re - a_im * c_im, 0.0)
            hi = hi + jnp.where(first, a_re * c_im + a_im * c_re, 0.0)
            for s, d in enumerate(steps):
                a_re, a_im = apr[s:s + 1], api[s:s + 1]
                sr = _shift_rows(hr, d, row)
                si = _shift_rows(hi, d, row)
                hr, hi = hr + (a_re * sr - a_im * si), hi + (a_re * si + a_im * sr)
            c_re = hr[SCAN_TILE - 1:SCAN_TILE]
            c_im = hi[SCAN_TILE - 1:SCAN_TILE]
            h_re_tiles.append(hr.astype(bf16))
            h_im_tiles.append(hi.astype(bf16))
        carry_re[j] = c_re
        carry_im[j] = c_im
        h_re = jnp.concatenate(h_re_tiles, axis=0)
        h_im = jnp.concatenate(h_im_tiles, axis=0)
        y_s[...] += _dot(h_re, cre_ref[j]) - _dot(h_im, cim_ref[j])
        return carry

    lax.fori_loop(0, n_chunks, chunk_body, 0)
    y = _gelu(y_s[...])
    y_ssm = y * _sigmoid(_dot(y.astype(bf16), wglu_ref[...]))

    v = z_ref[:, S5_WIDTH:S5_WIDTH + CONV_WIDTH] * _sigmoid(z_ref[:, S5_WIDTH + CONV_WIDTH:S5_WIDTH + 2 * CONV_WIDTH])
    vbuf[CONV_HALO:CONV_HALO + tb, :] = v
    acc = jnp.broadcast_to(dwb_ref[...], (tb, CONV_WIDTH))
    for k in range(CONV_K):
        acc = acc + dww_ref[k:k + 1, :] * vbuf[pl.ds(CONV_HALO - (CONV_K - 1) + k, tb), :]
    vbuf[0:CONV_HALO, :] = vbuf[tb:tb + CONV_HALO, :]
    mu = jnp.mean(acc, axis=-1, keepdims=True)
    cen = acc - mu
    var = jnp.mean(cen * cen, axis=-1, keepdims=True)
    yn = cen * lax.rsqrt(var + NORM_EPS) * lng_ref[...] + lnb_ref[...]
    y_conv = yn * _sigmoid(yn)

    o_ref[...] = (x_ref[...] + _dot(y_ssm.astype(bf16), wout_ref[0:S5_WIDTH, :])
                  + _dot(y_conv.astype(bf16), wout_ref[S5_WIDTH:S5_WIDTH + CONV_WIDTH, :]))


def _s5_tables(lam_re, lam_im, log_step, b_re, b_im, c_re, c_im):
    n_chunks = S5_LANES // LANES
    g_per_chunk = LANES // S5_STATE
    lam = lax.complex(lam_re, lam_im)
    dt = jnp.exp(log_step)[:, None]
    a_bar = jnp.exp(lam * dt)
    b_bar = ((a_bar - 1.0) / lam)[..., None] * lax.complex(b_re, b_im)
    n_steps = SCAN_TILE.bit_length() - 1
    powers = jnp.stack([jnp.exp(lam * dt * float(1 << s)) for s in range(n_steps)]
                       + [jnp.ones_like(a_bar)] * (SUBLANES - n_steps))
    powers = powers.reshape(SUBLANES, n_chunks, LANES).transpose(1, 0, 2)
    eye = jnp.eye(S5_GROUPS, dtype=f32)

    def in_mat(b):
        m = jnp.einsum('gph,gk->ghkp', b, eye).reshape(S5_WIDTH, S5_LANES)
        return m.reshape(S5_WIDTH, n_chunks, LANES).transpose(1, 0, 2).astype(bf16)

    def out_mat(c):
        m = jnp.einsum('ghp,gk->gpkh', c, eye).reshape(S5_LANES, S5_WIDTH)
        return m.reshape(n_chunks, LANES, S5_WIDTH).astype(bf16)

    del g_per_chunk
    return (jnp.real(powers), jnp.imag(powers), in_mat(jnp.real(b_bar)), in_mat(jnp.imag(b_bar)),
            out_mat(c_re), out_mat(c_im))


def _mixer(z, x, s5_tabs, d_vec, w_glu, dw_w, dw_b, ln_g, ln_b, w_out):
    bsz, seq, zw = z.shape
    d = x.shape[-1]
    apr, api, bre, bim, cre, cim = s5_tabs
    n_chunks = S5_LANES // LANES
    tb = MIX_TILE
    const2 = lambda b, n: (0, 0)
    const3 = lambda b, n: (0, 0, 0)
    return pl.pallas_call(
        _mixer_kernel,
        grid=(bsz, seq // tb),
        in_specs=[pl.BlockSpec((None, tb, zw), lambda b, n: (b, n, 0)),
                  pl.BlockSpec((None, tb, d), lambda b, n: (b, n, 0)),
                  pl.BlockSpec((n_chunks, SUBLANES, LANES), const3),
                  pl.BlockSpec((n_chunks, SUBLANES, LANES), const3),
                  pl.BlockSpec((n_chunks, S5_WIDTH, LANES), const3),
                  pl.BlockSpec((n_chunks, S5_WIDTH, LANES), const3),
                  pl.BlockSpec((n_chunks, LANES, S5_WIDTH), const3),
                  pl.BlockSpec((n_chunks, LANES, S5_WIDTH), const3),
                  pl.BlockSpec((1, S5_WIDTH), const2),
                  pl.BlockSpec((S5_WIDTH, S5_WIDTH), const2),
                  pl.BlockSpec((CONV_K, CONV_WIDTH), const2),
                  pl.BlockSpec((1, CONV_WIDTH), const2),
                  pl.BlockSpec((1, CONV_WIDTH), const2),
                  pl.BlockSpec((1, CONV_WIDTH), const2),
                  pl.BlockSpec((S5_WIDTH + CONV_WIDTH, d), const2)],
        out_specs=pl.BlockSpec((None, tb, d), lambda b, n: (b, n, 0)),
        out_shape=jax.ShapeDtypeStruct((bsz, seq, d), f32),
        scratch_shapes=[pltpu.VMEM((n_chunks, 1, LANES), f32),
                        pltpu.VMEM((n_chunks, 1, LANES), f32),
                        pltpu.VMEM((tb, S5_WIDTH), f32),
                        pltpu.VMEM((tb + CONV_HALO, CONV_WIDTH), f32)],
        compiler_params=_tc_params("parallel", "arbitrary"),
        name="mixer",
    )(z, x, apr, api, bre, bim, cre, cim, d_vec.reshape(1, S5_WIDTH), w_glu.astype(bf16),
      dw_w, dw_b.reshape(1, -1), ln_g.reshape(1, -1), ln_b.reshape(1, -1), w_out.astype(bf16))


def _attn_kernel(q_ref, kvc_ref, kvp_ref, x_ref, sink_ref, wo_ref, o_ref, o_s):
    n = pl.program_id(1)
    qi = lax.broadcasted_iota(i32, (WINDOW, 2 * WINDOW), 0)
    si = lax.broadcasted_iota(i32, (WINDOW, 2 * WINDOW), 1)
    first_key = jnp.where(n > 0, 0, WINDOW)
    valid = (si > qi) & (si <= qi + WINDOW) & (si >= first_key)
    kv = jnp.concatenate([kvp_ref[...], kvc_ref[...]], axis=0).astype(bf16)
    for kh in range(N_KV_HEADS):
        k = kv[:, kh * HEAD_DIM:(kh + 1) * HEAD_DIM]
        v = kv[:, (N_KV_HEADS + kh) * HEAD_DIM:(N_KV_HEADS + kh + 1) * HEAD_DIM]
        for g in range(Q_PER_KV):
            h = kh * Q_PER_KV + g
            q = q_ref[:, h * HEAD_DIM:(h + 1) * HEAD_DIM].astype(bf16)
            s = lax.dot_general(q, k, (((1,), (1,)), ((), ())), preferred_element_type=f32)
            s = jnp.where(valid, s, -jnp.inf)
            sink = sink_ref[0:1, h:h + 1]
            m = jnp.maximum(jnp.max(s, axis=-1, keepdims=True), sink)
            p = jnp.exp(s - m)
            denom = jnp.sum(p, axis=-1, keepdims=True) + jnp.exp(sink - m)
            probs = (p / denom).astype(bf16)
            o_s[:, h * HEAD_DIM:(h + 1) * HEAD_DIM] = _dot(probs, v)
    o_ref[...] = x_ref[...] + _dot(o_s[...].astype(bf16), wo_ref[...])


def _attention(qkv, x, sinks, w_o):
    bsz, seq, _ = qkv.shape
    d = x.shape[-1]
    kv_block = Q_WIDTH // KV_WIDTH
    return pl.pallas_call(
        _attn_kernel,
        grid=(bsz, seq // WINDOW),
        in_specs=[pl.BlockSpec((None, WINDOW, Q_WIDTH), lambda b, n: (b, n, 0)),
                  pl.BlockSpec((None, WINDOW, KV_WIDTH), lambda b, n: (b, n, kv_block)),
                  pl.BlockSpec((None, WINDOW, KV_WIDTH), lambda b, n: (b, jnp.maximum(n - 1, 0), kv_block)),
                  pl.BlockSpec((None, WINDOW, d), lambda b, n: (b, n, 0)),
                  pl.BlockSpec((1, N_Q_HEADS), lambda b, n: (0, 0)),
                  pl.BlockSpec((Q_WIDTH, d), lambda b, n: (0, 0))],
        out_specs=pl.BlockSpec((None, WINDOW, d), lambda b, n: (b, n, 0)),
        out_shape=jax.ShapeDtypeStruct((bsz, seq, d), f32),
        scratch_shapes=[pltpu.VMEM((WINDOW, Q_WIDTH), f32)],
        compiler_params=_tc_params("parallel", "parallel"),
        name="attention",
    )(qkv, qkv, qkv, x, sinks.reshape(1, N_Q_HEADS), w_o.astype(bf16))


def _top16(s, ids=None):
    m_rows = s.shape[0]
    pos = lax.broadcasted_iota(i32, s.shape, 0).astype(f32)
    vals, outs = [], []
    for _ in range(PEER_TOPK):
        mx = jnp.max(s, axis=0, keepdims=True)
        first = jnp.min(jnp.where(s == mx, pos, float(m_rows)), axis=0, keepdims=True)
        hit = pos == first
        vals.append(mx)
        outs.append(first if ids is None else jnp.sum(jnp.where(hit, ids, 0.0), axis=0, keepdims=True))
        s = jnp.where(hit, -jnp.inf, s)
    return jnp.concatenate(vals, axis=0), jnp.concatenate(outs, axis=0)


def _pair_candidates(a0, a1, combine, fill):
    n = a0.shape[1]
    sub = lax.broadcasted_iota(i32, (SUBLANES, n), 0)
    pieces = [combine(a0[0:1], a1)]
    for i in range(1, SUBLANES):
        keep = PEER_TOPK // (i + 1)
        pieces.append(jnp.where(sub < keep, combine(a0[i:i + 1], a1[0:SUBLANES]), fill))
    pieces.append(combine(a0[SUBLANES:PEER_TOPK], a1[0:1]))
    return jnp.concatenate(pieces, axis=0)


def _topk_kernel(q_ref, keys_ref, ids_ref, gates_ref, ids_s, gates_s):
    def head_body(h, carry):
        tops = []
        for c in range(2):
            q = q_ref[2 * h + c].astype(bf16)
            s = lax.dot_general(keys_ref[2 * h + c], q, (((1,), (1,)), ((), ())),
                                preferred_element_type=f32)
            tops.append(_top16(s))
        (s0, i0), (s1, i1) = tops
        cand_s = _pair_candidates(s0, s1, lambda a, b: a + b, -jnp.inf)
        cand_i = _pair_candidates(i0, i1, lambda a, b: a * float(PEER_N_KEYS) + b, 0.0)
        best_s, best_i = _top16(cand_s, cand_i)
        e = jnp.exp(best_s - best_s[0:1])
        off = pl.multiple_of(h * PEER_TOPK, PEER_TOPK)
        gates_s[pl.ds(off, PEER_TOPK), :] = e / jnp.sum(e, axis=0, keepdims=True)
        ids_s[pl.ds(off, PEER_TOPK), :] = best_i.astype(i32)
        return carry

    lax.fori_loop(0, PEER_HEADS, head_body, 0)
    ids_ref[...] = ids_s[...].T
    gates_ref[...] = gates_s[...].T


def _peer_topk(q3, sub_keys):
    nq, t, half = q3.shape
    keys = sub_keys.reshape(nq, PEER_N_KEYS, half).astype(bf16)
    tb = TOPK_TILE
    return pl.pallas_call(
        _topk_kernel,
        grid=(t // tb,),
        in_specs=[pl.BlockSpec((nq, tb, half), lambda i: (0, i, 0)),
                  pl.BlockSpec((nq, PEER_N_KEYS, half), lambda i: (0, 0, 0))],
        out_specs=[pl.BlockSpec((tb, N_SLOTS), lambda i: (i, 0)),
                   pl.BlockSpec((tb, N_SLOTS), lambda i: (i, 0))],
        out_shape=[jax.ShapeDtypeStruct((t, N_SLOTS), i32), jax.ShapeDtypeStruct((t, N_SLOTS), f32)],
        scratch_shapes=[pltpu.VMEM((N_SLOTS, tb), i32), pltpu.VMEM((N_SLOTS, tb), f32)],
        compiler_params=_tc_params("parallel"),
        name="peer_topk",
    )(q3, keys)


def _gate_kernel(a_ref, g_ref, w_ref):
    w_ref[...] = _gelu(a_ref[...]) * g_ref[...]


def _peer_gate(a, gates):
    t, n = a.shape
    tm = 2048
    spec = pl.BlockSpec((tm, n), lambda i: (i, 0))
    return pl.pallas_call(
        _gate_kernel, grid=(t // tm,), in_specs=[spec, spec], out_specs=spec,
        out_shape=jax.ShapeDtypeStruct((t, n), f32),
        compiler_params=_tc_params("parallel"), name="peer_gate",
    )(a, gates)


def _sc_params():
    return pltpu.CompilerParams(needs_layout_passes=False)


def _pack_bf16_pairs(a):
    bits = lax.bitcast_convert_type(a.astype(bf16), jnp.uint16).astype(u32)
    w = a.shape[1] // 2
    return bits[:, :w] | (bits[:, w:] << 16)


def _halves_f32(pairs_bf16):
    word = plsc.bitcast(pairs_bf16, u32)
    return plsc.bitcast(word << 16, f32), plsc.bitcast(word & jnp.uint32(0xFFFF0000), f32)


def _peer_sc_stage(side, ids, tab, mode):
    t_total, side_w = side.shape
    dw = tab.shape[1]
    out_w = N_SLOTS if mode == "dot" else 2 * dw
    tpw = t_total // SC_WORKERS
    n_pairs = tpw // 2
    assert tpw * SC_WORKERS == t_total and n_pairs * 2 == tpw and N_BATCH % 2 == 0
    n_grp = ROW_BATCH // SC_LANES
    mesh = plsc.VectorSubcoreMesh(core_axis_name="c", subcore_axis_name="s")

    @functools.partial(
        pl.kernel, mesh=mesh,
        out_type=jax.ShapeDtypeStruct((t_total, out_w), f32),
        scratch_types=[
            pltpu.VMEM((N_SLOTS,), i32), pltpu.VMEM((N_SLOTS,), i32),
            pltpu.VMEM((side_w,), side.dtype), pltpu.VMEM((side_w,), side.dtype),
            pltpu.VMEM((out_w,), f32), pltpu.VMEM((out_w,), f32),
            pltpu.VMEM((2, ROW_BATCH, dw), u32),
            pltpu.VMEM((SC_LANES, SC_LANES), f32),
            pltpu.SemaphoreType.DMA((2,)), pltpu.SemaphoreType.DMA((2,)),
            pltpu.SemaphoreType.DMA((2,)), pltpu.SemaphoreType.DMA((2,)),
        ],
        compiler_params=_sc_params())
    def k(side_hbm, ids_hbm, tab_hbm, out_hbm, idx0, idx1, side0, side1, out0, out1, rows_v, tmp_v,
          sem_g, sem_i, sem_s, sem_o):
        idx_v, side_v, out_v = (idx0, idx1), (side0, side1), (out0, out1)
        wid = lax.axis_index("s") * SC_CORES + lax.axis_index("c")
        base = wid * tpw
        iota = lax.iota(i32, SC_LANES)

        def row_pairs(slot, r, off):
            return plsc.bitcast(rows_v[slot, r, pl.ds(off, SC_LANES)], bf16)

        def gather(p, b):
            return pltpu.make_async_copy(
                tab_hbm.at[idx_v[p].at[pl.ds(b * ROW_BATCH, ROW_BATCH)]], rows_v.at[b % 2], sem_g.at[b % 2])

        def loads(t, p):
            return (pltpu.make_async_copy(ids_hbm.at[t], idx_v[p], sem_i.at[p]),
                    pltpu.make_async_copy(side_hbm.at[t], side_v[p], sem_s.at[p]))

        def store(t, p):
            return pltpu.make_async_copy(out_v[p], out_hbm.at[t], sem_o.at[p])

        def compute_dot(p, b):
            slot = b % 2
            for g in range(n_grp):
                def cbody(c, accs):
                    off = pl.multiple_of(c * 2 * SC_LANES, 2 * SC_LANES)
                    xa = plsc.bitcast(side_v[p][pl.ds(off, SC_LANES)], bf16)
                    xb = plsc.bitcast(side_v[p][pl.ds(off + SC_LANES, SC_LANES)], bf16)
                    out = []
                    for j in range(SC_LANES):
                        r = g * SC_LANES + j
                        lo, hi = _halves_f32(row_pairs(slot, r, off) * xa + row_pairs(slot, r, off + SC_LANES) * xb)
                        out.append(accs[j] + (lo + hi))
                    return tuple(out)
                accs = lax.fori_loop(0, dw // (2 * SC_LANES), cbody,
                                     tuple(jnp.zeros((SC_LANES,), f32) for _ in range(SC_LANES)))
                for j in range(SC_LANES):
                    tmp_v[j, :] = accs[j]
                r = jnp.zeros((SC_LANES,), f32)
                for l in range(SC_LANES):
                    r = r + plsc.load_gather(tmp_v, [iota, jnp.full((SC_LANES,), l, i32)])
                out_v[p][pl.ds(b * ROW_BATCH + g * SC_LANES, SC_LANES)] = r

        def compute_wsum(p, b):
            slot = b % 2
            for g in range(n_grp):
                sp = []
                for j in range(SC_LANES):
                    w = plsc.load_gather(side_v[p], [jnp.full((SC_LANES,), b * ROW_BATCH + g * SC_LANES + j, i32)])
                    sp.append(plsc.pack(w, w, format=plsc.PackFormat.INTERLEAVED))
                first = b == 0 and g == 0

                @plsc.parallel_loop(0, dw, step=SC_LANES)
                def _(c):
                    off = pl.multiple_of(c, SC_LANES)
                    t = [row_pairs(slot, g * SC_LANES + j, off) * sp[j] for j in range(SC_LANES)]
                    quads = [_halves_f32((t[q] + t[q + 1]) + (t[q + 2] + t[q + 3])) for q in range(0, SC_LANES, 4)]
                    lo = (quads[0][0] + quads[1][0]) + (quads[2][0] + quads[3][0])
                    hi = (quads[0][1] + quads[1][1]) + (quads[2][1] + quads[3][1])
                    if first:
                        out_v[p][pl.ds(off, SC_LANES)] = lo
                        out_v[p][pl.ds(dw + off, SC_LANES)] = hi
                    else:
                        plsc.addupdate(out_v[p].at[pl.ds(off, SC_LANES)], lo)
                        plsc.addupdate(out_v[p].at[pl.ds(dw + off, SC_LANES)], hi)

        compute = compute_dot if mode == "dot" else compute_wsum

        for c in loads(base, 0):
            c.start()
        for c in loads(base, 0):
            c.wait()
        gather(0, 0).start()
        for c in loads(base + 1, 1):
            c.start()

        def pair_body(i2, carry):
            not_last = i2 < n_pairs - 1
            for p in (0, 1):
                t = base + 2 * i2 + p

                @pl.when(i2 > 0)
                def _():
                    store(t - 2, p).wait()

                for b in range(N_BATCH):
                    if b + 1 < N_BATCH:
                        gather(p, b + 1).start()
                    else:
                        def start_next():
                            for c in loads(t + 1, 1 - p):
                                c.wait()
                            gather(1 - p, 0).start()
                        if p == 0:
                            start_next()
                        else:
                            pl.when(not_last)(start_next)
                    gather(p, b).wait()
                    compute(p, b)
                store(t, p).start()

                @pl.when(not_last)
                def _():
                    for c in loads(t + 2, p):
                        c.start()
            return carry

        lax.fori_loop(0, n_pairs, pair_body, 0)
        store(base + tpw - 2, 0).wait()
        store(base + tpw - 1, 1).wait()

    return k(side, ids, tab)


def _peer(x2d, g, w_q, sub_keys, u_tab, v_tab):
    h, q3 = _peer_q(x2d, g, w_q)
    ids, gates = _peer_topk(q3, sub_keys)
    a = _peer_sc_stage(_pack_bf16_pairs(h), ids, _pack_bf16_pairs(u_tab), "dot")
    w = _peer_gate(a, gates)
    return _peer_sc_stage(w, ids, _pack_bf16_pairs(v_tab), "wsum")


def kernel(x, norm_mix, norm_ffn, norm_final, mix_w_in, s5_lambda_re, s5_lambda_im, s5_log_step, s5_b_re, s5_b_im, s5_c_re, s5_c_im, s5_d, s5_w_glu, conv_dw_w, conv_dw_b, conv_ln_g, conv_ln_b, mix_w_out, attn_w_qkv, attn_b_qkv, attn_sinks, attn_w_o, peer_w_q, peer_sub_keys, peer_u, peer_v):
    bsz, seq, d = x.shape
    tabs = _s5_tables(s5_lambda_re[0], s5_lambda_im[0], s5_log_step[0], s5_b_re[0], s5_b_im[0],
                      s5_c_re[0], s5_c_im[0])

    def trunk(xb):
        x0 = xb.reshape(seq, d)
        z = _mix_in(x0, norm_mix[0], mix_w_in[0])
        x1 = _mixer(z.reshape(1, seq, -1), xb, tabs, s5_d[0], s5_w_glu[0], conv_dw_w[0], conv_dw_b[0],
                    conv_ln_g[0], conv_ln_b[0], mix_w_out[0]).reshape(seq, d)
        p0 = _peer(x1, norm_ffn[0], peer_w_q[0], peer_sub_keys[0], peer_u[0], peer_v[0])
        x2, qkv = _qkv(x1, p0, norm_mix[1], attn_w_qkv[0], attn_b_qkv[0], seq)
        x3 = _attention(qkv.reshape(1, seq, -1), x2.reshape(1, seq, d), attn_sinks[0], attn_w_o[0]).reshape(seq, d)
        p1 = _peer(x3, norm_ffn[1], peer_w_q[1], peer_sub_keys[1], peer_u[1], peer_v[1])
        return _final_norm(x3, p1, norm_final).reshape(1, seq, d)

    return jnp.concatenate([trunk(x[b:b + 1]) for b in range(bsz)], axis=0)
```

```python
import functools

import jax
import jax.numpy as jnp
from jax import lax
from jax.experimental import pallas as pl
from jax.experimental.pallas import tpu as pltpu
from jax.experimental.pallas import tpu_sc as plsc

f32 = jnp.float32
bf16 = jnp.bfloat16
i32 = jnp.int32
u32 = jnp.uint32

D_MODEL = 1024
S5_WIDTH = 512
S5_GROUP = 16
S5_GROUPS = 32
S5_STATE = 64
S5_LANES = S5_GROUPS * S5_STATE
CONV_WIDTH = 512
CONV_K = 31
HEAD_DIM = 64
N_Q_HEADS = 16
N_KV_HEADS = 2
Q_PER_KV = 8
Q_WIDTH = N_Q_HEADS * HEAD_DIM
KV_WIDTH = 2 * N_KV_HEADS * HEAD_DIM
QKV_WIDTH = Q_WIDTH + KV_WIDTH
WINDOW = 128
ROPE_THETA = 500000.0
ROPE_DIM = 16
PEER_HEADS = 8
PEER_N_KEYS = 128
PEER_HALF = 128
PEER_TOPK = 16
N_SLOTS = PEER_HEADS * PEER_TOPK
NORM_EPS = 1e-6

LANES = 128
SUBLANES = 8
TC_VMEM_LIMIT = 48 * 1024 * 1024

SC_CORES = 2
SC_SUBCORES = 16
SC_LANES = 16
SC_WORKERS = SC_CORES * SC_SUBCORES
ROW_BATCH = 32
N_BATCH = N_SLOTS // ROW_BATCH

ROW_TILE = 512
MIX_TILE = 256
SCAN_TILE = 128
TOPK_TILE = 128
PEER_CHUNKS = 4
CONV_HALO = 32


def _tc_params(*sem):
    return pltpu.CompilerParams(dimension_semantics=sem, vmem_limit_bytes=TC_VMEM_LIMIT)


def _rms(x, g):
    return x * lax.rsqrt(jnp.mean(x * x, axis=-1, keepdims=True) + NORM_EPS) * g


def _sigmoid(x):
    return 1.0 / (1.0 + jnp.exp(-x))


def _gelu(x):
    return 0.5 * x * (1.0 + lax.erf(x * 0.7071067811865476))


def _dot(a, b):
    return jnp.dot(a, b, preferred_element_type=f32)


def _mix_in_kernel(x_ref, g_ref, w_ref, z_ref):
    h = _rms(x_ref[...], g_ref[...])
    z_ref[...] = _dot(h.astype(bf16), w_ref[...])


def _mix_in(x2d, g, w):
    t, d = x2d.shape
    n = w.shape[1]
    return pl.pallas_call(
        _mix_in_kernel,
        grid=(t // ROW_TILE,),
        in_specs=[pl.BlockSpec((ROW_TILE, d), lambda i: (i, 0)),
                  pl.BlockSpec((1, d), lambda i: (0, 0)),
                  pl.BlockSpec((d, n), lambda i: (0, 0))],
        out_specs=pl.BlockSpec((ROW_TILE, n), lambda i: (i, 0)),
        out_shape=jax.ShapeDtypeStruct((t, n), f32),
        compiler_params=_tc_params("parallel"),
        name="mix_in",
    )(x2d, g.reshape(1, d), w.astype(bf16))


def _peer_q_kernel(x_ref, g_ref, w_ref, h_ref, q_ref):
    h = _rms(x_ref[...], g_ref[...])
    h_ref[...] = h
    q = _dot(h.astype(bf16), w_ref[...])
    for c in range(2 * PEER_HEADS):
        q_ref[c] = q[:, c * PEER_HALF:(c + 1) * PEER_HALF]


def _peer_q(x2d, g, w_q):
    t, d = x2d.shape
    nq = 2 * PEER_HEADS
    return pl.pallas_call(
        _peer_q_kernel,
        grid=(t // ROW_TILE,),
        in_specs=[pl.BlockSpec((ROW_TILE, d), lambda i: (i, 0)),
                  pl.BlockSpec((1, d), lambda i: (0, 0)),
                  pl.BlockSpec((d, nq * PEER_HALF), lambda i: (0, 0))],
        out_specs=[pl.BlockSpec((ROW_TILE, d), lambda i: (i, 0)),
                   pl.BlockSpec((nq, ROW_TILE, PEER_HALF), lambda i: (0, i, 0))],
        out_shape=[jax.ShapeDtypeStruct((t, d), f32),
                   jax.ShapeDtypeStruct((nq, t, PEER_HALF), f32)],
        compiler_params=_tc_params("parallel"),
        name="peer_q",
    )(x2d, g.reshape(1, d), w_q.astype(bf16))


def _qkv_kernel(x_ref, r_ref, g_ref, w_ref, b_ref, cos_ref, sin_ref, xo_ref, qkv_ref):
    x = x_ref[...] + r_ref[...]
    xo_ref[...] = x
    h = _rms(x, g_ref[...])
    qkv = _dot(h.astype(bf16), w_ref[...]) + b_ref[...]
    cos = cos_ref[...]
    sin = sin_ref[...]
    lane = lax.broadcasted_iota(i32, cos.shape, 1)
    low = (lane % HEAD_DIM) < (ROPE_DIM // 2)
    n_rot = (Q_WIDTH + N_KV_HEADS * HEAD_DIM) // LANES
    for c in range(QKV_WIDTH // LANES):
        t = qkv[:, c * LANES:(c + 1) * LANES]
        if c < n_rot:
            partner = jnp.where(low, pltpu.roll(t, LANES - ROPE_DIM // 2, axis=1),
                                pltpu.roll(t, ROPE_DIM // 2, axis=1))
            t = t * cos + partner * sin
        if c < Q_WIDTH // LANES:
            t = t * (HEAD_DIM ** -0.5)
        qkv_ref[:, c * LANES:(c + 1) * LANES] = t


def _rope_tables(seq):
    inv = jnp.power(ROPE_THETA, -jnp.arange(0, ROPE_DIM, 2, dtype=f32) / ROPE_DIM)
    ang = jnp.arange(seq, dtype=f32)[:, None] * inv[None, :]
    cos, sin = jnp.cos(ang), jnp.sin(ang)
    pad = HEAD_DIM - ROPE_DIM
    cos_h = jnp.concatenate([cos, cos, jnp.ones((seq, pad), f32)], axis=1)
    sin_h = jnp.concatenate([-sin, sin, jnp.zeros((seq, pad), f32)], axis=1)
    return jnp.tile(cos_h, (1, LANES // HEAD_DIM)), jnp.tile(sin_h, (1, LANES // HEAD_DIM))


def _qkv(x2d, r2d, g, w, b, seq):
    t, d = x2d.shape
    n = w.shape[1]
    cos, sin = _rope_tables(seq)
    blocks_per_seq = seq // ROW_TILE
    return pl.pallas_call(
        _qkv_kernel,
        grid=(t // ROW_TILE,),
        in_specs=[pl.BlockSpec((ROW_TILE, d), lambda i: (i, 0)),
                  pl.BlockSpec((ROW_TILE, d), lambda i: (i, 0)),
                  pl.BlockSpec((1, d), lambda i: (0, 0)),
                  pl.BlockSpec((d, n), lambda i: (0, 0)),
                  pl.BlockSpec((1, n), lambda i: (0, 0)),
                  pl.BlockSpec((ROW_TILE, LANES), lambda i: (i % blocks_per_seq, 0)),
                  pl.BlockSpec((ROW_TILE, LANES), lambda i: (i % blocks_per_seq, 0))],
        out_specs=[pl.BlockSpec((ROW_TILE, d), lambda i: (i, 0)),
                   pl.BlockSpec((ROW_TILE, n), lambda i: (i, 0))],
        out_shape=[jax.ShapeDtypeStruct((t, d), f32), jax.ShapeDtypeStruct((t, n), f32)],
        compiler_params=_tc_params("parallel"),
        name="qkv",
    )(x2d, r2d, g.reshape(1, d), w.astype(bf16), b.reshape(1, n), cos, sin)


def _final_norm_kernel(x_ref, r_ref, g_ref, o_ref):
    o_ref[...] = _rms(x_ref[...] + r_ref[...], g_ref[...])


def _final_norm(x2d, r2d, g):
    t, d = x2d.shape
    spec = pl.BlockSpec((ROW_TILE, d), lambda i: (i, 0))
    return pl.pallas_call(
        _final_norm_kernel,
        grid=(t // ROW_TILE,),
        in_specs=[spec, spec, pl.BlockSpec((1, d), lambda i: (0, 0))],
        out_specs=spec,
        out_shape=jax.ShapeDtypeStruct((t, d), f32),
        compiler_params=_tc_params("parallel"),
        name="final_norm",
    )(x2d, r2d, g.reshape(1, d))


def _shift_rows(v, d, row):
    if d % SUBLANES == 0:
        return jnp.concatenate([jnp.zeros((d, v.shape[1]), v.dtype), v[:v.shape[0] - d]], axis=0)
    return jnp.where(row >= d, pltpu.roll(v, d, axis=0), 0.0)


def _mixer_kernel(z_ref, x_ref, apr_ref, api_ref, bre_ref, bim_ref, cre_ref, cim_ref, dvec_ref, wglu_ref,
                  dww_ref, dwb_ref, lng_ref, lnb_ref, wout_ref, o_ref,
                  carry_re, carry_im, y_s, vbuf):
    tb = z_ref.shape[0]
    n_chunks = S5_LANES // LANES
    steps = [1 << s for s in range(SCAN_TILE.bit_length() - 1)]

    @pl.when(pl.program_id(1) == 0)
    def _():
        carry_re[...] = jnp.zeros_like(carry_re)
        carry_im[...] = jnp.zeros_like(carry_im)
        vbuf[0:CONV_HALO, :] = jnp.zeros((CONV_HALO, CONV_WIDTH), f32)

    u = z_ref[:, 0:S5_WIDTH]
    ub = u.astype(bf16)
    y_s[...] = u * dvec_ref[...]
    row = lax.broadcasted_iota(i32, (SCAN_TILE, LANES), 0)

    def chunk_body(j, carry):
        bu_re = _dot(ub, bre_ref[j])
        bu_im = _dot(ub, bim_ref[j])
        apr = apr_ref[j]
        api = api_ref[j]
        c_re = carry_re[j]
        c_im = carry_im[j]
        h_re_tiles, h_im_tiles = [], []
        for r in range(tb // SCAN_TILE):
            hr = bu_re[r * SCAN_TILE:(r + 1) * SCAN_TILE]
            hi = bu_im[r * SCAN_TILE:(r + 1) * SCAN_TILE]
            a_re, a_im = apr[0:1], api[0:1]
            first = row == 0
            hr = hr + jnp.where(first, a_re * c_re - a_im * c_im, 0.0)
            hi = hi + jnp.where(first, a_re * c_im + a_im * c_re, 0.0)
            for s, d in enumerate(steps):
                a_re, a_im = apr[s:s + 1], api[s:s + 1]
                sr = _shift_rows(hr, d, row)
                si = _shift_rows(hi, d, row)
                hr, hi = hr + (a_re * sr - a_im * si), hi + (a_re * si + a_im * sr)
            c_re = hr[SCAN_TILE - 1:SCAN_TILE]
            c_im = hi[SCAN_TILE - 1:SCAN_TILE]
            h_re_tiles.append(hr.astype(bf16))
            h_im_tiles.append(hi.astype(bf16))
        carry_re[j] = c_re
        carry_im[j] = c_im
        h_re = jnp.concatenate(h_re_tiles, axis=0)
        h_im = jnp.concatenate(h_im_tiles, axis=0)
        y_s[...] += _dot(h_re, cre_ref[j]) - _dot(h_im, cim_ref[j])
        return carry

    lax.fori_loop(0, n_chunks, chunk_body, 0)
    y = _gelu(y_s[...])
    y_ssm = y * _sigmoid(_dot(y.astype(bf16), wglu_ref[...]))

    v = z_ref[:, S5_WIDTH:S5_WIDTH + CONV_WIDTH] * _sigmoid(z_ref[:, S5_WIDTH + CONV_WIDTH:S5_WIDTH + 2 * CONV_WIDTH])
    vbuf[CONV_HALO:CONV_HALO + tb, :] = v
    acc = jnp.broadcast_to(dwb_ref[...], (tb, CONV_WIDTH))
    for k in range(CONV_K):
        acc = acc + dww_ref[k:k + 1, :] * vbuf[pl.ds(CONV_HALO - (CONV_K - 1) + k, tb), :]
    vbuf[0:CONV_HALO, :] = vbuf[tb:tb + CONV_HALO, :]
    mu = jnp.mean(acc, axis=-1, keepdims=True)
    cen = acc - mu
    var = jnp.mean(cen * cen, axis=-1, keepdims=True)
    yn = cen * lax.rsqrt(var + NORM_EPS) * lng_ref[...] + lnb_ref[...]
    y_conv = yn * _sigmoid(yn)

    o_ref[...] = (x_ref[...] + _dot(y_ssm.astype(bf16), wout_ref[0:S5_WIDTH, :])
                  + _dot(y_conv.astype(bf16), wout_ref[S5_WIDTH:S5_WIDTH + CONV_WIDTH, :]))


def _s5_tables(lam_re, lam_im, log_step, b_re, b_im, c_re, c_im):
    n_chunks = S5_LANES // LANES
    g_per_chunk = LANES // S5_STATE
    lam = lax.complex(lam_re, lam_im)
    dt = jnp.exp(log_step)[:, None]
    a_bar = jnp.exp(lam * dt)
    b_bar = ((a_bar - 1.0) / lam)[..., None] * lax.complex(b_re, b_im)
    n_steps = SCAN_TILE.bit_length() - 1
    powers = jnp.stack([jnp.exp(lam * dt * float(1 << s)) for s in range(n_steps)]
                       + [jnp.ones_like(a_bar)] * (SUBLANES - n_steps))
    powers = powers.reshape(SUBLANES, n_chunks, LANES).transpose(1, 0, 2)
    eye = jnp.eye(S5_GROUPS, dtype=f32)

    def in_mat(b):
        m = jnp.einsum('gph,gk->ghkp', b, eye).reshape(S5_WIDTH, S5_LANES)
        return m.reshape(S5_WIDTH, n_chunks, LANES).transpose(1, 0, 2).astype(bf16)

    def out_mat(c):
        m = jnp.einsum('ghp,gk->gpkh', c, eye).reshape(S5_LANES, S5_WIDTH)
        return m.reshape(n_chunks, LANES, S5_WIDTH).astype(bf16)

    del g_per_chunk
    return (jnp.real(powers), jnp.imag(powers), in_mat(jnp.real(b_bar)), in_mat(jnp.imag(b_bar)),
            out_mat(c_re), out_mat(c_im))


def _mixer(z, x, s5_tabs, d_vec, w_glu, dw_w, dw_b, ln_g, ln_b, w_out):
    bsz, seq, zw = z.shape
    d = x.shape[-1]
    apr, api, bre, bim, cre, cim = s5_tabs
    n_chunks = S5_LANES // LANES
    tb = MIX_TILE
    const2 = lambda b, n: (0, 0)
    const3 = lambda b, n: (0, 0, 0)
    return pl.pallas_call(
        _mixer_kernel,
        grid=(bsz, seq // tb),
        in_specs=[pl.BlockSpec((None, tb, zw), lambda b, n: (b, n, 0)),
                  pl.BlockSpec((None, tb, d), lambda b, n: (b, n, 0)),
                  pl.BlockSpec((n_chunks, SUBLANES, LANES), const3),
                  pl.BlockSpec((n_chunks, SUBLANES, LANES), const3),
                  pl.BlockSpec((n_chunks, S5_WIDTH, LANES), const3),
                  pl.BlockSpec((n_chunks, S5_WIDTH, LANES), const3),
                  pl.BlockSpec((n_chunks, LANES, S5_WIDTH), const3),
                  pl.BlockSpec((n_chunks, LANES, S5_WIDTH), const3),
                  pl.BlockSpec((1, S5_WIDTH), const2),
                  pl.BlockSpec((S5_WIDTH, S5_WIDTH), const2),
                  pl.BlockSpec((CONV_K, CONV_WIDTH), const2),
                  pl.BlockSpec((1, CONV_WIDTH), const2),
                  pl.BlockSpec((1, CONV_WIDTH), const2),
                  pl.BlockSpec((1, CONV_WIDTH), const2),
                  pl.BlockSpec((S5_WIDTH + CONV_WIDTH, d), const2)],
        out_specs=pl.BlockSpec((None, tb, d), lambda b, n: (b, n, 0)),
        out_shape=jax.ShapeDtypeStruct((bsz, seq, d), f32),
        scratch_shapes=[pltpu.VMEM((n_chunks, 1, LANES), f32),
                        pltpu.VMEM((n_chunks, 1, LANES), f32),
                        pltpu.VMEM((tb, S5_WIDTH), f32),
                        pltpu.VMEM((tb + CONV_HALO, CONV_WIDTH), f32)],
        compiler_params=_tc_params("parallel", "arbitrary"),
        name="mixer",
    )(z, x, apr, api, bre, bim, cre, cim, d_vec.reshape(1, S5_WIDTH), w_glu.astype(bf16),
      dw_w, dw_b.reshape(1, -1), ln_g.reshape(1, -1), ln_b.reshape(1, -1), w_out.astype(bf16))


def _attn_kernel(q_ref, kvc_ref, kvp_ref, x_ref, sink_ref, wo_ref, o_ref, o_s):
    n = pl.program_id(1)
    qi = lax.broadcasted_iota(i32, (WINDOW, 2 * WINDOW), 0)
    si = lax.broadcasted_iota(i32, (WINDOW, 2 * WINDOW), 1)
    first_key = jnp.where(n > 0, 0, WINDOW)
    valid = (si > qi) & (si <= qi + WINDOW) & (si >= first_key)
    kv = jnp.concatenate([kvp_ref[...], kvc_ref[...]], axis=0).astype(bf16)
    for kh in range(N_KV_HEADS):
        k = kv[:, kh * HEAD_DIM:(kh + 1) * HEAD_DIM]
        v = kv[:, (N_KV_HEADS + kh) * HEAD_DIM:(N_KV_HEADS + kh + 1) * HEAD_DIM]
        for g in range(Q_PER_KV):
            h = kh * Q_PER_KV + g
            q = q_ref[:, h * HEAD_DIM:(h + 1) * HEAD_DIM].astype(bf16)
            s = lax.dot_general(q, k, (((1,), (1,)), ((), ())), preferred_element_type=f32)
            s = jnp.where(valid, s, -jnp.inf)
            sink = sink_ref[0:1, h:h + 1]
            m = jnp.maximum(jnp.max(s, axis=-1, keepdims=True), sink)
            p = jnp.exp(s - m)
            denom = jnp.sum(p, axis=-1, keepdims=True) + jnp.exp(sink - m)
            probs = (p / denom).astype(bf16)
            o_s[:, h * HEAD_DIM:(h + 1) * HEAD_DIM] = _dot(probs, v)
    o_ref[...] = x_ref[...] + _dot(o_s[...].astype(bf16), wo_ref[...])


def _attention(qkv, x, sinks, w_o):
    bsz, seq, _ = qkv.shape
    d = x.shape[-1]
    kv_block = Q_WIDTH // KV_WIDTH
    return pl.pallas_call(
        _attn_kernel,
        grid=(bsz, seq // WINDOW),
        in_specs=[pl.BlockSpec((None, WINDOW, Q_WIDTH), lambda b, n: (b, n, 0)),
                  pl.BlockSpec((None, WINDOW, KV_WIDTH), lambda b, n: (b, n, kv_block)),
                  pl.BlockSpec((None, WINDOW, KV_WIDTH), lambda b, n: (b, jnp.maximum(n - 1, 0), kv_block)),
                  pl.BlockSpec((None, WINDOW, d), lambda b, n: (b, n, 0)),
                  pl.BlockSpec((1, N_Q_HEADS), lambda b, n: (0, 0)),
                  pl.BlockSpec((Q_WIDTH, d), lambda b, n: (0, 0))],
        out_specs=pl.BlockSpec((None, WINDOW, d), lambda b, n: (b, n, 0)),
        out_shape=jax.ShapeDtypeStruct((bsz, seq, d), f32),
        scratch_shapes=[pltpu.VMEM((WINDOW, Q_WIDTH), f32)],
        compiler_params=_tc_params("parallel", "parallel"),
        name="attention",
    )(qkv, qkv, qkv, x, sinks.reshape(1, N_Q_HEADS), w_o.astype(bf16))


def _top16(s, ids=None):
    m_rows = s.shape[0]
    pos = lax.broadcasted_iota(i32, s.shape, 0).astype(f32)
    vals, outs = [], []
    for _ in range(PEER_TOPK):
        mx = jnp.max(s, axis=0, keepdims=True)
        first = jnp.min(jnp.where(s == mx, pos, float(m_rows)), axis=0, keepdims=True)
        hit = pos == first
        vals.append(mx)
        outs.append(first if ids is None else jnp.sum(jnp.where(hit, ids, 0.0), axis=0, keepdims=True))
        s = jnp.where(hit, -jnp.inf, s)
    return jnp.concatenate(vals, axis=0), jnp.concatenate(outs, axis=0)


def _pair_candidates(a0, a1, combine, fill):
    n = a0.shape[1]
    sub = lax.broadcasted_iota(i32, (SUBLANES, n), 0)
    pieces = [combine(a0[0:1], a1)]
    for i in range(1, SUBLANES):
        keep = PEER_TOPK // (i + 1)
        pieces.append(jnp.where(sub < keep, combine(a0[i:i + 1], a1[0:SUBLANES]), fill))
    pieces.append(combine(a0[SUBLANES:PEER_TOPK], a1[0:1]))
    return jnp.concatenate(pieces, axis=0)


def _topk_kernel(q_ref, keys_ref, ids_ref, gates_ref, ids_s, gates_s):
    def head_body(h, carry):
        tops = []
        for c in range(2):
            q = q_ref[2 * h + c].astype(bf16)
            s = lax.dot_general(keys_ref[2 * h + c], q, (((1,), (1,)), ((), ())),
                                preferred_element_type=f32)
            tops.append(_top16(s))
        (s0, i0), (s1, i1) = tops
        cand_s = _pair_candidates(s0, s1, lambda a, b: a + b, -jnp.inf)
        cand_i = _pair_candidates(i0, i1, lambda a, b: a * float(PEER_N_KEYS) + b, 0.0)
        best_s, best_i = _top16(cand_s, cand_i)
        e = jnp.exp(best_s - best_s[0:1])
        off = pl.multiple_of(h * PEER_TOPK, PEER_TOPK)
        gates_s[pl.ds(off, PEER_TOPK), :] = e / jnp.sum(e, axis=0, keepdims=True)
        ids_s[pl.ds(off, PEER_TOPK), :] = best_i.astype(i32)
        return carry

    lax.fori_loop(0, PEER_HEADS, head_body, 0, unroll=4)
    ids_ref[...] = ids_s[...].T
    gates_ref[...] = gates_s[...].T


def _peer_topk(q3, sub_keys, chunk, n_chunks):
    nq, t_all, half = q3.shape
    keys = sub_keys.reshape(nq, PEER_N_KEYS, half).astype(bf16)
    tb = TOPK_TILE
    t = t_all // n_chunks
    first = chunk * (t // tb)
    return pl.pallas_call(
        _topk_kernel,
        grid=(t // tb,),
        in_specs=[pl.BlockSpec((nq, tb, half), lambda i: (0, first + i, 0)),
                  pl.BlockSpec((nq, PEER_N_KEYS, half), lambda i: (0, 0, 0))],
        out_specs=[pl.BlockSpec((tb, N_SLOTS), lambda i: (i, 0)),
                   pl.BlockSpec((tb, N_SLOTS), lambda i: (i, 0))],
        out_shape=[jax.ShapeDtypeStruct((t, N_SLOTS), i32), jax.ShapeDtypeStruct((t, N_SLOTS), f32)],
        scratch_shapes=[pltpu.VMEM((N_SLOTS, tb), i32), pltpu.VMEM((N_SLOTS, tb), f32)],
        compiler_params=_tc_params("parallel"),
        name="peer_topk",
    )(q3, keys)


def _gate_kernel(a_ref, g_ref, w_ref):
    w_ref[...] = _gelu(a_ref[...]) * g_ref[...]


def _peer_gate(a, gates):
    t, n = a.shape
    tm = 2048
    spec = pl.BlockSpec((tm, n), lambda i: (i, 0))
    return pl.pallas_call(
        _gate_kernel, grid=(t // tm,), in_specs=[spec, spec], out_specs=spec,
        out_shape=jax.ShapeDtypeStruct((t, n), f32),
        compiler_params=_tc_params("parallel"), name="peer_gate",
    )(a, gates)


def _sc_params():
    return pltpu.CompilerParams(needs_layout_passes=False)


def _pack_bf16_pairs(a):
    bits = lax.bitcast_convert_type(a.astype(bf16), jnp.uint16).astype(u32)
    w = a.shape[1] // 2
    return bits[:, :w] | (bits[:, w:] << 16)


def _halves_f32(pairs_bf16):
    word = plsc.bitcast(pairs_bf16, u32)
    return plsc.bitcast(word << 16, f32), plsc.bitcast(word & jnp.uint32(0xFFFF0000), f32)


def _peer_sc_stage(side, ids, tab, mode):
    t_total, side_w = side.shape
    dw = tab.shape[1]
    out_w = N_SLOTS if mode == "dot" else 2 * dw
    tpw = t_total // SC_WORKERS
    n_pairs = tpw // 2
    assert tpw * SC_WORKERS == t_total and n_pairs * 2 == tpw and N_BATCH >= 2
    n_grp = ROW_BATCH // SC_LANES
    mesh = plsc.VectorSubcoreMesh(core_axis_name="c", subcore_axis_name="s")

    @functools.partial(
        pl.kernel, mesh=mesh,
        out_type=jax.ShapeDtypeStruct((t_total, out_w), f32),
        scratch_types=[
            pltpu.VMEM((N_SLOTS,), i32), pltpu.VMEM((N_SLOTS,), i32),
            pltpu.VMEM((side_w,), side.dtype), pltpu.VMEM((side_w,), side.dtype),
            pltpu.VMEM((out_w,), f32), pltpu.VMEM((out_w,), f32),
            pltpu.VMEM((N_BATCH, ROW_BATCH, dw), u32),
            pltpu.VMEM((SC_LANES, SC_LANES), f32),
            pltpu.SemaphoreType.DMA((N_BATCH,)), pltpu.SemaphoreType.DMA((2,)),
            pltpu.SemaphoreType.DMA((2,)), pltpu.SemaphoreType.DMA((2,)),
        ],
        compiler_params=_sc_params())
    def k(side_hbm, ids_hbm, tab_hbm, out_hbm, idx0, idx1, side0, side1, out0, out1, rows_v, tmp_v,
          sem_g, sem_i, sem_s, sem_o):
        idx_v, side_v, out_v = (idx0, idx1), (side0, side1), (out0, out1)
        wid = lax.axis_index("s") * SC_CORES + lax.axis_index("c")
        base = wid * tpw
        iota = lax.iota(i32, SC_LANES)

        def row_pairs(slot, r, off):
            return plsc.bitcast(rows_v[slot, r, pl.ds(off, SC_LANES)], bf16)

        def gather(p, b):
            return pltpu.make_async_copy(
                tab_hbm.at[idx_v[p].at[pl.ds(b * ROW_BATCH, ROW_BATCH)]], rows_v.at[b], sem_g.at[b])

        def loads(t, p):
            return (pltpu.make_async_copy(ids_hbm.at[t], idx_v[p], sem_i.at[p]),
                    pltpu.make_async_copy(side_hbm.at[t], side_v[p], sem_s.at[p]))

        def store(t, p):
            return pltpu.make_async_copy(out_v[p], out_hbm.at[t], sem_o.at[p])

        def compute_dot(p, b):
            slot = b
            for g in range(n_grp):
                def cbody(c, accs):
                    off = pl.multiple_of(c * 4 * SC_LANES, 4 * SC_LANES)
                    xs = [plsc.bitcast(side_v[p][pl.ds(off + q * SC_LANES, SC_LANES)], bf16) for q in range(4)]
                    out = []
                    for j in range(SC_LANES):
                        t = [row_pairs(slot, g * SC_LANES + j, off + q * SC_LANES) * xs[q] for q in range(4)]
                        lo, hi = _halves_f32((t[0] + t[1]) + (t[2] + t[3]))
                        out.append(accs[j] + (lo + hi))
                    return tuple(out)
                accs = lax.fori_loop(0, dw // (4 * SC_LANES), cbody,
                                     tuple(jnp.zeros((SC_LANES,), f32) for _ in range(SC_LANES)))
                for j in range(SC_LANES):
                    tmp_v[j, :] = accs[j]
                r = jnp.zeros((SC_LANES,), f32)
                for l in range(SC_LANES):
                    r = r + plsc.load_gather(tmp_v, [iota, jnp.full((SC_LANES,), l, i32)])
                out_v[p][pl.ds(b * ROW_BATCH + g * SC_LANES, SC_LANES)] = r

        def compute_wsum(p, b):
            slot = b
            for g in range(n_grp):
                sp = []
                for j in range(SC_LANES):
                    w = plsc.load_gather(side_v[p], [jnp.full((SC_LANES,), b * ROW_BATCH + g * SC_LANES + j, i32)])
                    sp.append(plsc.pack(w, w, format=plsc.PackFormat.INTERLEAVED))
                first = b == 0 and g == 0

                @plsc.parallel_loop(0, dw, step=SC_LANES)
                def _(c):
                    off = pl.multiple_of(c, SC_LANES)
                    t = [row_pairs(slot, g * SC_LANES + j, off) * sp[j] for j in range(SC_LANES)]
                    quads = [_halves_f32((t[q] + t[q + 1]) + (t[q + 2] + t[q + 3])) for q in range(0, SC_LANES, 4)]
                    lo = (quads[0][0] + quads[1][0]) + (quads[2][0] + quads[3][0])
                    hi = (quads[0][1] + quads[1][1]) + (quads[2][1] + quads[3][1])
                    if first:
                        out_v[p][pl.ds(off, SC_LANES)] = lo
                        out_v[p][pl.ds(dw + off, SC_LANES)] = hi
                    else:
                        plsc.addupdate(out_v[p].at[pl.ds(off, SC_LANES)], lo)
                        plsc.addupdate(out_v[p].at[pl.ds(dw + off, SC_LANES)], hi)

        compute = compute_dot if mode == "dot" else compute_wsum

        for c in loads(base, 0):
            c.start()
        for c in loads(base, 0):
            c.wait()
        for b in range(N_BATCH - 1):
            gather(0, b).start()
        for c in loads(base + 1, 1):
            c.start()

        def pair_body(i2, carry):
            not_last = i2 < n_pairs - 1
            for p in (0, 1):
                t = base + 2 * i2 + p

                @pl.when(i2 > 0)
                def _():
                    store(t - 2, p).wait()

                for b in range(N_BATCH):
                    if b == 0:
                        gather(p, N_BATCH - 1).start()
                    else:
                        def start_next(b=b):
                            if b == 1:
                                for c in loads(t + 1, 1 - p):
                                    c.wait()
                            gather(1 - p, b - 1).start()
                        if p == 0:
                            start_next()
                        else:
                            pl.when(not_last)(start_next)
                    gather(p, b).wait()
                    compute(p, b)
                store(t, p).start()

                @pl.when(not_last)
                def _():
                    for c in loads(t + 2, p):
                        c.start()
            return carry

        lax.fori_loop(0, n_pairs, pair_body, 0)
        store(base + tpw - 2, 0).wait()
        store(base + tpw - 1, 1).wait()

    return k(side, ids, tab)


def _peer(x2d, g, w_q, sub_keys, u_tab, v_tab):
    h, q3 = _peer_q(x2d, g, w_q)
    u_pairs, v_pairs = _pack_bf16_pairs(u_tab), _pack_bf16_pairs(v_tab)
    tc = x2d.shape[0] // PEER_CHUNKS
    outs = []
    for c in range(PEER_CHUNKS):
        ids, gates = _peer_topk(q3, sub_keys, c, PEER_CHUNKS)
        a = _peer_sc_stage(_pack_bf16_pairs(h[c * tc:(c + 1) * tc]), ids, u_pairs, "dot")
        w = _peer_gate(a, gates)
        outs.append(_peer_sc_stage(w, ids, v_pairs, "wsum"))
    return jnp.concatenate(outs, axis=0)


def kernel(x, norm_mix, norm_ffn, norm_final, mix_w_in, s5_lambda_re, s5_lambda_im, s5_log_step, s5_b_re, s5_b_im, s5_c_re, s5_c_im, s5_d, s5_w_glu, conv_dw_w, conv_dw_b, conv_ln_g, conv_ln_b, mix_w_out, attn_w_qkv, attn_b_qkv, attn_sinks, attn_w_o, peer_w_q, peer_sub_keys, peer_u, peer_v):
    bsz, seq, d = x.shape
    tabs = _s5_tables(s5_lambda_re[0], s5_lambda_im[0], s5_log_step[0], s5_b_re[0], s5_b_im[0],
                      s5_c_re[0], s5_c_im[0])

    def trunk(xb):
        x0 = xb.reshape(seq, d)
        z = _mix_in(x0, norm_mix[0], mix_w_in[0])
        x1 = _mixer(z.reshape(1, seq, -1), xb, tabs, s5_d[0], s5_w_glu[0], conv_dw_w[0], conv_dw_b[0],
                    conv_ln_g[0], conv_ln_b[0], mix_w_out[0]).reshape(seq, d)
        p0 = _peer(x1, norm_ffn[0], peer_w_q[0], peer_sub_keys[0], peer_u[0], peer_v[0])
        x2, qkv = _qkv(x1, p0, norm_mix[1], attn_w_qkv[0], attn_b_qkv[0], seq)
        x3 = _attention(qkv.reshape(1, seq, -1), x2.reshape(1, seq, d), attn_sinks[0], attn_w_o[0]).reshape(seq, d)
        p1 = _peer(x3, norm_ffn[1], peer_w_q[1], peer_sub_keys[1], peer_u[1], peer_v[1])
        return _final_norm(x3, p1, norm_final).reshape(1, seq, d)

    return jnp.concatenate([trunk(x[b:b + 1]) for b in range(bsz)], axis=0)
```

```python
import functools

import jax
import jax.numpy as jnp
from jax import lax
from jax.experimental import pallas as pl
from jax.experimental.pallas import tpu as pltpu
from jax.experimental.pallas import tpu_sc as plsc

f32 = jnp.float32
bf16 = jnp.bfloat16
i32 = jnp.int32
u32 = jnp.uint32

D_MODEL = 1024
S5_WIDTH = 512
S5_GROUP = 16
S5_GROUPS = 32
S5_STATE = 64
S5_LANES = S5_GROUPS * S5_STATE
CONV_WIDTH = 512
CONV_K = 31
HEAD_DIM = 64
N_Q_HEADS = 16
N_KV_HEADS = 2
Q_PER_KV = 8
Q_WIDTH = N_Q_HEADS * HEAD_DIM
KV_WIDTH = 2 * N_KV_HEADS * HEAD_DIM
QKV_WIDTH = Q_WIDTH + KV_WIDTH
WINDOW = 128
ROPE_THETA = 500000.0
ROPE_DIM = 16
PEER_HEADS = 8
PEER_N_KEYS = 128
PEER_HALF = 128
PEER_TOPK = 16
N_SLOTS = PEER_HEADS * PEER_TOPK
NORM_EPS = 1e-6

LANES = 128
SUBLANES = 8
TC_VMEM_LIMIT = 48 * 1024 * 1024

SC_CORES = 2
SC_SUBCORES = 16
SC_LANES = 16
SC_WORKERS = SC_CORES * SC_SUBCORES
ROW_BATCH = 32
N_BATCH = N_SLOTS // ROW_BATCH
TRANSPOSE_STRIDE = SC_LANES + 1

ROW_TILE = 512
MIX_TILE = 256
SCAN_TILE = 128
TOPK_TILE = 128
PEER_CHUNKS = 4
CONV_HALO = 32


def _tc_params(*sem):
    return pltpu.CompilerParams(dimension_semantics=sem, vmem_limit_bytes=TC_VMEM_LIMIT)


def _rms(x, g):
    return x * lax.rsqrt(jnp.mean(x * x, axis=-1, keepdims=True) + NORM_EPS) * g


def _sigmoid(x):
    return 1.0 / (1.0 + jnp.exp(-x))


def _gelu(x):
    return 0.5 * x * (1.0 + lax.erf(x * 0.7071067811865476))


def _dot(a, b):
    return jnp.dot(a, b, preferred_element_type=f32)


def _mix_in_kernel(x_ref, g_ref, w_ref, z_ref):
    h = _rms(x_ref[...], g_ref[...])
    z_ref[...] = _dot(h.astype(bf16), w_ref[...])


def _mix_in(x2d, g, w):
    t, d = x2d.shape
    n = w.shape[1]
    return pl.pallas_call(
        _mix_in_kernel,
        grid=(t // ROW_TILE,),
        in_specs=[pl.BlockSpec((ROW_TILE, d), lambda i: (i, 0)),
                  pl.BlockSpec((1, d), lambda i: (0, 0)),
                  pl.BlockSpec((d, n), lambda i: (0, 0))],
        out_specs=pl.BlockSpec((ROW_TILE, n), lambda i: (i, 0)),
        out_shape=jax.ShapeDtypeStruct((t, n), f32),
        compiler_params=_tc_params("parallel"),
        name="mix_in",
    )(x2d, g.reshape(1, d), w.astype(bf16))


def _peer_q_kernel(x_ref, g_ref, w_ref, h_ref, q_ref):
    h = _rms(x_ref[...], g_ref[...])
    h_ref[...] = h
    q = _dot(h.astype(bf16), w_ref[...])
    for c in range(2 * PEER_HEADS):
        q_ref[c] = q[:, c * PEER_HALF:(c + 1) * PEER_HALF]


def _peer_q(x2d, g, w_q):
    t, d = x2d.shape
    nq = 2 * PEER_HEADS
    return pl.pallas_call(
        _peer_q_kernel,
        grid=(t // ROW_TILE,),
        in_specs=[pl.BlockSpec((ROW_TILE, d), lambda i: (i, 0)),
                  pl.BlockSpec((1, d), lambda i: (0, 0)),
                  pl.BlockSpec((d, nq * PEER_HALF), lambda i: (0, 0))],
        out_specs=[pl.BlockSpec((ROW_TILE, d), lambda i: (i, 0)),
                   pl.BlockSpec((nq, ROW_TILE, PEER_HALF), lambda i: (0, i, 0))],
        out_shape=[jax.ShapeDtypeStruct((t, d), f32),
                   jax.ShapeDtypeStruct((nq, t, PEER_HALF), f32)],
        compiler_params=_tc_params("parallel"),
        name="peer_q",
    )(x2d, g.reshape(1, d), w_q.astype(bf16))


def _qkv_kernel(x_ref, r_ref, g_ref, w_ref, b_ref, cos_ref, sin_ref, xo_ref, qkv_ref):
    x = x_ref[...] + r_ref[...]
    xo_ref[...] = x
    h = _rms(x, g_ref[...])
    qkv = _dot(h.astype(bf16), w_ref[...]) + b_ref[...]
    cos = cos_ref[...]
    sin = sin_ref[...]
    lane = lax.broadcasted_iota(i32, cos.shape, 1)
    low = (lane % HEAD_DIM) < (ROPE_DIM // 2)
    n_rot = (Q_WIDTH + N_KV_HEADS * HEAD_DIM) // LANES
    for c in range(QKV_WIDTH // LANES):
        t = qkv[:, c * LANES:(c + 1) * LANES]
        if c < n_rot:
            partner = jnp.where(low, pltpu.roll(t, LANES - ROPE_DIM // 2, axis=1),
                                pltpu.roll(t, ROPE_DIM // 2, axis=1))
            t = t * cos + partner * sin
        if c < Q_WIDTH // LANES:
            t = t * (HEAD_DIM ** -0.5)
        qkv_ref[:, c * LANES:(c + 1) * LANES] = t


def _rope_tables(seq):
    inv = jnp.power(ROPE_THETA, -jnp.arange(0, ROPE_DIM, 2, dtype=f32) / ROPE_DIM)
    ang = jnp.arange(seq, dtype=f32)[:, None] * inv[None, :]
    cos, sin = jnp.cos(ang), jnp.sin(ang)
    pad = HEAD_DIM - ROPE_DIM
    cos_h = jnp.concatenate([cos, cos, jnp.ones((seq, pad), f32)], axis=1)
    sin_h = jnp.concatenate([-sin, sin, jnp.zeros((seq, pad), f32)], axis=1)
    return jnp.tile(cos_h, (1, LANES // HEAD_DIM)), jnp.tile(sin_h, (1, LANES // HEAD_DIM))


def _qkv(x2d, r2d, g, w, b, seq):
    t, d = x2d.shape
    n = w.shape[1]
    cos, sin = _rope_tables(seq)
    blocks_per_seq = seq // ROW_TILE
    return pl.pallas_call(
        _qkv_kernel,
        grid=(t // ROW_TILE,),
        in_specs=[pl.BlockSpec((ROW_TILE, d), lambda i: (i, 0)),
                  pl.BlockSpec((ROW_TILE, d), lambda i: (i, 0)),
                  pl.BlockSpec((1, d), lambda i: (0, 0)),
                  pl.BlockSpec((d, n), lambda i: (0, 0)),
                  pl.BlockSpec((1, n), lambda i: (0, 0)),
                  pl.BlockSpec((ROW_TILE, LANES), lambda i: (i % blocks_per_seq, 0)),
                  pl.BlockSpec((ROW_TILE, LANES), lambda i: (i % blocks_per_seq, 0))],
        out_specs=[pl.BlockSpec((ROW_TILE, d), lambda i: (i, 0)),
                   pl.BlockSpec((ROW_TILE, n), lambda i: (i, 0))],
        out_shape=[jax.ShapeDtypeStruct((t, d), f32), jax.ShapeDtypeStruct((t, n), f32)],
        compiler_params=_tc_params("parallel"),
        name="qkv",
    )(x2d, r2d, g.reshape(1, d), w.astype(bf16), b.reshape(1, n), cos, sin)


def _final_norm_kernel(x_ref, r_ref, g_ref, o_ref):
    o_ref[...] = _rms(x_ref[...] + r_ref[...], g_ref[...])


def _final_norm(x2d, r2d, g):
    t, d = x2d.shape
    spec = pl.BlockSpec((ROW_TILE, d), lambda i: (i, 0))
    return pl.pallas_call(
        _final_norm_kernel,
        grid=(t // ROW_TILE,),
        in_specs=[spec, spec, pl.BlockSpec((1, d), lambda i: (0, 0))],
        out_specs=spec,
        out_shape=jax.ShapeDtypeStruct((t, d), f32),
        compiler_params=_tc_params("parallel"),
        name="final_norm",
    )(x2d, r2d, g.reshape(1, d))


def _shift_rows(v, d, row):
    if d % SUBLANES == 0:
        return jnp.concatenate([jnp.zeros((d, v.shape[1]), v.dtype), v[:v.shape[0] - d]], axis=0)
    return jnp.where(row >= d, pltpu.roll(v, d, axis=0), 0.0)


def _mixer_kernel(z_ref, x_ref, apr_ref, api_ref, bre_ref, bim_ref, cre_ref, cim_ref, dvec_ref, wglu_ref,
                  dww_ref, dwb_ref, lng_ref, lnb_ref, wout_ref, sre_in, sim_in, tail_in,
                  o_ref, sre_out, sim_out, tail_out,
                  carry_re, carry_im, y_s, vbuf):
    tb = z_ref.shape[0]
    n_chunks = S5_LANES // LANES
    steps = [1 << s for s in range(SCAN_TILE.bit_length() - 1)]

    @pl.when(pl.program_id(1) == 0)
    def _():
        carry_re[...] = sre_in[...]
        carry_im[...] = sim_in[...]
        vbuf[0:CONV_HALO, :] = tail_in[...]

    u = z_ref[:, 0:S5_WIDTH]
    ub = u.astype(bf16)
    y_s[...] = u * dvec_ref[...]
    row = lax.broadcasted_iota(i32, (SCAN_TILE, LANES), 0)

    def chunk_body(j, carry):
        bu_re = _dot(ub, bre_ref[j])
        bu_im = _dot(ub, bim_ref[j])
        apr = apr_ref[j]
        api = api_ref[j]
        c_re = carry_re[j]
        c_im = carry_im[j]
        h_re_tiles, h_im_tiles = [], []
        for r in range(tb // SCAN_TILE):
            hr = bu_re[r * SCAN_TILE:(r + 1) * SCAN_TILE]
            hi = bu_im[r * SCAN_TILE:(r + 1) * SCAN_TILE]
            a_re, a_im = apr[0:1], api[0:1]
            first = row == 0
            hr = hr + jnp.where(first, a_re * c_re - a_im * c_im, 0.0)
            hi = hi + jnp.where(first, a_re * c_im + a_im * c_re, 0.0)
            for s, d in enumerate(steps):
                a_re, a_im = apr[s:s + 1], api[s:s + 1]
                sr = _shift_rows(hr, d, row)
                si = _shift_rows(hi, d, row)
                hr, hi = hr + (a_re * sr - a_im * si), hi + (a_re * si + a_im * sr)
            c_re = hr[SCAN_TILE - 1:SCAN_TILE]
            c_im = hi[SCAN_TILE - 1:SCAN_TILE]
            h_re_tiles.append(hr.astype(bf16))
            h_im_tiles.append(hi.astype(bf16))
        carry_re[j] = c_re
        carry_im[j] = c_im
        h_re = jnp.concatenate(h_re_tiles, axis=0)
        h_im = jnp.concatenate(h_im_tiles, axis=0)
        y_s[...] += _dot(h_re, cre_ref[j]) - _dot(h_im, cim_ref[j])
        return carry

    lax.fori_loop(0, n_chunks, chunk_body, 0)
    y = _gelu(y_s[...])
    y_ssm = y * _sigmoid(_dot(y.astype(bf16), wglu_ref[...]))

    v = z_ref[:, S5_WIDTH:S5_WIDTH + CONV_WIDTH] * _sigmoid(z_ref[:, S5_WIDTH + CONV_WIDTH:S5_WIDTH + 2 * CONV_WIDTH])
    vbuf[CONV_HALO:CONV_HALO + tb, :] = v
    acc = jnp.broadcast_to(dwb_ref[...], (tb, CONV_WIDTH))
    for k in range(CONV_K):
        acc = acc + dww_ref[k:k + 1, :] * vbuf[pl.ds(CONV_HALO - (CONV_K - 1) + k, tb), :]
    vbuf[0:CONV_HALO, :] = vbuf[tb:tb + CONV_HALO, :]
    mu = jnp.mean(acc, axis=-1, keepdims=True)
    cen = acc - mu
    var = jnp.mean(cen * cen, axis=-1, keepdims=True)
    yn = cen * lax.rsqrt(var + NORM_EPS) * lng_ref[...] + lnb_ref[...]
    y_conv = yn * _sigmoid(yn)

    o_ref[...] = (x_ref[...] + _dot(y_ssm.astype(bf16), wout_ref[0:S5_WIDTH, :])
                  + _dot(y_conv.astype(bf16), wout_ref[S5_WIDTH:S5_WIDTH + CONV_WIDTH, :]))

    @pl.when(pl.program_id(1) == pl.num_programs(1) - 1)
    def _():
        sre_out[...] = carry_re[...]
        sim_out[...] = carry_im[...]
        tail_out[...] = vbuf[0:CONV_HALO, :]


def _s5_tables(lam_re, lam_im, log_step, b_re, b_im, c_re, c_im):
    n_chunks = S5_LANES // LANES
    g_per_chunk = LANES // S5_STATE
    lam = lax.complex(lam_re, lam_im)
    dt = jnp.exp(log_step)[:, None]
    a_bar = jnp.exp(lam * dt)
    b_bar = ((a_bar - 1.0) / lam)[..., None] * lax.complex(b_re, b_im)
    n_steps = SCAN_TILE.bit_length() - 1
    powers = jnp.stack([jnp.exp(lam * dt * float(1 << s)) for s in range(n_steps)]
                       + [jnp.ones_like(a_bar)] * (SUBLANES - n_steps))
    powers = powers.reshape(SUBLANES, n_chunks, LANES).transpose(1, 0, 2)
    eye = jnp.eye(S5_GROUPS, dtype=f32)

    def in_mat(b):
        m = jnp.einsum('gph,gk->ghkp', b, eye).reshape(S5_WIDTH, S5_LANES)
        return m.reshape(S5_WIDTH, n_chunks, LANES).transpose(1, 0, 2).astype(bf16)

    def out_mat(c):
        m = jnp.einsum('ghp,gk->gpkh', c, eye).reshape(S5_LANES, S5_WIDTH)
        return m.reshape(n_chunks, LANES, S5_WIDTH).astype(bf16)

    del g_per_chunk
    return (jnp.real(powers), jnp.imag(powers), in_mat(jnp.real(b_bar)), in_mat(jnp.imag(b_bar)),
            out_mat(c_re), out_mat(c_im))


def _mixer_state0():
    n_chunks = S5_LANES // LANES
    return (jnp.zeros((n_chunks, 1, LANES), f32), jnp.zeros((n_chunks, 1, LANES), f32),
            jnp.zeros((CONV_HALO, CONV_WIDTH), f32))


def _mixer(z, x, state, part, n_parts, s5_tabs, d_vec, w_glu, dw_w, dw_b, ln_g, ln_b, w_out):
    bsz, seq_all, zw = z.shape
    assert bsz == 1
    seq = seq_all // n_parts
    d = x.shape[-1]
    apr, api, bre, bim, cre, cim = s5_tabs
    n_chunks = S5_LANES // LANES
    tb = MIX_TILE
    first = part * (seq // tb)
    const2 = lambda b, n: (0, 0)
    const3 = lambda b, n: (0, 0, 0)
    state_specs = [pl.BlockSpec((n_chunks, 1, LANES), const3), pl.BlockSpec((n_chunks, 1, LANES), const3),
                   pl.BlockSpec((CONV_HALO, CONV_WIDTH), const2)]
    out, s_re, s_im, tail = pl.pallas_call(
        _mixer_kernel,
        grid=(bsz, seq // tb),
        in_specs=[pl.BlockSpec((None, tb, zw), lambda b, n: (b, first + n, 0)),
                  pl.BlockSpec((None, tb, d), lambda b, n: (b, first + n, 0)),
                  pl.BlockSpec((n_chunks, SUBLANES, LANES), const3),
                  pl.BlockSpec((n_chunks, SUBLANES, LANES), const3),
                  pl.BlockSpec((n_chunks, S5_WIDTH, LANES), const3),
                  pl.BlockSpec((n_chunks, S5_WIDTH, LANES), const3),
                  pl.BlockSpec((n_chunks, LANES, S5_WIDTH), const3),
                  pl.BlockSpec((n_chunks, LANES, S5_WIDTH), const3),
                  pl.BlockSpec((1, S5_WIDTH), const2),
                  pl.BlockSpec((S5_WIDTH, S5_WIDTH), const2),
                  pl.BlockSpec((CONV_K, CONV_WIDTH), const2),
                  pl.BlockSpec((1, CONV_WIDTH), const2),
                  pl.BlockSpec((1, CONV_WIDTH), const2),
                  pl.BlockSpec((1, CONV_WIDTH), const2),
                  pl.BlockSpec((S5_WIDTH + CONV_WIDTH, d), const2)] + state_specs,
        out_specs=[pl.BlockSpec((None, tb, d), lambda b, n: (b, n, 0))] + state_specs,
        out_shape=[jax.ShapeDtypeStruct((bsz, seq, d), f32),
                   jax.ShapeDtypeStruct((n_chunks, 1, LANES), f32),
                   jax.ShapeDtypeStruct((n_chunks, 1, LANES), f32),
                   jax.ShapeDtypeStruct((CONV_HALO, CONV_WIDTH), f32)],
        scratch_shapes=[pltpu.VMEM((n_chunks, 1, LANES), f32),
                        pltpu.VMEM((n_chunks, 1, LANES), f32),
                        pltpu.VMEM((tb, S5_WIDTH), f32),
                        pltpu.VMEM((tb + CONV_HALO, CONV_WIDTH), f32)],
        compiler_params=_tc_params("arbitrary", "arbitrary"),
        name="mixer",
    )(z, x, apr, api, bre, bim, cre, cim, d_vec.reshape(1, S5_WIDTH), w_glu.astype(bf16),
      dw_w, dw_b.reshape(1, -1), ln_g.reshape(1, -1), ln_b.reshape(1, -1), w_out.astype(bf16), *state)
    return out, (s_re, s_im, tail)


def _attn_kernel(q_ref, kvc_ref, kvp_ref, x_ref, sink_ref, wo_ref, o_ref, o_s):
    n = pl.program_id(1)
    qi = lax.broadcasted_iota(i32, (WINDOW, 2 * WINDOW), 0)
    si = lax.broadcasted_iota(i32, (WINDOW, 2 * WINDOW), 1)
    first_key = jnp.where(n > 0, 0, WINDOW)
    valid = (si > qi) & (si <= qi + WINDOW) & (si >= first_key)
    kv = jnp.concatenate([kvp_ref[...], kvc_ref[...]], axis=0).astype(bf16)
    for kh in range(N_KV_HEADS):
        k = kv[:, kh * HEAD_DIM:(kh + 1) * HEAD_DIM]
        v = kv[:, (N_KV_HEADS + kh) * HEAD_DIM:(N_KV_HEADS + kh + 1) * HEAD_DIM]
        for g in range(Q_PER_KV):
            h = kh * Q_PER_KV + g
            q = q_ref[:, h * HEAD_DIM:(h + 1) * HEAD_DIM].astype(bf16)
            s = lax.dot_general(q, k, (((1,), (1,)), ((), ())), preferred_element_type=f32)
            s = jnp.where(valid, s, -jnp.inf)
            sink = sink_ref[0:1, h:h + 1]
            m = jnp.maximum(jnp.max(s, axis=-1, keepdims=True), sink)
            p = jnp.exp(s - m)
            denom = jnp.sum(p, axis=-1, keepdims=True) + jnp.exp(sink - m)
            probs = (p / denom).astype(bf16)
            o_s[:, h * HEAD_DIM:(h + 1) * HEAD_DIM] = _dot(probs, v)
    o_ref[...] = x_ref[...] + _dot(o_s[...].astype(bf16), wo_ref[...])


def _attention(qkv, x, sinks, w_o):
    bsz, seq, _ = qkv.shape
    d = x.shape[-1]
    kv_block = Q_WIDTH // KV_WIDTH
    return pl.pallas_call(
        _attn_kernel,
        grid=(bsz, seq // WINDOW),
        in_specs=[pl.BlockSpec((None, WINDOW, Q_WIDTH), lambda b, n: (b, n, 0)),
                  pl.BlockSpec((None, WINDOW, KV_WIDTH), lambda b, n: (b, n, kv_block)),
                  pl.BlockSpec((None, WINDOW, KV_WIDTH), lambda b, n: (b, jnp.maximum(n - 1, 0), kv_block)),
                  pl.BlockSpec((None, WINDOW, d), lambda b, n: (b, n, 0)),
                  pl.BlockSpec((1, N_Q_HEADS), lambda b, n: (0, 0)),
                  pl.BlockSpec((Q_WIDTH, d), lambda b, n: (0, 0))],
        out_specs=pl.BlockSpec((None, WINDOW, d), lambda b, n: (b, n, 0)),
        out_shape=jax.ShapeDtypeStruct((bsz, seq, d), f32),
        scratch_shapes=[pltpu.VMEM((WINDOW, Q_WIDTH), f32)],
        compiler_params=_tc_params("parallel", "parallel"),
        name="attention",
    )(qkv, qkv, qkv, x, sinks.reshape(1, N_Q_HEADS), w_o.astype(bf16))


def _top16(s, ids=None):
    m_rows = s.shape[0]
    pos = lax.broadcasted_iota(i32, s.shape, 0).astype(f32)
    vals, outs = [], []
    for _ in range(PEER_TOPK):
        mx = jnp.max(s, axis=0, keepdims=True)
        first = jnp.min(jnp.where(s == mx, pos, float(m_rows)), axis=0, keepdims=True)
        hit = pos == first
        vals.append(mx)
        outs.append(first if ids is None else jnp.sum(jnp.where(hit, ids, 0.0), axis=0, keepdims=True))
        s = jnp.where(hit, -jnp.inf, s)
    return jnp.concatenate(vals, axis=0), jnp.concatenate(outs, axis=0)


def _pair_candidates(a0, a1, combine, fill):
    n = a0.shape[1]
    sub = lax.broadcasted_iota(i32, (SUBLANES, n), 0)
    pieces = [combine(a0[0:1], a1)]
    for i in range(1, SUBLANES):
        keep = PEER_TOPK // (i + 1)
        pieces.append(jnp.where(sub < keep, combine(a0[i:i + 1], a1[0:SUBLANES]), fill))
    pieces.append(combine(a0[SUBLANES:PEER_TOPK], a1[0:1]))
    return jnp.concatenate(pieces, axis=0)


def _topk_kernel(q_ref, keys_ref, ids_ref, gates_ref, ids_s, gates_s):
    def head_body(h, carry):
        tops = []
        for c in range(2):
            q = q_ref[2 * h + c].astype(bf16)
            s = lax.dot_general(keys_ref[2 * h + c], q, (((1,), (1,)), ((), ())),
                                preferred_element_type=f32)
            tops.append(_top16(s))
        (s0, i0), (s1, i1) = tops
        cand_s = _pair_candidates(s0, s1, lambda a, b: a + b, -jnp.inf)
        cand_i = _pair_candidates(i0, i1, lambda a, b: a * float(PEER_N_KEYS) + b, 0.0)
        best_s, best_i = _top16(cand_s, cand_i)
        e = jnp.exp(best_s - best_s[0:1])
        off = pl.multiple_of(h * PEER_TOPK, PEER_TOPK)
        gates_s[pl.ds(off, PEER_TOPK), :] = e / jnp.sum(e, axis=0, keepdims=True)
        ids_s[pl.ds(off, PEER_TOPK), :] = best_i.astype(i32)
        return carry

    lax.fori_loop(0, PEER_HEADS, head_body, 0, unroll=4)
    ids_ref[...] = ids_s[...].T
    gates_ref[...] = gates_s[...].T


def _peer_topk(q3, sub_keys, chunk, n_chunks):
    nq, t_all, half = q3.shape
    keys = sub_keys.reshape(nq, PEER_N_KEYS, half).astype(bf16)
    tb = TOPK_TILE
    t = t_all // n_chunks
    first = chunk * (t // tb)
    return pl.pallas_call(
        _topk_kernel,
        grid=(t // tb,),
        in_specs=[pl.BlockSpec((nq, tb, half), lambda i: (0, first + i, 0)),
                  pl.BlockSpec((nq, PEER_N_KEYS, half), lambda i: (0, 0, 0))],
        out_specs=[pl.BlockSpec((tb, N_SLOTS), lambda i: (i, 0)),
                   pl.BlockSpec((tb, N_SLOTS), lambda i: (i, 0))],
        out_shape=[jax.ShapeDtypeStruct((t, N_SLOTS), i32), jax.ShapeDtypeStruct((t, N_SLOTS), f32)],
        scratch_shapes=[pltpu.VMEM((N_SLOTS, tb), i32), pltpu.VMEM((N_SLOTS, tb), f32)],
        compiler_params=_tc_params("parallel"),
        name="peer_topk",
    )(q3, keys)


def _gate_kernel(a_ref, g_ref, w_ref):
    w_ref[...] = _gelu(a_ref[...]) * g_ref[...]


def _peer_gate(a, gates):
    t, n = a.shape
    tm = 2048
    spec = pl.BlockSpec((tm, n), lambda i: (i, 0))
    return pl.pallas_call(
        _gate_kernel, grid=(t // tm,), in_specs=[spec, spec], out_specs=spec,
        out_shape=jax.ShapeDtypeStruct((t, n), f32),
        compiler_params=_tc_params("parallel"), name="peer_gate",
    )(a, gates)


def _sc_params():
    return pltpu.CompilerParams(needs_layout_passes=False)


def _pack_bf16_pairs(a):
    bits = lax.bitcast_convert_type(a.astype(bf16), jnp.uint16).astype(u32)
    w = a.shape[1] // 2
    return bits[:, :w] | (bits[:, w:] << 16)


def _halves_f32(pairs_bf16):
    word = plsc.bitcast(pairs_bf16, u32)
    return plsc.bitcast(word << 16, f32), plsc.bitcast(word & jnp.uint32(0xFFFF0000), f32)


def _peer_sc_stage(side, ids, tab, mode):
    t_total, side_w = side.shape
    dw = tab.shape[1]
    out_w = N_SLOTS if mode == "dot" else 2 * dw
    tpw = t_total // SC_WORKERS
    n_pairs = tpw // 2
    assert tpw * SC_WORKERS == t_total and n_pairs * 2 == tpw and N_BATCH >= 2
    n_grp = ROW_BATCH // SC_LANES
    mesh = plsc.VectorSubcoreMesh(core_axis_name="c", subcore_axis_name="s")

    @functools.partial(
        pl.kernel, mesh=mesh,
        out_type=jax.ShapeDtypeStruct((t_total, out_w), f32),
        scratch_types=[
            pltpu.VMEM((N_SLOTS,), i32), pltpu.VMEM((N_SLOTS,), i32),
            pltpu.VMEM((side_w,), side.dtype), pltpu.VMEM((side_w,), side.dtype),
            pltpu.VMEM((out_w,), f32), pltpu.VMEM((out_w,), f32),
            pltpu.VMEM((N_BATCH, ROW_BATCH, dw), u32),
            pltpu.VMEM((SC_LANES * TRANSPOSE_STRIDE,), f32),
            pltpu.SemaphoreType.DMA((N_BATCH,)), pltpu.SemaphoreType.DMA((2,)),
            pltpu.SemaphoreType.DMA((2,)), pltpu.SemaphoreType.DMA((2,)),
        ],
        compiler_params=_sc_params())
    def k(side_hbm, ids_hbm, tab_hbm, out_hbm, idx0, idx1, side0, side1, out0, out1, rows_v, tmp_v,
          sem_g, sem_i, sem_s, sem_o):
        idx_v, side_v, out_v = (idx0, idx1), (side0, side1), (out0, out1)
        wid = lax.axis_index("s") * SC_CORES + lax.axis_index("c")
        base = wid * tpw
        iota = lax.iota(i32, SC_LANES)

        def row_pairs(slot, r, off):
            return plsc.bitcast(rows_v[slot, r, pl.ds(off, SC_LANES)], bf16)

        def gather(p, b):
            return pltpu.make_async_copy(
                tab_hbm.at[idx_v[p].at[pl.ds(b * ROW_BATCH, ROW_BATCH)]], rows_v.at[b], sem_g.at[b])

        def loads(t, p):
            return (pltpu.make_async_copy(ids_hbm.at[t], idx_v[p], sem_i.at[p]),
                    pltpu.make_async_copy(side_hbm.at[t], side_v[p], sem_s.at[p]))

        def store(t, p):
            return pltpu.make_async_copy(out_v[p], out_hbm.at[t], sem_o.at[p])

        def compute_dot(p, b):
            slot = b
            for g in range(n_grp):
                def cbody(c, accs):
                    off = pl.multiple_of(c * 4 * SC_LANES, 4 * SC_LANES)
                    xs = [plsc.bitcast(side_v[p][pl.ds(off + q * SC_LANES, SC_LANES)], bf16) for q in range(4)]
                    out = []
                    for j in range(SC_LANES):
                        t = [row_pairs(slot, g * SC_LANES + j, off + q * SC_LANES) * xs[q] for q in range(4)]
                        lo, hi = _halves_f32((t[0] + t[1]) + (t[2] + t[3]))
                        out.append(accs[j] + (lo + hi))
                    return tuple(out)
                accs = lax.fori_loop(0, dw // (4 * SC_LANES), cbody,
                                     tuple(jnp.zeros((SC_LANES,), f32) for _ in range(SC_LANES)))
                for j in range(SC_LANES):
                    plsc.store_scatter(tmp_v, [iota * TRANSPOSE_STRIDE + j], accs[j])
                rows = [plsc.load_gather(tmp_v, [iota + l * TRANSPOSE_STRIDE]) for l in range(SC_LANES)]
                while len(rows) > 1:
                    rows = [rows[i] + rows[i + 1] for i in range(0, len(rows), 2)]
                out_v[p][pl.ds(b * ROW_BATCH + g * SC_LANES, SC_LANES)] = rows[0]

        def compute_wsum(p, b):
            slot = b
            for g in range(n_grp):
                sp = []
                for j in range(SC_LANES):
                    w = plsc.load_gather(side_v[p], [jnp.full((SC_LANES,), b * ROW_BATCH + g * SC_LANES + j, i32)])
                    sp.append(plsc.pack(w, w, format=plsc.PackFormat.INTERLEAVED))
                first = b == 0 and g == 0

                @plsc.parallel_loop(0, dw, step=SC_LANES)
                def _(c):
                    off = pl.multiple_of(c, SC_LANES)
                    t = [row_pairs(slot, g * SC_LANES + j, off) * sp[j] for j in range(SC_LANES)]
                    quads = [_halves_f32((t[q] + t[q + 1]) + (t[q + 2] + t[q + 3])) for q in range(0, SC_LANES, 4)]
                    lo = (quads[0][0] + quads[1][0]) + (quads[2][0] + quads[3][0])
                    hi = (quads[0][1] + quads[1][1]) + (quads[2][1] + quads[3][1])
                    if first:
                        out_v[p][pl.ds(off, SC_LANES)] = lo
                        out_v[p][pl.ds(dw + off, SC_LANES)] = hi
                    else:
                        plsc.addupdate(out_v[p].at[pl.ds(off, SC_LANES)], lo)
                        plsc.addupdate(out_v[p].at[pl.ds(dw + off, SC_LANES)], hi)

        compute = compute_dot if mode == "dot" else compute_wsum

        for c in loads(base, 0):
            c.start()
        for c in loads(base, 0):
            c.wait()
        for b in range(N_BATCH - 1):
            gather(0, b).start()
        for c in loads(base + 1, 1):
            c.start()

        def pair_body(i2, carry):
            not_last = i2 < n_pairs - 1
            for p in (0, 1):
                t = base + 2 * i2 + p

                @pl.when(i2 > 0)
                def _():
                    store(t - 2, p).wait()

                for b in range(N_BATCH):
                    if b == 0:
                        gather(p, N_BATCH - 1).start()
                    else:
                        def start_next(b=b):
                            if b == 1:
                                for c in loads(t + 1, 1 - p):
                                    c.wait()
                            gather(1 - p, b - 1).start()
                        if p == 0:
                            start_next()
                        else:
                            pl.when(not_last)(start_next)
                    gather(p, b).wait()
                    compute(p, b)
                store(t, p).start()

                @pl.when(not_last)
                def _():
                    for c in loads(t + 2, p):
                        c.start()
            return carry

        lax.fori_loop(0, n_pairs, pair_body, 0)
        store(base + tpw - 2, 0).wait()
        store(base + tpw - 1, 1).wait()

    return k(side, ids, tab)


def _peer(x2d, g, w_q, sub_keys, u_pairs, v_pairs):
    h, q3 = _peer_q(x2d, g, w_q)
    ids, gates = _peer_topk(q3, sub_keys, 0, 1)
    a = _peer_sc_stage(_pack_bf16_pairs(h), ids, u_pairs, "dot")
    w = _peer_gate(a, gates)
    return _peer_sc_stage(w, ids, v_pairs, "wsum")


def kernel(x, norm_mix, norm_ffn, norm_final, mix_w_in, s5_lambda_re, s5_lambda_im, s5_log_step, s5_b_re, s5_b_im, s5_c_re, s5_c_im, s5_d, s5_w_glu, conv_dw_w, conv_dw_b, conv_ln_g, conv_ln_b, mix_w_out, attn_w_qkv, attn_b_qkv, attn_sinks, attn_w_o, peer_w_q, peer_sub_keys, peer_u, peer_v):
    bsz, seq, d = x.shape
    tabs = _s5_tables(s5_lambda_re[0], s5_lambda_im[0], s5_log_step[0], s5_b_re[0], s5_b_im[0],
                      s5_c_re[0], s5_c_im[0])

    pairs = [(_pack_bf16_pairs(peer_u[l]), _pack_bf16_pairs(peer_v[l])) for l in range(2)]
    tc = seq // PEER_CHUNKS

    def peer(xc, l):
        return _peer(xc, norm_ffn[l], peer_w_q[l], peer_sub_keys[l], *pairs[l])

    def trunk(xb):
        z = _mix_in(xb.reshape(seq, d), norm_mix[0], mix_w_in[0]).reshape(1, seq, -1)
        state = _mixer_state0()
        x1, p0 = [], []
        for c in range(PEER_CHUNKS):
            xc, state = _mixer(z, xb, state, c, PEER_CHUNKS, tabs, s5_d[0], s5_w_glu[0], conv_dw_w[0],
                               conv_dw_b[0], conv_ln_g[0], conv_ln_b[0], mix_w_out[0])
            x1.append(xc.reshape(tc, d))
            p0.append(peer(x1[-1], 0))
        x2, qkv = _qkv(jnp.concatenate(x1, axis=0), jnp.concatenate(p0, axis=0), norm_mix[1],
                       attn_w_qkv[0], attn_b_qkv[0], seq)
        x3 = _attention(qkv.reshape(1, seq, -1), x2.reshape(1, seq, d), attn_sinks[0], attn_w_o[0]).reshape(seq, d)
        p1 = jnp.concatenate([peer(x3[c * tc:(c + 1) * tc], 1) for c in range(PEER_CHUNKS)], axis=0)
        return _final_norm(x3, p1, norm_final).reshape(1, seq, d)

    return jnp.concatenate([trunk(x[b:b + 1]) for b in range(bsz)], axis=0)
```

```python
import functools

import jax
import jax.numpy as jnp
from jax import lax
from jax.experimental import pallas as pl
from jax.experimental.pallas import tpu as pltpu
from jax.experimental.pallas import tpu_sc as plsc

f32 = jnp.float32
bf16 = jnp.bfloat16
i32 = jnp.int32
u32 = jnp.uint32

D_MODEL = 1024
S5_WIDTH = 512
S5_GROUP = 16
S5_GROUPS = 32
S5_STATE = 64
S5_LANES = S5_GROUPS * S5_STATE
CONV_WIDTH = 512
CONV_K = 31
HEAD_DIM = 64
N_Q_HEADS = 16
N_KV_HEADS = 2
Q_PER_KV = 8
Q_WIDTH = N_Q_HEADS * HEAD_DIM
KV_WIDTH = 2 * N_KV_HEADS * HEAD_DIM
QKV_WIDTH = Q_WIDTH + KV_WIDTH
WINDOW = 128
ROPE_THETA = 500000.0
ROPE_DIM = 16
PEER_HEADS = 8
PEER_N_KEYS = 128
PEER_HALF = 128
PEER_TOPK = 16
N_SLOTS = PEER_HEADS * PEER_TOPK
NORM_EPS = 1e-6

LANES = 128
SUBLANES = 8
TC_VMEM_LIMIT = 48 * 1024 * 1024

SC_CORES = 2
SC_SUBCORES = 16
SC_LANES = 16
SC_WORKERS = SC_CORES * SC_SUBCORES
ROW_BATCH = 32
N_BATCH = N_SLOTS // ROW_BATCH

ROW_TILE = 512
MIX_TILE = 256
SCAN_TILE = 128
TOPK_TILE = 128
PEER_CHUNKS = 4
CONV_HALO = 32


def _tc_params(*sem):
    return pltpu.CompilerParams(dimension_semantics=sem, vmem_limit_bytes=TC_VMEM_LIMIT)


def _rms(x, g):
    return x * lax.rsqrt(jnp.mean(x * x, axis=-1, keepdims=True) + NORM_EPS) * g


def _sigmoid(x):
    return 1.0 / (1.0 + jnp.exp(-x))


def _gelu(x):
    return 0.5 * x * (1.0 + lax.erf(x * 0.7071067811865476))


def _dot(a, b):
    return jnp.dot(a, b, preferred_element_type=f32)


def _mix_in_kernel(x_ref, g_ref, w_ref, z_ref):
    h = _rms(x_ref[...], g_ref[...])
    z_ref[...] = _dot(h.astype(bf16), w_ref[...])


def _mix_in(x2d, g, w):
    t, d = x2d.shape
    n = w.shape[1]
    return pl.pallas_call(
        _mix_in_kernel,
        grid=(t // ROW_TILE,),
        in_specs=[pl.BlockSpec((ROW_TILE, d), lambda i: (i, 0)),
                  pl.BlockSpec((1, d), lambda i: (0, 0)),
                  pl.BlockSpec((d, n), lambda i: (0, 0))],
        out_specs=pl.BlockSpec((ROW_TILE, n), lambda i: (i, 0)),
        out_shape=jax.ShapeDtypeStruct((t, n), f32),
        compiler_params=_tc_params("parallel"),
        name="mix_in",
    )(x2d, g.reshape(1, d), w.astype(bf16))


def _peer_q_kernel(x_ref, g_ref, w_ref, h_ref, q_ref):
    h = _rms(x_ref[...], g_ref[...])
    h_ref[...] = h
    q = _dot(h.astype(bf16), w_ref[...])
    for c in range(2 * PEER_HEADS):
        q_ref[c] = q[:, c * PEER_HALF:(c + 1) * PEER_HALF]


def _peer_q(x2d, g, w_q):
    t, d = x2d.shape
    nq = 2 * PEER_HEADS
    return pl.pallas_call(
        _peer_q_kernel,
        grid=(t // ROW_TILE,),
        in_specs=[pl.BlockSpec((ROW_TILE, d), lambda i: (i, 0)),
                  pl.BlockSpec((1, d), lambda i: (0, 0)),
                  pl.BlockSpec((d, nq * PEER_HALF), lambda i: (0, 0))],
        out_specs=[pl.BlockSpec((ROW_TILE, d), lambda i: (i, 0)),
                   pl.BlockSpec((nq, ROW_TILE, PEER_HALF), lambda i: (0, i, 0))],
        out_shape=[jax.ShapeDtypeStruct((t, d), f32),
                   jax.ShapeDtypeStruct((nq, t, PEER_HALF), f32)],
        compiler_params=_tc_params("parallel"),
        name="peer_q",
    )(x2d, g.reshape(1, d), w_q.astype(bf16))


def _qkv_kernel(x_ref, r_ref, g_ref, w_ref, b_ref, cos_ref, sin_ref, xo_ref, qkv_ref):
    x = x_ref[...] + r_ref[...]
    xo_ref[...] = x
    h = _rms(x, g_ref[...])
    qkv = _dot(h.astype(bf16), w_ref[...]) + b_ref[...]
    cos = cos_ref[...]
    sin = sin_ref[...]
    lane = lax.broadcasted_iota(i32, cos.shape, 1)
    low = (lane % HEAD_DIM) < (ROPE_DIM // 2)
    n_rot = (Q_WIDTH + N_KV_HEADS * HEAD_DIM) // LANES
    for c in range(QKV_WIDTH // LANES):
        t = qkv[:, c * LANES:(c + 1) * LANES]
        if c < n_rot:
            partner = jnp.where(low, pltpu.roll(t, LANES - ROPE_DIM // 2, axis=1),
                                pltpu.roll(t, ROPE_DIM // 2, axis=1))
            t = t * cos + partner * sin
        if c < Q_WIDTH // LANES:
            t = t * (HEAD_DIM ** -0.5)
        qkv_ref[:, c * LANES:(c + 1) * LANES] = t


def _rope_tables(seq):
    inv = jnp.power(ROPE_THETA, -jnp.arange(0, ROPE_DIM, 2, dtype=f32) / ROPE_DIM)
    ang = jnp.arange(seq, dtype=f32)[:, None] * inv[None, :]
    cos, sin = jnp.cos(ang), jnp.sin(ang)
    pad = HEAD_DIM - ROPE_DIM
    cos_h = jnp.concatenate([cos, cos, jnp.ones((seq, pad), f32)], axis=1)
    sin_h = jnp.concatenate([-sin, sin, jnp.zeros((seq, pad), f32)], axis=1)
    return jnp.tile(cos_h, (1, LANES // HEAD_DIM)), jnp.tile(sin_h, (1, LANES // HEAD_DIM))


def _qkv(x2d, r2d, g, w, b, seq):
    t, d = x2d.shape
    n = w.shape[1]
    cos, sin = _rope_tables(seq)
    blocks_per_seq = seq // ROW_TILE
    return pl.pallas_call(
        _qkv_kernel,
        grid=(t // ROW_TILE,),
        in_specs=[pl.BlockSpec((ROW_TILE, d), lambda i: (i, 0)),
                  pl.BlockSpec((ROW_TILE, d), lambda i: (i, 0)),
                  pl.BlockSpec((1, d), lambda i: (0, 0)),
                  pl.BlockSpec((d, n), lambda i: (0, 0)),
                  pl.BlockSpec((1, n), lambda i: (0, 0)),
                  pl.BlockSpec((ROW_TILE, LANES), lambda i: (i % blocks_per_seq, 0)),
                  pl.BlockSpec((ROW_TILE, LANES), lambda i: (i % blocks_per_seq, 0))],
        out_specs=[pl.BlockSpec((ROW_TILE, d), lambda i: (i, 0)),
                   pl.BlockSpec((ROW_TILE, n), lambda i: (i, 0))],
        out_shape=[jax.ShapeDtypeStruct((t, d), f32), jax.ShapeDtypeStruct((t, n), f32)],
        compiler_params=_tc_params("parallel"),
        name="qkv",
    )(x2d, r2d, g.reshape(1, d), w.astype(bf16), b.reshape(1, n), cos, sin)


def _final_norm_kernel(x_ref, r_ref, g_ref, o_ref):
    o_ref[...] = _rms(x_ref[...] + r_ref[...], g_ref[...])


def _final_norm(x2d, r2d, g):
    t, d = x2d.shape
    spec = pl.BlockSpec((ROW_TILE, d), lambda i: (i, 0))
    return pl.pallas_call(
        _final_norm_kernel,
        grid=(t // ROW_TILE,),
        in_specs=[spec, spec, pl.BlockSpec((1, d), lambda i: (0, 0))],
        out_specs=spec,
        out_shape=jax.ShapeDtypeStruct((t, d), f32),
        compiler_params=_tc_params("parallel"),
        name="final_norm",
    )(x2d, r2d, g.reshape(1, d))


def _shift_rows(v, d, row):
    if d % SUBLANES == 0:
        return jnp.concatenate([jnp.zeros((d, v.shape[1]), v.dtype), v[:v.shape[0] - d]], axis=0)
    return jnp.where(row >= d, pltpu.roll(v, d, axis=0), 0.0)


def _mixer_kernel(z_ref, x_ref, apr_ref, api_ref, bre_ref, bim_ref, cre_ref, cim_ref, dvec_ref, wglu_ref,
                  dww_ref, dwb_ref, lng_ref, lnb_ref, wout_ref, sre_in, sim_in, tail_in,
                  o_ref, sre_out, sim_out, tail_out,
                  carry_re, carry_im, y_s, vbuf):
    tb = z_ref.shape[0]
    n_chunks = S5_LANES // LANES
    steps = [1 << s for s in range(SCAN_TILE.bit_length() - 1)]

    @pl.when(pl.program_id(1) == 0)
    def _():
        carry_re[...] = sre_in[...]
        carry_im[...] = sim_in[...]
        vbuf[0:CONV_HALO, :] = tail_in[...]

    u = z_ref[:, 0:S5_WIDTH]
    ub = u.astype(bf16)
    y_s[...] = u * dvec_ref[...]
    row = lax.broadcasted_iota(i32, (SCAN_TILE, LANES), 0)

    def chunk_body(j, carry):
        bu_re = _dot(ub, bre_ref[j])
        bu_im = _dot(ub, bim_ref[j])
        apr = apr_ref[j]
        api = api_ref[j]
        c_re = carry_re[j]
        c_im = carry_im[j]
        h_re_tiles, h_im_tiles = [], []
        for r in range(tb // SCAN_TILE):
            hr = bu_re[r * SCAN_TILE:(r + 1) * SCAN_TILE]
            hi = bu_im[r * SCAN_TILE:(r + 1) * SCAN_TILE]
            a_re, a_im = apr[0:1], api[0:1]
            first = row == 0
            hr = hr + jnp.where(first, a_re * c_re - a_im * c_im, 0.0)
            hi = hi + jnp.where(first, a_re * c_im + a_im * c_re, 0.0)
            for s, d in enumerate(steps):
                a_re, a_im = apr[s:s + 1], api[s:s + 1]
                sr = _shift_rows(hr, d, row)
                si = _shift_rows(hi, d, row)
                hr, hi = hr + (a_re * sr - a_im * si), hi + (a_re * si + a_im * sr)
            c_re = hr[SCAN_TILE - 1:SCAN_TILE]
            c_im = hi[SCAN_TILE - 1:SCAN_TILE]
            h_re_tiles.append(hr.astype(bf16))
            h_im_tiles.append(hi.astype(bf16))
        carry_re[j] = c_re
        carry_im[j] = c_im
        h_re = jnp.concatenate(h_re_tiles, axis=0)
        h_im = jnp.concatenate(h_im_tiles, axis=0)
        y_s[...] += _dot(h_re, cre_ref[j]) - _dot(h_im, cim_ref[j])
        return carry

    lax.fori_loop(0, n_chunks, chunk_body, 0)
    y = _gelu(y_s[...])
    y_ssm = y * _sigmoid(_dot(y.astype(bf16), wglu_ref[...]))

    v = z_ref[:, S5_WIDTH:S5_WIDTH + CONV_WIDTH] * _sigmoid(z_ref[:, S5_WIDTH + CONV_WIDTH:S5_WIDTH + 2 * CONV_WIDTH])
    vbuf[CONV_HALO:CONV_HALO + tb, :] = v
    acc = jnp.broadcast_to(dwb_ref[...], (tb, CONV_WIDTH))
    for k in range(CONV_K):
        acc = acc + dww_ref[k:k + 1, :] * vbuf[pl.ds(CONV_HALO - (CONV_K - 1) + k, tb), :]
    vbuf[0:CONV_HALO, :] = vbuf[tb:tb + CONV_HALO, :]
    mu = jnp.mean(acc, axis=-1, keepdims=True)
    cen = acc - mu
    var = jnp.mean(cen * cen, axis=-1, keepdims=True)
    yn = cen * lax.rsqrt(var + NORM_EPS) * lng_ref[...] + lnb_ref[...]
    y_conv = yn * _sigmoid(yn)

    o_ref[...] = (x_ref[...] + _dot(y_ssm.astype(bf16), wout_ref[0:S5_WIDTH, :])
                  + _dot(y_conv.astype(bf16), wout_ref[S5_WIDTH:S5_WIDTH + CONV_WIDTH, :]))

    @pl.when(pl.program_id(1) == pl.num_programs(1) - 1)
    def _():
        sre_out[...] = carry_re[...]
        sim_out[...] = carry_im[...]
        tail_out[...] = vbuf[0:CONV_HALO, :]


def _s5_tables(lam_re, lam_im, log_step, b_re, b_im, c_re, c_im):
    n_chunks = S5_LANES // LANES
    g_per_chunk = LANES // S5_STATE
    lam = lax.complex(lam_re, lam_im)
    dt = jnp.exp(log_step)[:, None]
    a_bar = jnp.exp(lam * dt)
    b_bar = ((a_bar - 1.0) / lam)[..., None] * lax.complex(b_re, b_im)
    n_steps = SCAN_TILE.bit_length() - 1
    powers = jnp.stack([jnp.exp(lam * dt * float(1 << s)) for s in range(n_steps)]
                       + [jnp.ones_like(a_bar)] * (SUBLANES - n_steps))
    powers = powers.reshape(SUBLANES, n_chunks, LANES).transpose(1, 0, 2)
    eye = jnp.eye(S5_GROUPS, dtype=f32)

    def in_mat(b):
        m = jnp.einsum('gph,gk->ghkp', b, eye).reshape(S5_WIDTH, S5_LANES)
        return m.reshape(S5_WIDTH, n_chunks, LANES).transpose(1, 0, 2).astype(bf16)

    def out_mat(c):
        m = jnp.einsum('ghp,gk->gpkh', c, eye).reshape(S5_LANES, S5_WIDTH)
        return m.reshape(n_chunks, LANES, S5_WIDTH).astype(bf16)

    del g_per_chunk
    return (jnp.real(powers), jnp.imag(powers), in_mat(jnp.real(b_bar)), in_mat(jnp.imag(b_bar)),
            out_mat(c_re), out_mat(c_im))


def _mixer_state0():
    n_chunks = S5_LANES // LANES
    return (jnp.zeros((n_chunks, 1, LANES), f32), jnp.zeros((n_chunks, 1, LANES), f32),
            jnp.zeros((CONV_HALO, CONV_WIDTH), f32))


def _mixer(z, x, state, part, n_parts, s5_tabs, d_vec, w_glu, dw_w, dw_b, ln_g, ln_b, w_out):
    bsz, seq_all, zw = z.shape
    assert bsz == 1
    seq = seq_all // n_parts
    d = x.shape[-1]
    apr, api, bre, bim, cre, cim = s5_tabs
    n_chunks = S5_LANES // LANES
    tb = MIX_TILE
    first = part * (seq // tb)
    const2 = lambda b, n: (0, 0)
    const3 = lambda b, n: (0, 0, 0)
    state_specs = [pl.BlockSpec((n_chunks, 1, LANES), const3), pl.BlockSpec((n_chunks, 1, LANES), const3),
                   pl.BlockSpec((CONV_HALO, CONV_WIDTH), const2)]
    out, s_re, s_im, tail = pl.pallas_call(
        _mixer_kernel,
        grid=(bsz, seq // tb),
        in_specs=[pl.BlockSpec((None, tb, zw), lambda b, n: (b, first + n, 0)),
                  pl.BlockSpec((None, tb, d), lambda b, n: (b, first + n, 0)),
                  pl.BlockSpec((n_chunks, SUBLANES, LANES), const3),
                  pl.BlockSpec((n_chunks, SUBLANES, LANES), const3),
                  pl.BlockSpec((n_chunks, S5_WIDTH, LANES), const3),
                  pl.BlockSpec((n_chunks, S5_WIDTH, LANES), const3),
                  pl.BlockSpec((n_chunks, LANES, S5_WIDTH), const3),
                  pl.BlockSpec((n_chunks, LANES, S5_WIDTH), const3),
                  pl.BlockSpec((1, S5_WIDTH), const2),
                  pl.BlockSpec((S5_WIDTH, S5_WIDTH), const2),
                  pl.BlockSpec((CONV_K, CONV_WIDTH), const2),
                  pl.BlockSpec((1, CONV_WIDTH), const2),
                  pl.BlockSpec((1, CONV_WIDTH), const2),
                  pl.BlockSpec((1, CONV_WIDTH), const2),
                  pl.BlockSpec((S5_WIDTH + CONV_WIDTH, d), const2)] + state_specs,
        out_specs=[pl.BlockSpec((None, tb, d), lambda b, n: (b, n, 0))] + state_specs,
        out_shape=[jax.ShapeDtypeStruct((bsz, seq, d), f32),
                   jax.ShapeDtypeStruct((n_chunks, 1, LANES), f32),
                   jax.ShapeDtypeStruct((n_chunks, 1, LANES), f32),
                   jax.ShapeDtypeStruct((CONV_HALO, CONV_WIDTH), f32)],
        scratch_shapes=[pltpu.VMEM((n_chunks, 1, LANES), f32),
                        pltpu.VMEM((n_chunks, 1, LANES), f32),
                        pltpu.VMEM((tb, S5_WIDTH), f32),
                        pltpu.VMEM((tb + CONV_HALO, CONV_WIDTH), f32)],
        compiler_params=_tc_params("arbitrary", "arbitrary"),
        name="mixer",
    )(z, x, apr, api, bre, bim, cre, cim, d_vec.reshape(1, S5_WIDTH), w_glu.astype(bf16),
      dw_w, dw_b.reshape(1, -1), ln_g.reshape(1, -1), ln_b.reshape(1, -1), w_out.astype(bf16), *state)
    return out, (s_re, s_im, tail)


def _attn_kernel(q_ref, kvc_ref, kvp_ref, x_ref, sink_ref, wo_ref, o_ref, o_s):
    n = pl.program_id(1)
    qi = lax.broadcasted_iota(i32, (WINDOW, 2 * WINDOW), 0)
    si = lax.broadcasted_iota(i32, (WINDOW, 2 * WINDOW), 1)
    first_key = jnp.where(n > 0, 0, WINDOW)
    valid = (si > qi) & (si <= qi + WINDOW) & (si >= first_key)
    kv = jnp.concatenate([kvp_ref[...], kvc_ref[...]], axis=0).astype(bf16)
    for kh in range(N_KV_HEADS):
        k = kv[:, kh * HEAD_DIM:(kh + 1) * HEAD_DIM]
        v = kv[:, (N_KV_HEADS + kh) * HEAD_DIM:(N_KV_HEADS + kh + 1) * HEAD_DIM]
        for g in range(Q_PER_KV):
            h = kh * Q_PER_KV + g
            q = q_ref[:, h * HEAD_DIM:(h + 1) * HEAD_DIM].astype(bf16)
            s = lax.dot_general(q, k, (((1,), (1,)), ((), ())), preferred_element_type=f32)
            s = jnp.where(valid, s, -jnp.inf)
            sink = sink_ref[0:1, h:h + 1]
            m = jnp.maximum(jnp.max(s, axis=-1, keepdims=True), sink)
            p = jnp.exp(s - m)
            denom = jnp.sum(p, axis=-1, keepdims=True) + jnp.exp(sink - m)
            probs = (p / denom).astype(bf16)
            o_s[:, h * HEAD_DIM:(h + 1) * HEAD_DIM] = _dot(probs, v)
    o_ref[...] = x_ref[...] + _dot(o_s[...].astype(bf16), wo_ref[...])


def _attention(qkv, x, sinks, w_o):
    bsz, seq, _ = qkv.shape
    d = x.shape[-1]
    kv_block = Q_WIDTH // KV_WIDTH
    return pl.pallas_call(
        _attn_kernel,
        grid=(bsz, seq // WINDOW),
        in_specs=[pl.BlockSpec((None, WINDOW, Q_WIDTH), lambda b, n: (b, n, 0)),
                  pl.BlockSpec((None, WINDOW, KV_WIDTH), lambda b, n: (b, n, kv_block)),
                  pl.BlockSpec((None, WINDOW, KV_WIDTH), lambda b, n: (b, jnp.maximum(n - 1, 0), kv_block)),
                  pl.BlockSpec((None, WINDOW, d), lambda b, n: (b, n, 0)),
                  pl.BlockSpec((1, N_Q_HEADS), lambda b, n: (0, 0)),
                  pl.BlockSpec((Q_WIDTH, d), lambda b, n: (0, 0))],
        out_specs=pl.BlockSpec((None, WINDOW, d), lambda b, n: (b, n, 0)),
        out_shape=jax.ShapeDtypeStruct((bsz, seq, d), f32),
        scratch_shapes=[pltpu.VMEM((WINDOW, Q_WIDTH), f32)],
        compiler_params=_tc_params("parallel", "parallel"),
        name="attention",
    )(qkv, qkv, qkv, x, sinks.reshape(1, N_Q_HEADS), w_o.astype(bf16))


def _top16(s, ids=None):
    m_rows = s.shape[0]
    pos = lax.broadcasted_iota(i32, s.shape, 0).astype(f32)
    vals, outs = [], []
    for _ in range(PEER_TOPK):
        mx = jnp.max(s, axis=0, keepdims=True)
        first = jnp.min(jnp.where(s == mx, pos, float(m_rows)), axis=0, keepdims=True)
        hit = pos == first
        vals.append(mx)
        outs.append(first if ids is None else jnp.sum(jnp.where(hit, ids, 0.0), axis=0, keepdims=True))
        s = jnp.where(hit, -jnp.inf, s)
    return jnp.concatenate(vals, axis=0), jnp.concatenate(outs, axis=0)


def _pair_candidates(a0, a1, combine, fill):
    n = a0.shape[1]
    sub = lax.broadcasted_iota(i32, (SUBLANES, n), 0)
    pieces = [combine(a0[0:1], a1)]
    for i in range(1, SUBLANES):
        keep = PEER_TOPK // (i + 1)
        pieces.append(jnp.where(sub < keep, combine(a0[i:i + 1], a1[0:SUBLANES]), fill))
    pieces.append(combine(a0[SUBLANES:PEER_TOPK], a1[0:1]))
    return jnp.concatenate(pieces, axis=0)


def _topk_kernel(q_ref, keys_ref, ids_ref, gates_ref, ids_s, gates_s):
    def head_body(h, carry):
        tops = []
        for c in range(2):
            q = q_ref[2 * h + c].astype(bf16)
            s = lax.dot_general(keys_ref[2 * h + c], q, (((1,), (1,)), ((), ())),
                                preferred_element_type=f32)
            tops.append(_top16(s))
        (s0, i0), (s1, i1) = tops
        cand_s = _pair_candidates(s0, s1, lambda a, b: a + b, -jnp.inf)
        cand_i = _pair_candidates(i0, i1, lambda a, b: a * float(PEER_N_KEYS) + b, 0.0)
        best_s, best_i = _top16(cand_s, cand_i)
        e = jnp.exp(best_s - best_s[0:1])
        off = pl.multiple_of(h * PEER_TOPK, PEER_TOPK)
        gates_s[pl.ds(off, PEER_TOPK), :] = e / jnp.sum(e, axis=0, keepdims=True)
        ids_s[pl.ds(off, PEER_TOPK), :] = best_i.astype(i32)
        return carry

    lax.fori_loop(0, PEER_HEADS, head_body, 0, unroll=4)
    ids_ref[...] = ids_s[...].T
    gates_ref[...] = gates_s[...].T


def _peer_topk(q3, sub_keys, chunk, n_chunks):
    nq, t_all, half = q3.shape
    keys = sub_keys.reshape(nq, PEER_N_KEYS, half).astype(bf16)
    tb = TOPK_TILE
    t = t_all // n_chunks
    first = chunk * (t // tb)
    return pl.pallas_call(
        _topk_kernel,
        grid=(t // tb,),
        in_specs=[pl.BlockSpec((nq, tb, half), lambda i: (0, first + i, 0)),
                  pl.BlockSpec((nq, PEER_N_KEYS, half), lambda i: (0, 0, 0))],
        out_specs=[pl.BlockSpec((tb, N_SLOTS), lambda i: (i, 0)),
                   pl.BlockSpec((tb, N_SLOTS), lambda i: (i, 0))],
        out_shape=[jax.ShapeDtypeStruct((t, N_SLOTS), i32), jax.ShapeDtypeStruct((t, N_SLOTS), f32)],
        scratch_shapes=[pltpu.VMEM((N_SLOTS, tb), i32), pltpu.VMEM((N_SLOTS, tb), f32)],
        compiler_params=_tc_params("parallel"),
        name="peer_topk",
    )(q3, keys)


def _gate_kernel(a_ref, g_ref, w_ref):
    w_ref[...] = _gelu(a_ref[...]) * g_ref[...]


def _peer_gate(a, gates):
    t, n = a.shape
    tm = 2048
    spec = pl.BlockSpec((tm, n), lambda i: (i, 0))
    return pl.pallas_call(
        _gate_kernel, grid=(t // tm,), in_specs=[spec, spec], out_specs=spec,
        out_shape=jax.ShapeDtypeStruct((t, n), f32),
        compiler_params=_tc_params("parallel"), name="peer_gate",
    )(a, gates)


def _sc_params():
    return pltpu.CompilerParams(needs_layout_passes=False)


def _pack_bf16_pairs(a):
    bits = lax.bitcast_convert_type(a.astype(bf16), jnp.uint16).astype(u32)
    w = a.shape[1] // 2
    return bits[:, :w] | (bits[:, w:] << 16)


def _lane_permute(x, idx):
    return x.at[idx].get(mode="promise_in_bounds")


def _halves_f32(pairs_bf16):
    word = plsc.bitcast(pairs_bf16, u32)
    return plsc.bitcast(word << 16, f32), plsc.bitcast(word & jnp.uint32(0xFFFF0000), f32)


def _peer_sc_stage(side, ids, tab, mode):
    t_total, side_w = side.shape
    dw = tab.shape[1]
    out_w = N_SLOTS if mode == "dot" else 2 * dw
    tpw = t_total // SC_WORKERS
    n_pairs = tpw // 2
    assert tpw * SC_WORKERS == t_total and n_pairs * 2 == tpw and N_BATCH >= 2
    n_grp = ROW_BATCH // SC_LANES
    mesh = plsc.VectorSubcoreMesh(core_axis_name="c", subcore_axis_name="s")

    @functools.partial(
        pl.kernel, mesh=mesh,
        out_type=jax.ShapeDtypeStruct((t_total, out_w), f32),
        scratch_types=[
            pltpu.VMEM((N_SLOTS,), i32), pltpu.VMEM((N_SLOTS,), i32),
            pltpu.VMEM((side_w,), side.dtype), pltpu.VMEM((side_w,), side.dtype),
            pltpu.VMEM((out_w,), f32), pltpu.VMEM((out_w,), f32),
            pltpu.VMEM((N_BATCH, ROW_BATCH, dw), u32),
            pltpu.SemaphoreType.DMA((N_BATCH,)), pltpu.SemaphoreType.DMA((2,)),
            pltpu.SemaphoreType.DMA((2,)), pltpu.SemaphoreType.DMA((2,)),
        ],
        compiler_params=_sc_params())
    def k(side_hbm, ids_hbm, tab_hbm, out_hbm, idx0, idx1, side0, side1, out0, out1, rows_v,
          sem_g, sem_i, sem_s, sem_o):
        idx_v, side_v, out_v = (idx0, idx1), (side0, side1), (out0, out1)
        wid = lax.axis_index("s") * SC_CORES + lax.axis_index("c")
        base = wid * tpw
        iota = lax.iota(i32, SC_LANES)

        def row_pairs(slot, r, off):
            return plsc.bitcast(rows_v[slot, r, pl.ds(off, SC_LANES)], bf16)

        def gather(p, b):
            return pltpu.make_async_copy(
                tab_hbm.at[idx_v[p].at[pl.ds(b * ROW_BATCH, ROW_BATCH)]], rows_v.at[b], sem_g.at[b])

        def loads(t, p):
            return (pltpu.make_async_copy(ids_hbm.at[t], idx_v[p], sem_i.at[p]),
                    pltpu.make_async_copy(side_hbm.at[t], side_v[p], sem_s.at[p]))

        def store(t, p):
            return pltpu.make_async_copy(out_v[p], out_hbm.at[t], sem_o.at[p])

        def compute_dot(p, b):
            slot = b
            for g in range(n_grp):
                def cbody(c, accs):
                    off = pl.multiple_of(c * 4 * SC_LANES, 4 * SC_LANES)
                    xs = [plsc.bitcast(side_v[p][pl.ds(off + q * SC_LANES, SC_LANES)], bf16) for q in range(4)]
                    out = []
                    for j in range(SC_LANES):
                        t = [row_pairs(slot, g * SC_LANES + j, off + q * SC_LANES) * xs[q] for q in range(4)]
                        lo, hi = _halves_f32((t[0] + t[1]) + (t[2] + t[3]))
                        out.append(accs[j] + (lo + hi))
                    return tuple(out)
                accs = lax.fori_loop(0, dw // (4 * SC_LANES), cbody,
                                     tuple(jnp.zeros((SC_LANES,), f32) for _ in range(SC_LANES)))
                vs = list(accs)
                dist = SC_LANES // 2
                while dist >= 1:
                    keep = (iota & dist) == 0
                    swap = iota ^ dist
                    vs = [jnp.where(keep, vs[k], vs[k + dist]) + _lane_permute(jnp.where(keep, vs[k + dist], vs[k]), swap)
                          for k in range(dist)]
                    dist //= 2
                out_v[p][pl.ds(b * ROW_BATCH + g * SC_LANES, SC_LANES)] = vs[0]

        def compute_wsum(p, b):
            slot = b
            for g in range(n_grp):
                wv = side_v[p][pl.ds(b * ROW_BATCH + g * SC_LANES, SC_LANES)]
                sp = []
                for j in range(SC_LANES):
                    w = _lane_permute(wv, jnp.full((SC_LANES,), j, i32))
                    sp.append(plsc.pack(w, w, format=plsc.PackFormat.INTERLEAVED))
                first = b == 0 and g == 0

                @plsc.parallel_loop(0, dw, step=SC_LANES)
                def _(c):
                    off = pl.multiple_of(c, SC_LANES)
                    t = [row_pairs(slot, g * SC_LANES + j, off) * sp[j] for j in range(SC_LANES)]
                    quads = [_halves_f32((t[q] + t[q + 1]) + (t[q + 2] + t[q + 3])) for q in range(0, SC_LANES, 4)]
                    lo = (quads[0][0] + quads[1][0]) + (quads[2][0] + quads[3][0])
                    hi = (quads[0][1] + quads[1][1]) + (quads[2][1] + quads[3][1])
                    if first:
                        out_v[p][pl.ds(off, SC_LANES)] = lo
                        out_v[p][pl.ds(dw + off, SC_LANES)] = hi
                    else:
                        plsc.addupdate(out_v[p].at[pl.ds(off, SC_LANES)], lo)
                        plsc.addupdate(out_v[p].at[pl.ds(dw + off, SC_LANES)], hi)

        compute = compute_dot if mode == "dot" else compute_wsum

        for c in loads(base, 0):
            c.start()
        for c in loads(base, 0):
            c.wait()
        for b in range(N_BATCH - 1):
            gather(0, b).start()
        for c in loads(base + 1, 1):
            c.start()

        def pair_body(i2, carry):
            not_last = i2 < n_pairs - 1
            for p in (0, 1):
                t = base + 2 * i2 + p

                @pl.when(i2 > 0)
                def _():
                    store(t - 2, p).wait()

                for b in range(N_BATCH):
                    if b == 0:
                        gather(p, N_BATCH - 1).start()
                    else:
                        def start_next(b=b):
                            if b == 1:
                                for c in loads(t + 1, 1 - p):
                                    c.wait()
                            gather(1 - p, b - 1).start()
                        if p == 0:
                            start_next()
                        else:
                            pl.when(not_last)(start_next)
                    gather(p, b).wait()
                    compute(p, b)
                store(t, p).start()

                @pl.when(not_last)
                def _():
                    for c in loads(t + 2, p):
                        c.start()
            return carry

        lax.fori_loop(0, n_pairs, pair_body, 0)
        store(base + tpw - 2, 0).wait()
        store(base + tpw - 1, 1).wait()

    return k(side, ids, tab)


def _peer(x2d, g, w_q, sub_keys, u_pairs, v_pairs):
    h, q3 = _peer_q(x2d, g, w_q)
    ids, gates = _peer_topk(q3, sub_keys, 0, 1)
    a = _peer_sc_stage(_pack_bf16_pairs(h), ids, u_pairs, "dot")
    w = _peer_gate(a, gates)
    return _peer_sc_stage(w, ids, v_pairs, "wsum")


def kernel(x, norm_mix, norm_ffn, norm_final, mix_w_in, s5_lambda_re, s5_lambda_im, s5_log_step, s5_b_re, s5_b_im, s5_c_re, s5_c_im, s5_d, s5_w_glu, conv_dw_w, conv_dw_b, conv_ln_g, conv_ln_b, mix_w_out, attn_w_qkv, attn_b_qkv, attn_sinks, attn_w_o, peer_w_q, peer_sub_keys, peer_u, peer_v):
    bsz, seq, d = x.shape
    tabs = _s5_tables(s5_lambda_re[0], s5_lambda_im[0], s5_log_step[0], s5_b_re[0], s5_b_im[0],
                      s5_c_re[0], s5_c_im[0])

    pairs = [(_pack_bf16_pairs(peer_u[l]), _pack_bf16_pairs(peer_v[l])) for l in range(2)]
    tc = seq // PEER_CHUNKS

    def peer(xc, l):
        return _peer(xc, norm_ffn[l], peer_w_q[l], peer_sub_keys[l], *pairs[l])

    def trunk(xb):
        z = _mix_in(xb.reshape(seq, d), norm_mix[0], mix_w_in[0]).reshape(1, seq, -1)
        state = _mixer_state0()
        x1, p0 = [], []
        for c in range(PEER_CHUNKS):
            xc, state = _mixer(z, xb, state, c, PEER_CHUNKS, tabs, s5_d[0], s5_w_glu[0], conv_dw_w[0],
                               conv_dw_b[0], conv_ln_g[0], conv_ln_b[0], mix_w_out[0])
            x1.append(xc.reshape(tc, d))
            p0.append(peer(x1[-1], 0))
        x2, qkv = _qkv(jnp.concatenate(x1, axis=0), jnp.concatenate(p0, axis=0), norm_mix[1],
                       attn_w_qkv[0], attn_b_qkv[0], seq)
        x3 = _attention(qkv.reshape(1, seq, -1), x2.reshape(1, seq, d), attn_sinks[0], attn_w_o[0]).reshape(seq, d)
        p1 = jnp.concatenate([peer(x3[c * tc:(c + 1) * tc], 1) for c in range(PEER_CHUNKS)], axis=0)
        return _final_norm(x3, p1, norm_final).reshape(1, seq, d)

    return jnp.concatenate([trunk(x[b:b + 1]) for b in range(bsz)], axis=0)
```

```python
import functools

import jax
import jax.numpy as jnp
from jax import lax
from jax.experimental import pallas as pl
from jax.experimental.pallas import tpu as pltpu
from jax.experimental.pallas import tpu_sc as plsc

f32 = jnp.float32
bf16 = jnp.bfloat16
i32 = jnp.int32
u32 = jnp.uint32

D_MODEL = 1024
S5_WIDTH = 512
S5_GROUP = 16
S5_GROUPS = 32
S5_STATE = 64
S5_LANES = S5_GROUPS * S5_STATE
CONV_WIDTH = 512
CONV_K = 31
HEAD_DIM = 64
N_Q_HEADS = 16
N_KV_HEADS = 2
Q_PER_KV = 8
Q_WIDTH = N_Q_HEADS * HEAD_DIM
KV_WIDTH = 2 * N_KV_HEADS * HEAD_DIM
QKV_WIDTH = Q_WIDTH + KV_WIDTH
WINDOW = 128
ROPE_THETA = 500000.0
ROPE_DIM = 16
PEER_HEADS = 8
PEER_N_KEYS = 128
PEER_HALF = 128
PEER_TOPK = 16
N_SLOTS = PEER_HEADS * PEER_TOPK
NORM_EPS = 1e-6

LANES = 128
SUBLANES = 8
TC_VMEM_LIMIT = 48 * 1024 * 1024

SC_CORES = 2
SC_SUBCORES = 16
SC_LANES = 16
SC_WORKERS = SC_CORES * SC_SUBCORES
ROW_BATCH = 32
N_BATCH = N_SLOTS // ROW_BATCH

ROW_TILE = 512
MIX_TILE = 256
SCAN_TILE = 128
TOPK_TILE = 128
PEER_CHUNKS = 4
CONV_HALO = 32


def _tc_params(*sem):
    return pltpu.CompilerParams(dimension_semantics=sem, vmem_limit_bytes=TC_VMEM_LIMIT)


def _rms(x, g):
    return x * lax.rsqrt(jnp.mean(x * x, axis=-1, keepdims=True) + NORM_EPS) * g


def _sigmoid(x):
    return 1.0 / (1.0 + jnp.exp(-x))


def _gelu(x):
    return 0.5 * x * (1.0 + lax.erf(x * 0.7071067811865476))


def _dot(a, b):
    return jnp.dot(a, b, preferred_element_type=f32)


def _mix_in_kernel(x_ref, g_ref, w_ref, z_ref):
    h = _rms(x_ref[...], g_ref[...])
    z_ref[...] = _dot(h.astype(bf16), w_ref[...])


def _mix_in(x2d, g, w):
    t, d = x2d.shape
    n = w.shape[1]
    return pl.pallas_call(
        _mix_in_kernel,
        grid=(t // ROW_TILE,),
        in_specs=[pl.BlockSpec((ROW_TILE, d), lambda i: (i, 0)),
                  pl.BlockSpec((1, d), lambda i: (0, 0)),
                  pl.BlockSpec((d, n), lambda i: (0, 0))],
        out_specs=pl.BlockSpec((ROW_TILE, n), lambda i: (i, 0)),
        out_shape=jax.ShapeDtypeStruct((t, n), f32),
        compiler_params=_tc_params("parallel"),
        name="mix_in",
    )(x2d, g.reshape(1, d), w.astype(bf16))


def _peer_q_kernel(x_ref, g_ref, w_ref, h_ref, q_ref):
    h = _rms(x_ref[...], g_ref[...])
    h_ref[...] = h
    q = _dot(h.astype(bf16), w_ref[...])
    for c in range(2 * PEER_HEADS):
        q_ref[c] = q[:, c * PEER_HALF:(c + 1) * PEER_HALF]


def _peer_q(x2d, g, w_q):
    t, d = x2d.shape
    nq = 2 * PEER_HEADS
    return pl.pallas_call(
        _peer_q_kernel,
        grid=(t // ROW_TILE,),
        in_specs=[pl.BlockSpec((ROW_TILE, d), lambda i: (i, 0)),
                  pl.BlockSpec((1, d), lambda i: (0, 0)),
                  pl.BlockSpec((d, nq * PEER_HALF), lambda i: (0, 0))],
        out_specs=[pl.BlockSpec((ROW_TILE, d), lambda i: (i, 0)),
                   pl.BlockSpec((nq, ROW_TILE, PEER_HALF), lambda i: (0, i, 0))],
        out_shape=[jax.ShapeDtypeStruct((t, d), f32),
                   jax.ShapeDtypeStruct((nq, t, PEER_HALF), f32)],
        compiler_params=_tc_params("parallel"),
        name="peer_q",
    )(x2d, g.reshape(1, d), w_q.astype(bf16))


def _qkv_kernel(x_ref, r_ref, g_ref, w_ref, b_ref, cos_ref, sin_ref, xo_ref, qkv_ref):
    x = x_ref[...] + r_ref[...]
    xo_ref[...] = x
    h = _rms(x, g_ref[...])
    qkv = _dot(h.astype(bf16), w_ref[...]) + b_ref[...]
    cos = cos_ref[...]
    sin = sin_ref[...]
    lane = lax.broadcasted_iota(i32, cos.shape, 1)
    low = (lane % HEAD_DIM) < (ROPE_DIM // 2)
    n_rot = (Q_WIDTH + N_KV_HEADS * HEAD_DIM) // LANES
    for c in range(QKV_WIDTH // LANES):
        t = qkv[:, c * LANES:(c + 1) * LANES]
        if c < n_rot:
            partner = jnp.where(low, pltpu.roll(t, LANES - ROPE_DIM // 2, axis=1),
                                pltpu.roll(t, ROPE_DIM // 2, axis=1))
            t = t * cos + partner * sin
        if c < Q_WIDTH // LANES:
            t = t * (HEAD_DIM ** -0.5)
        qkv_ref[:, c * LANES:(c + 1) * LANES] = t


def _rope_tables(seq):
    inv = jnp.power(ROPE_THETA, -jnp.arange(0, ROPE_DIM, 2, dtype=f32) / ROPE_DIM)
    ang = jnp.arange(seq, dtype=f32)[:, None] * inv[None, :]
    cos, sin = jnp.cos(ang), jnp.sin(ang)
    pad = HEAD_DIM - ROPE_DIM
    cos_h = jnp.concatenate([cos, cos, jnp.ones((seq, pad), f32)], axis=1)
    sin_h = jnp.concatenate([-sin, sin, jnp.zeros((seq, pad), f32)], axis=1)
    return jnp.tile(cos_h, (1, LANES // HEAD_DIM)), jnp.tile(sin_h, (1, LANES // HEAD_DIM))


def _qkv(x2d, r2d, g, w, b, seq):
    t, d = x2d.shape
    n = w.shape[1]
    cos, sin = _rope_tables(seq)
    blocks_per_seq = seq // ROW_TILE
    return pl.pallas_call(
        _qkv_kernel,
        grid=(t // ROW_TILE,),
        in_specs=[pl.BlockSpec((ROW_TILE, d), lambda i: (i, 0)),
                  pl.BlockSpec((ROW_TILE, d), lambda i: (i, 0)),
                  pl.BlockSpec((1, d), lambda i: (0, 0)),
                  pl.BlockSpec((d, n), lambda i: (0, 0)),
                  pl.BlockSpec((1, n), lambda i: (0, 0)),
                  pl.BlockSpec((ROW_TILE, LANES), lambda i: (i % blocks_per_seq, 0)),
                  pl.BlockSpec((ROW_TILE, LANES), lambda i: (i % blocks_per_seq, 0))],
        out_specs=[pl.BlockSpec((ROW_TILE, d), lambda i: (i, 0)),
                   pl.BlockSpec((ROW_TILE, n), lambda i: (i, 0))],
        out_shape=[jax.ShapeDtypeStruct((t, d), f32), jax.ShapeDtypeStruct((t, n), f32)],
        compiler_params=_tc_params("parallel"),
        name="qkv",
    )(x2d, r2d, g.reshape(1, d), w.astype(bf16), b.reshape(1, n), cos, sin)


def _final_norm_kernel(x_ref, r_ref, g_ref, o_ref):
    o_ref[...] = _rms(x_ref[...] + r_ref[...], g_ref[...])


def _final_norm(x2d, r2d, g):
    t, d = x2d.shape
    spec = pl.BlockSpec((ROW_TILE, d), lambda i: (i, 0))
    return pl.pallas_call(
        _final_norm_kernel,
        grid=(t // ROW_TILE,),
        in_specs=[spec, spec, pl.BlockSpec((1, d), lambda i: (0, 0))],
        out_specs=spec,
        out_shape=jax.ShapeDtypeStruct((t, d), f32),
        compiler_params=_tc_params("parallel"),
        name="final_norm",
    )(x2d, r2d, g.reshape(1, d))


def _shift_rows(v, d, row):
    if d % SUBLANES == 0:
        return jnp.concatenate([jnp.zeros((d, v.shape[1]), v.dtype), v[:v.shape[0] - d]], axis=0)
    return jnp.where(row >= d, pltpu.roll(v, d, axis=0), 0.0)


def _mixer_kernel(z_ref, x_ref, apr_ref, api_ref, bre_ref, bim_ref, cre_ref, cim_ref, dvec_ref, wglu_ref,
                  dww_ref, dwb_ref, lng_ref, lnb_ref, wout_ref, sre_in, sim_in, tail_in,
                  o_ref, sre_out, sim_out, tail_out,
                  carry_re, carry_im, y_s, vbuf):
    tb = z_ref.shape[0]
    n_chunks = S5_LANES // LANES
    steps = [1 << s for s in range(SCAN_TILE.bit_length() - 1)]

    @pl.when(pl.program_id(1) == 0)
    def _():
        carry_re[...] = sre_in[...]
        carry_im[...] = sim_in[...]
        vbuf[0:CONV_HALO, :] = tail_in[...]

    u = z_ref[:, 0:S5_WIDTH]
    ub = u.astype(bf16)
    y_s[...] = u * dvec_ref[...]
    row = lax.broadcasted_iota(i32, (SCAN_TILE, LANES), 0)

    def chunk_body(j, carry):
        bu_re = _dot(ub, bre_ref[j])
        bu_im = _dot(ub, bim_ref[j])
        apr = apr_ref[j]
        api = api_ref[j]
        c_re = carry_re[j]
        c_im = carry_im[j]
        h_re_tiles, h_im_tiles = [], []
        for r in range(tb // SCAN_TILE):
            hr = bu_re[r * SCAN_TILE:(r + 1) * SCAN_TILE]
            hi = bu_im[r * SCAN_TILE:(r + 1) * SCAN_TILE]
            a_re, a_im = apr[0:1], api[0:1]
            first = row == 0
            hr = hr + jnp.where(first, a_re * c_re - a_im * c_im, 0.0)
            hi = hi + jnp.where(first, a_re * c_im + a_im * c_re, 0.0)
            for s, d in enumerate(steps):
                a_re, a_im = apr[s:s + 1], api[s:s + 1]
                sr = _shift_rows(hr, d, row)
                si = _shift_rows(hi, d, row)
                hr, hi = hr + (a_re * sr - a_im * si), hi + (a_re * si + a_im * sr)
            c_re = hr[SCAN_TILE - 1:SCAN_TILE]
            c_im = hi[SCAN_TILE - 1:SCAN_TILE]
            h_re_tiles.append(hr.astype(bf16))
            h_im_tiles.append(hi.astype(bf16))
        carry_re[j] = c_re
        carry_im[j] = c_im
        h_re = jnp.concatenate(h_re_tiles, axis=0)
        h_im = jnp.concatenate(h_im_tiles, axis=0)
        y_s[...] += _dot(h_re, cre_ref[j]) - _dot(h_im, cim_ref[j])
        return carry

    lax.fori_loop(0, n_chunks, chunk_body, 0)
    y = _gelu(y_s[...])
    y_ssm = y * _sigmoid(_dot(y.astype(bf16), wglu_ref[...]))

    v = z_ref[:, S5_WIDTH:S5_WIDTH + CONV_WIDTH] * _sigmoid(z_ref[:, S5_WIDTH + CONV_WIDTH:S5_WIDTH + 2 * CONV_WIDTH])
    vbuf[CONV_HALO:CONV_HALO + tb, :] = v
    acc = jnp.broadcast_to(dwb_ref[...], (tb, CONV_WIDTH))
    for k in range(CONV_K):
        acc = acc + dww_ref[k:k + 1, :] * vbuf[pl.ds(CONV_HALO - (CONV_K - 1) + k, tb), :]
    vbuf[0:CONV_HALO, :] = vbuf[tb:tb + CONV_HALO, :]
    mu = jnp.mean(acc, axis=-1, keepdims=True)
    cen = acc - mu
    var = jnp.mean(cen * cen, axis=-1, keepdims=True)
    yn = cen * lax.rsqrt(var + NORM_EPS) * lng_ref[...] + lnb_ref[...]
    y_conv = yn * _sigmoid(yn)

    o_ref[...] = (x_ref[...] + _dot(y_ssm.astype(bf16), wout_ref[0:S5_WIDTH, :])
                  + _dot(y_conv.astype(bf16), wout_ref[S5_WIDTH:S5_WIDTH + CONV_WIDTH, :]))

    @pl.when(pl.program_id(1) == pl.num_programs(1) - 1)
    def _():
        sre_out[...] = carry_re[...]
        sim_out[...] = carry_im[...]
        tail_out[...] = vbuf[0:CONV_HALO, :]


def _s5_tables(lam_re, lam_im, log_step, b_re, b_im, c_re, c_im):
    n_chunks = S5_LANES // LANES
    g_per_chunk = LANES // S5_STATE
    lam = lax.complex(lam_re, lam_im)
    dt = jnp.exp(log_step)[:, None]
    a_bar = jnp.exp(lam * dt)
    b_bar = ((a_bar - 1.0) / lam)[..., None] * lax.complex(b_re, b_im)
    n_steps = SCAN_TILE.bit_length() - 1
    powers = jnp.stack([jnp.exp(lam * dt * float(1 << s)) for s in range(n_steps)]
                       + [jnp.ones_like(a_bar)] * (SUBLANES - n_steps))
    powers = powers.reshape(SUBLANES, n_chunks, LANES).transpose(1, 0, 2)
    eye = jnp.eye(S5_GROUPS, dtype=f32)

    def in_mat(b):
        m = jnp.einsum('gph,gk->ghkp', b, eye).reshape(S5_WIDTH, S5_LANES)
        return m.reshape(S5_WIDTH, n_chunks, LANES).transpose(1, 0, 2).astype(bf16)

    def out_mat(c):
        m = jnp.einsum('ghp,gk->gpkh', c, eye).reshape(S5_LANES, S5_WIDTH)
        return m.reshape(n_chunks, LANES, S5_WIDTH).astype(bf16)

    del g_per_chunk
    return (jnp.real(powers), jnp.imag(powers), in_mat(jnp.real(b_bar)), in_mat(jnp.imag(b_bar)),
            out_mat(c_re), out_mat(c_im))


def _mixer_state0():
    n_chunks = S5_LANES // LANES
    return (jnp.zeros((n_chunks, 1, LANES), f32), jnp.zeros((n_chunks, 1, LANES), f32),
            jnp.zeros((CONV_HALO, CONV_WIDTH), f32))


def _mixer(z, x, state, part, n_parts, s5_tabs, d_vec, w_glu, dw_w, dw_b, ln_g, ln_b, w_out):
    bsz, seq_all, zw = z.shape
    assert bsz == 1
    seq = seq_all // n_parts
    d = x.shape[-1]
    apr, api, bre, bim, cre, cim = s5_tabs
    n_chunks = S5_LANES // LANES
    tb = MIX_TILE
    first = part * (seq // tb)
    const2 = lambda b, n: (0, 0)
    const3 = lambda b, n: (0, 0, 0)
    state_specs = [pl.BlockSpec((n_chunks, 1, LANES), const3), pl.BlockSpec((n_chunks, 1, LANES), const3),
                   pl.BlockSpec((CONV_HALO, CONV_WIDTH), const2)]
    out, s_re, s_im, tail = pl.pallas_call(
        _mixer_kernel,
        grid=(bsz, seq // tb),
        in_specs=[pl.BlockSpec((None, tb, zw), lambda b, n: (b, first + n, 0)),
                  pl.BlockSpec((None, tb, d), lambda b, n: (b, first + n, 0)),
                  pl.BlockSpec((n_chunks, SUBLANES, LANES), const3),
                  pl.BlockSpec((n_chunks, SUBLANES, LANES), const3),
                  pl.BlockSpec((n_chunks, S5_WIDTH, LANES), const3),
                  pl.BlockSpec((n_chunks, S5_WIDTH, LANES), const3),
                  pl.BlockSpec((n_chunks, LANES, S5_WIDTH), const3),
                  pl.BlockSpec((n_chunks, LANES, S5_WIDTH), const3),
                  pl.BlockSpec((1, S5_WIDTH), const2),
                  pl.BlockSpec((S5_WIDTH, S5_WIDTH), const2),
                  pl.BlockSpec((CONV_K, CONV_WIDTH), const2),
                  pl.BlockSpec((1, CONV_WIDTH), const2),
                  pl.BlockSpec((1, CONV_WIDTH), const2),
                  pl.BlockSpec((1, CONV_WIDTH), const2),
                  pl.BlockSpec((S5_WIDTH + CONV_WIDTH, d), const2)] + state_specs,
        out_specs=[pl.BlockSpec((None, tb, d), lambda b, n: (b, n, 0))] + state_specs,
        out_shape=[jax.ShapeDtypeStruct((bsz, seq, d), f32),
                   jax.ShapeDtypeStruct((n_chunks, 1, LANES), f32),
                   jax.ShapeDtypeStruct((n_chunks, 1, LANES), f32),
                   jax.ShapeDtypeStruct((CONV_HALO, CONV_WIDTH), f32)],
        scratch_shapes=[pltpu.VMEM((n_chunks, 1, LANES), f32),
                        pltpu.VMEM((n_chunks, 1, LANES), f32),
                        pltpu.VMEM((tb, S5_WIDTH), f32),
                        pltpu.VMEM((tb + CONV_HALO, CONV_WIDTH), f32)],
        compiler_params=_tc_params("arbitrary", "arbitrary"),
        name="mixer",
    )(z, x, apr, api, bre, bim, cre, cim, d_vec.reshape(1, S5_WIDTH), w_glu.astype(bf16),
      dw_w, dw_b.reshape(1, -1), ln_g.reshape(1, -1), ln_b.reshape(1, -1), w_out.astype(bf16), *state)
    return out, (s_re, s_im, tail)


def _attn_kernel(q_ref, kvc_ref, kvp_ref, x_ref, sink_ref, wo_ref, o_ref, o_s):
    n = pl.program_id(1)
    qi = lax.broadcasted_iota(i32, (WINDOW, 2 * WINDOW), 0)
    si = lax.broadcasted_iota(i32, (WINDOW, 2 * WINDOW), 1)
    first_key = jnp.where(n > 0, 0, WINDOW)
    valid = (si > qi) & (si <= qi + WINDOW) & (si >= first_key)
    kv = jnp.concatenate([kvp_ref[...], kvc_ref[...]], axis=0).astype(bf16)
    for kh in range(N_KV_HEADS):
        k = kv[:, kh * HEAD_DIM:(kh + 1) * HEAD_DIM]
        v = kv[:, (N_KV_HEADS + kh) * HEAD_DIM:(N_KV_HEADS + kh + 1) * HEAD_DIM]
        for g in range(Q_PER_KV):
            h = kh * Q_PER_KV + g
            q = q_ref[:, h * HEAD_DIM:(h + 1) * HEAD_DIM].astype(bf16)
            s = lax.dot_general(q, k, (((1,), (1,)), ((), ())), preferred_element_type=f32)
            s = jnp.where(valid, s, -jnp.inf)
            sink = sink_ref[0:1, h:h + 1]
            m = jnp.maximum(jnp.max(s, axis=-1, keepdims=True), sink)
            p = jnp.exp(s - m)
            denom = jnp.sum(p, axis=-1, keepdims=True) + jnp.exp(sink - m)
            probs = (p / denom).astype(bf16)
            o_s[:, h * HEAD_DIM:(h + 1) * HEAD_DIM] = _dot(probs, v)
    o_ref[...] = x_ref[...] + _dot(o_s[...].astype(bf16), wo_ref[...])


def _attention(qkv, x, sinks, w_o):
    bsz, seq, _ = qkv.shape
    d = x.shape[-1]
    kv_block = Q_WIDTH // KV_WIDTH
    return pl.pallas_call(
        _attn_kernel,
        grid=(bsz, seq // WINDOW),
        in_specs=[pl.BlockSpec((None, WINDOW, Q_WIDTH), lambda b, n: (b, n, 0)),
                  pl.BlockSpec((None, WINDOW, KV_WIDTH), lambda b, n: (b, n, kv_block)),
                  pl.BlockSpec((None, WINDOW, KV_WIDTH), lambda b, n: (b, jnp.maximum(n - 1, 0), kv_block)),
                  pl.BlockSpec((None, WINDOW, d), lambda b, n: (b, n, 0)),
                  pl.BlockSpec((1, N_Q_HEADS), lambda b, n: (0, 0)),
                  pl.BlockSpec((Q_WIDTH, d), lambda b, n: (0, 0))],
        out_specs=pl.BlockSpec((None, WINDOW, d), lambda b, n: (b, n, 0)),
        out_shape=jax.ShapeDtypeStruct((bsz, seq, d), f32),
        scratch_shapes=[pltpu.VMEM((WINDOW, Q_WIDTH), f32)],
        compiler_params=_tc_params("parallel", "parallel"),
        name="attention",
    )(qkv, qkv, qkv, x, sinks.reshape(1, N_Q_HEADS), w_o.astype(bf16))


def _top16(s, ids=None):
    m_rows = s.shape[0]
    pos = lax.broadcasted_iota(i32, s.shape, 0).astype(f32)
    vals, outs = [], []
    for _ in range(PEER_TOPK):
        mx = jnp.max(s, axis=0, keepdims=True)
        first = jnp.min(jnp.where(s == mx, pos, float(m_rows)), axis=0, keepdims=True)
        hit = pos == first
        vals.append(mx)
        outs.append(first if ids is None else jnp.sum(jnp.where(hit, ids, 0.0), axis=0, keepdims=True))
        s = jnp.where(hit, -jnp.inf, s)
    return jnp.concatenate(vals, axis=0), jnp.concatenate(outs, axis=0)


def _pair_candidates(a0, a1, combine, fill):
    n = a0.shape[1]
    sub = lax.broadcasted_iota(i32, (SUBLANES, n), 0)
    pieces = [combine(a0[0:1], a1)]
    for i in range(1, SUBLANES):
        keep = PEER_TOPK // (i + 1)
        pieces.append(jnp.where(sub < keep, combine(a0[i:i + 1], a1[0:SUBLANES]), fill))
    pieces.append(combine(a0[SUBLANES:PEER_TOPK], a1[0:1]))
    return jnp.concatenate(pieces, axis=0)


def _topk_kernel(q_ref, keys_ref, ids_ref, gates_ref, ids_s, gates_s):
    def head_body(h, carry):
        tops = []
        for c in range(2):
            q = q_ref[2 * h + c].astype(bf16)
            s = lax.dot_general(keys_ref[2 * h + c], q, (((1,), (1,)), ((), ())),
                                preferred_element_type=f32)
            tops.append(_top16(s))
        (s0, i0), (s1, i1) = tops
        cand_s = _pair_candidates(s0, s1, lambda a, b: a + b, -jnp.inf)
        cand_i = _pair_candidates(i0, i1, lambda a, b: a * float(PEER_N_KEYS) + b, 0.0)
        best_s, best_i = _top16(cand_s, cand_i)
        e = jnp.exp(best_s - best_s[0:1])
        off = pl.multiple_of(h * PEER_TOPK, PEER_TOPK)
        gates_s[pl.ds(off, PEER_TOPK), :] = e / jnp.sum(e, axis=0, keepdims=True)
        ids_s[pl.ds(off, PEER_TOPK), :] = best_i.astype(i32)
        return carry

    lax.fori_loop(0, PEER_HEADS, head_body, 0, unroll=4)
    ids_ref[...] = ids_s[...].T
    gates_ref[...] = gates_s[...].T


def _peer_topk(q3, sub_keys, chunk, n_chunks):
    nq, t_all, half = q3.shape
    keys = sub_keys.reshape(nq, PEER_N_KEYS, half).astype(bf16)
    tb = TOPK_TILE
    t = t_all // n_chunks
    first = chunk * (t // tb)
    return pl.pallas_call(
        _topk_kernel,
        grid=(t // tb,),
        in_specs=[pl.BlockSpec((nq, tb, half), lambda i: (0, first + i, 0)),
                  pl.BlockSpec((nq, PEER_N_KEYS, half), lambda i: (0, 0, 0))],
        out_specs=[pl.BlockSpec((tb, N_SLOTS), lambda i: (i, 0)),
                   pl.BlockSpec((tb, N_SLOTS), lambda i: (i, 0))],
        out_shape=[jax.ShapeDtypeStruct((t, N_SLOTS), i32), jax.ShapeDtypeStruct((t, N_SLOTS), f32)],
        scratch_shapes=[pltpu.VMEM((N_SLOTS, tb), i32), pltpu.VMEM((N_SLOTS, tb), f32)],
        compiler_params=_tc_params("parallel"),
        name="peer_topk",
    )(q3, keys)


def _gate_kernel(a_ref, g_ref, w_ref):
    w_ref[...] = _gelu(a_ref[...]) * g_ref[...]


def _peer_gate(a, gates):
    t, n = a.shape
    tm = 2048
    spec = pl.BlockSpec((tm, n), lambda i: (i, 0))
    return pl.pallas_call(
        _gate_kernel, grid=(t // tm,), in_specs=[spec, spec], out_specs=spec,
        out_shape=jax.ShapeDtypeStruct((t, n), f32),
        compiler_params=_tc_params("parallel"), name="peer_gate",
    )(a, gates)


def _sc_params():
    return pltpu.CompilerParams(needs_layout_passes=False)


def _pack_bf16_pairs(a):
    bits = lax.bitcast_convert_type(a.astype(bf16), jnp.uint16).astype(u32)
    w = a.shape[1] // 2
    return bits[:, :w] | (bits[:, w:] << 16)


def _lane_permute(x, idx):
    return x.at[idx].get(mode="promise_in_bounds")


def _halves_f32(pairs_bf16):
    word = plsc.bitcast(pairs_bf16, u32)
    return plsc.bitcast(word << 16, f32), plsc.bitcast(word & jnp.uint32(0xFFFF0000), f32)


def _peer_sc_stage(side, ids, tab, mode):
    t_total, side_w = side.shape
    dw = tab.shape[1]
    out_w = N_SLOTS if mode == "dot" else 2 * dw
    tpw = t_total // SC_WORKERS
    n_pairs = tpw // 2
    assert tpw * SC_WORKERS == t_total and n_pairs * 2 == tpw and N_BATCH >= 2
    n_grp = ROW_BATCH // SC_LANES
    mesh = plsc.VectorSubcoreMesh(core_axis_name="c", subcore_axis_name="s")

    @functools.partial(
        pl.kernel, mesh=mesh,
        out_type=jax.ShapeDtypeStruct((t_total, out_w), f32),
        scratch_types=[
            pltpu.VMEM((N_SLOTS,), i32), pltpu.VMEM((N_SLOTS,), i32),
            pltpu.VMEM((side_w,), side.dtype), pltpu.VMEM((side_w,), side.dtype),
            pltpu.VMEM((out_w,), f32), pltpu.VMEM((out_w,), f32),
            pltpu.VMEM((N_BATCH, ROW_BATCH, dw), u32),
            pltpu.SemaphoreType.DMA((N_BATCH,)), pltpu.SemaphoreType.DMA((2,)),
            pltpu.SemaphoreType.DMA((2,)), pltpu.SemaphoreType.DMA((2,)),
        ],
        compiler_params=_sc_params())
    def k(side_hbm, ids_hbm, tab_hbm, out_hbm, idx0, idx1, side0, side1, out0, out1, rows_v,
          sem_g, sem_i, sem_s, sem_o):
        idx_v, side_v, out_v = (idx0, idx1), (side0, side1), (out0, out1)
        wid = lax.axis_index("s") * SC_CORES + lax.axis_index("c")
        base = wid * tpw
        iota = lax.iota(i32, SC_LANES)

        def row_pairs(slot, r, off):
            return plsc.bitcast(rows_v[slot, r, pl.ds(off, SC_LANES)], bf16)

        def gather(p, b):
            first = b * ROW_BATCH if isinstance(b, int) else pl.multiple_of(b * ROW_BATCH, ROW_BATCH)
            return pltpu.make_async_copy(
                tab_hbm.at[idx_v[p].at[pl.ds(first, ROW_BATCH)]], rows_v.at[b], sem_g.at[b])

        def loads(t, p):
            return (pltpu.make_async_copy(ids_hbm.at[t], idx_v[p], sem_i.at[p]),
                    pltpu.make_async_copy(side_hbm.at[t], side_v[p], sem_s.at[p]))

        def store(t, p):
            return pltpu.make_async_copy(out_v[p], out_hbm.at[t], sem_o.at[p])

        def compute_dot(p, b):
            def group(g, carry):
                r0 = pl.multiple_of(g * SC_LANES, SC_LANES)

                def cbody(c, accs):
                    off = pl.multiple_of(c * LANES, LANES)
                    n_q = LANES // SC_LANES
                    xs = [plsc.bitcast(side_v[p][pl.ds(off + q * SC_LANES, SC_LANES)], bf16) for q in range(n_q)]
                    out = []
                    for j in range(SC_LANES):
                        t = [row_pairs(b, r0 + j, off + q * SC_LANES) * xs[q] for q in range(n_q)]
                        lo0, hi0 = _halves_f32((t[0] + t[1]) + (t[2] + t[3]))
                        lo1, hi1 = _halves_f32((t[4] + t[5]) + (t[6] + t[7]))
                        out.append(accs[j] + ((lo0 + hi0) + (lo1 + hi1)))
                    return tuple(out)
                accs = lax.fori_loop(0, dw // LANES, cbody,
                                     tuple(jnp.zeros((SC_LANES,), f32) for _ in range(SC_LANES)))
                vs = list(accs)
                dist = SC_LANES // 2
                while dist >= 1:
                    keep = (iota & dist) == 0
                    swap = iota ^ dist
                    vs = [jnp.where(keep, vs[k], vs[k + dist]) + _lane_permute(jnp.where(keep, vs[k + dist], vs[k]), swap)
                          for k in range(dist)]
                    dist //= 2
                out_v[p][pl.ds(pl.multiple_of(b * ROW_BATCH + r0, SC_LANES), SC_LANES)] = vs[0]
                return carry

            lax.fori_loop(0, n_grp, group, 0)

        def compute_wsum(p, b):
            def group(g, carry):
                r0 = pl.multiple_of(g * SC_LANES, SC_LANES)
                wv = side_v[p][pl.ds(pl.multiple_of(b * ROW_BATCH + r0, SC_LANES), SC_LANES)]
                sp = []
                for j in range(SC_LANES):
                    w = _lane_permute(wv, jnp.full((SC_LANES,), j, i32))
                    sp.append(plsc.pack(w, w, format=plsc.PackFormat.INTERLEAVED))

                @plsc.parallel_loop(0, dw, step=SC_LANES)
                def _(c):
                    off = pl.multiple_of(c, SC_LANES)
                    t = [row_pairs(b, r0 + j, off) * sp[j] for j in range(SC_LANES)]
                    quads = [_halves_f32((t[q] + t[q + 1]) + (t[q + 2] + t[q + 3])) for q in range(0, SC_LANES, 4)]
                    lo = (quads[0][0] + quads[1][0]) + (quads[2][0] + quads[3][0])
                    hi = (quads[0][1] + quads[1][1]) + (quads[2][1] + quads[3][1])
                    plsc.addupdate(out_v[p].at[pl.ds(off, SC_LANES)], lo)
                    plsc.addupdate(out_v[p].at[pl.ds(dw + off, SC_LANES)], hi)
                return carry

            lax.fori_loop(0, n_grp, group, 0)

        compute = compute_dot if mode == "dot" else compute_wsum

        for c in loads(base, 0):
            c.start()
        for c in loads(base, 0):
            c.wait()
        for b in range(N_BATCH - 1):
            gather(0, b).start()
        for c in loads(base + 1, 1):
            c.start()

        def pair_body(i2, carry):
            not_last = i2 < n_pairs - 1
            for p in (0, 1):
                t = base + 2 * i2 + p

                @pl.when(i2 > 0)
                def _():
                    store(t - 2, p).wait()

                if mode == "wsum":
                    @pl.loop(0, out_w, step=SC_LANES)
                    def _(c):
                        out_v[p][pl.ds(pl.multiple_of(c, SC_LANES), SC_LANES)] = jnp.zeros((SC_LANES,), f32)

                def batch_body(b, carry2):
                    @pl.when(b == 0)
                    def _():
                        gather(p, N_BATCH - 1).start()

                    def start_next():
                        @pl.when(b == 1)
                        def _():
                            for c in loads(t + 1, 1 - p):
                                c.wait()
                        gather(1 - p, b - 1).start()

                    pl.when(b > 0 if p == 0 else jnp.logical_and(b > 0, not_last))(start_next)
                    gather(p, b).wait()
                    compute(p, b)
                    return carry2

                lax.fori_loop(0, N_BATCH, batch_body, 0)
                store(t, p).start()

                @pl.when(not_last)
                def _():
                    for c in loads(t + 2, p):
                        c.start()
            return carry

        lax.fori_loop(0, n_pairs, pair_body, 0)
        store(base + tpw - 2, 0).wait()
        store(base + tpw - 1, 1).wait()

    return k(side, ids, tab)


def _peer(x2d, g, w_q, sub_keys, u_pairs, v_pairs):
    h, q3 = _peer_q(x2d, g, w_q)
    ids, gates = _peer_topk(q3, sub_keys, 0, 1)
    a = _peer_sc_stage(_pack_bf16_pairs(h), ids, u_pairs, "dot")
    w = _peer_gate(a, gates)
    return _peer_sc_stage(w, ids, v_pairs, "wsum")


def kernel(x, norm_mix, norm_ffn, norm_final, mix_w_in, s5_lambda_re, s5_lambda_im, s5_log_step, s5_b_re, s5_b_im, s5_c_re, s5_c_im, s5_d, s5_w_glu, conv_dw_w, conv_dw_b, conv_ln_g, conv_ln_b, mix_w_out, attn_w_qkv, attn_b_qkv, attn_sinks, attn_w_o, peer_w_q, peer_sub_keys, peer_u, peer_v):
    bsz, seq, d = x.shape
    tabs = _s5_tables(s5_lambda_re[0], s5_lambda_im[0], s5_log_step[0], s5_b_re[0], s5_b_im[0],
                      s5_c_re[0], s5_c_im[0])

    pairs = [(_pack_bf16_pairs(peer_u[l]), _pack_bf16_pairs(peer_v[l])) for l in range(2)]
    tc = seq // PEER_CHUNKS

    def peer(xc, l):
        return _peer(xc, norm_ffn[l], peer_w_q[l], peer_sub_keys[l], *pairs[l])

    def trunk(xb):
        z = _mix_in(xb.reshape(seq, d), norm_mix[0], mix_w_in[0]).reshape(1, seq, -1)
        state = _mixer_state0()
        x1, p0 = [], []
        for c in range(PEER_CHUNKS):
            xc, state = _mixer(z, xb, state, c, PEER_CHUNKS, tabs, s5_d[0], s5_w_glu[0], conv_dw_w[0],
                               conv_dw_b[0], conv_ln_g[0], conv_ln_b[0], mix_w_out[0])
            x1.append(xc.reshape(tc, d))
            p0.append(peer(x1[-1], 0))
        x2, qkv = _qkv(jnp.concatenate(x1, axis=0), jnp.concatenate(p0, axis=0), norm_mix[1],
                       attn_w_qkv[0], attn_b_qkv[0], seq)
        x3 = _attention(qkv.reshape(1, seq, -1), x2.reshape(1, seq, d), attn_sinks[0], attn_w_o[0]).reshape(seq, d)
        p1 = jnp.concatenate([peer(x3[c * tc:(c + 1) * tc], 1) for c in range(PEER_CHUNKS)], axis=0)
        return _final_norm(x3, p1, norm_final).reshape(1, seq, d)

    return jnp.concatenate([trunk(x[b:b + 1]) for b in range(bsz)], axis=0)
```

```python
import functools

import jax
import jax.numpy as jnp
from jax import lax
from jax.experimental import pallas as pl
from jax.experimental.pallas import tpu as pltpu
from jax.experimental.pallas import tpu_sc as plsc

f32 = jnp.float32
bf16 = jnp.bfloat16
i32 = jnp.int32
u32 = jnp.uint32

D_MODEL = 1024
S5_WIDTH = 512
S5_GROUP = 16
S5_GROUPS = 32
S5_STATE = 64
S5_LANES = S5_GROUPS * S5_STATE
CONV_WIDTH = 512
CONV_K = 31
HEAD_DIM = 64
N_Q_HEADS = 16
N_KV_HEADS = 2
Q_PER_KV = 8
Q_WIDTH = N_Q_HEADS * HEAD_DIM
KV_WIDTH = 2 * N_KV_HEADS * HEAD_DIM
QKV_WIDTH = Q_WIDTH + KV_WIDTH
WINDOW = 128
ROPE_THETA = 500000.0
ROPE_DIM = 16
PEER_HEADS = 8
PEER_N_KEYS = 128
PEER_HALF = 128
PEER_TOPK = 16
N_SLOTS = PEER_HEADS * PEER_TOPK
NORM_EPS = 1e-6

LANES = 128
SUBLANES = 8
TC_VMEM_LIMIT = 48 * 1024 * 1024

SC_CORES = 2
SC_SUBCORES = 16
SC_LANES = 16
SC_WORKERS = SC_CORES * SC_SUBCORES
ROW_BATCH = 32
N_BATCH = N_SLOTS // ROW_BATCH

ROW_TILE = 512
MIX_TILE = 256
SCAN_TILE = 128
TOPK_TILE = 128
PEER_CHUNKS = 4
CONV_HALO = 32


def _tc_params(*sem):
    return pltpu.CompilerParams(dimension_semantics=sem, vmem_limit_bytes=TC_VMEM_LIMIT)


def _rms(x, g):
    return x * lax.rsqrt(jnp.mean(x * x, axis=-1, keepdims=True) + NORM_EPS) * g


def _sigmoid(x):
    return 1.0 / (1.0 + jnp.exp(-x))


def _gelu(x):
    return 0.5 * x * (1.0 + lax.erf(x * 0.7071067811865476))


def _dot(a, b):
    return jnp.dot(a, b, preferred_element_type=f32)


def _mix_in_kernel(x_ref, g_ref, w_ref, z_ref):
    h = _rms(x_ref[...], g_ref[...])
    z_ref[...] = _dot(h.astype(bf16), w_ref[...])


def _mix_in(x2d, part, n_parts, g, w):
    t_all, d = x2d.shape
    t = t_all // n_parts
    first = part * (t // ROW_TILE)
    n = w.shape[1]
    return pl.pallas_call(
        _mix_in_kernel,
        grid=(t // ROW_TILE,),
        in_specs=[pl.BlockSpec((ROW_TILE, d), lambda i: (first + i, 0)),
                  pl.BlockSpec((1, d), lambda i: (0, 0)),
                  pl.BlockSpec((d, n), lambda i: (0, 0))],
        out_specs=pl.BlockSpec((ROW_TILE, n), lambda i: (i, 0)),
        out_shape=jax.ShapeDtypeStruct((t, n), f32),
        compiler_params=_tc_params("parallel"),
        name="mix_in",
    )(x2d, g.reshape(1, d), w.astype(bf16))


def _bf16_pairs(a):
    bits = pltpu.bitcast(a.astype(bf16).astype(f32), u32)
    w = a.shape[1] // 2
    return (bits[:, :w] >> 16) | (bits[:, w:] & jnp.uint32(0xFFFF0000))


def _pack_kernel(a_ref, o_ref):
    o_ref[...] = _bf16_pairs(a_ref[...])


def _pack_bf16_pairs(tabs, layer):
    _, n, d = tabs.shape
    return pl.pallas_call(
        _pack_kernel,
        grid=(n // ROW_TILE,),
        in_specs=[pl.BlockSpec((None, ROW_TILE, d), lambda i: (layer, i, 0))],
        out_specs=pl.BlockSpec((ROW_TILE, d // 2), lambda i: (i, 0)),
        out_shape=jax.ShapeDtypeStruct((n, d // 2), u32),
        compiler_params=_tc_params("parallel"),
        name="pack_pairs",
    )(tabs)


def _peer_q_kernel(x_ref, g_ref, w_ref, h_ref, q_ref):
    h = _rms(x_ref[...], g_ref[...])
    h_ref[...] = _bf16_pairs(h)
    q = _dot(h.astype(bf16), w_ref[...])
    for c in range(2 * PEER_HEADS):
        q_ref[c] = q[:, c * PEER_HALF:(c + 1) * PEER_HALF].astype(bf16)


def _peer_q(x2d, g, w_q):
    t, d = x2d.shape
    nq = 2 * PEER_HEADS
    return pl.pallas_call(
        _peer_q_kernel,
        grid=(t // ROW_TILE,),
        in_specs=[pl.BlockSpec((ROW_TILE, d), lambda i: (i, 0)),
                  pl.BlockSpec((1, d), lambda i: (0, 0)),
                  pl.BlockSpec((d, nq * PEER_HALF), lambda i: (0, 0))],
        out_specs=[pl.BlockSpec((ROW_TILE, d // 2), lambda i: (i, 0)),
                   pl.BlockSpec((nq, ROW_TILE, PEER_HALF), lambda i: (0, i, 0))],
        out_shape=[jax.ShapeDtypeStruct((t, d // 2), u32),
                   jax.ShapeDtypeStruct((nq, t, PEER_HALF), bf16)],
        compiler_params=_tc_params("parallel"),
        name="peer_q",
    )(x2d, g.reshape(1, d), w_q.astype(bf16))


def _qkv_kernel(x_ref, r_ref, g_ref, w_ref, b_ref, cos_ref, sin_ref, xo_ref, qkv_ref):
    x = x_ref[...] + r_ref[...]
    xo_ref[...] = x
    h = _rms(x, g_ref[...])
    qkv = _dot(h.astype(bf16), w_ref[...]) + b_ref[...]
    cos = cos_ref[...]
    sin = sin_ref[...]
    lane = lax.broadcasted_iota(i32, cos.shape, 1)
    low = (lane % HEAD_DIM) < (ROPE_DIM // 2)
    n_rot = (Q_WIDTH + N_KV_HEADS * HEAD_DIM) // LANES
    for c in range(QKV_WIDTH // LANES):
        t = qkv[:, c * LANES:(c + 1) * LANES]
        if c < n_rot:
            partner = jnp.where(low, pltpu.roll(t, LANES - ROPE_DIM // 2, axis=1),
                                pltpu.roll(t, ROPE_DIM // 2, axis=1))
            t = t * cos + partner * sin
        if c < Q_WIDTH // LANES:
            t = t * (HEAD_DIM ** -0.5)
        qkv_ref[:, c * LANES:(c + 1) * LANES] = t


def _rope_tables(seq):
    inv = jnp.power(ROPE_THETA, -jnp.arange(0, ROPE_DIM, 2, dtype=f32) / ROPE_DIM)
    ang = jnp.arange(seq, dtype=f32)[:, None] * inv[None, :]
    cos, sin = jnp.cos(ang), jnp.sin(ang)
    pad = HEAD_DIM - ROPE_DIM
    cos_h = jnp.concatenate([cos, cos, jnp.ones((seq, pad), f32)], axis=1)
    sin_h = jnp.concatenate([-sin, sin, jnp.zeros((seq, pad), f32)], axis=1)
    return jnp.tile(cos_h, (1, LANES // HEAD_DIM)), jnp.tile(sin_h, (1, LANES // HEAD_DIM))


def _qkv(x2d, r2d, g, w, b, seq):
    t, d = x2d.shape
    n = w.shape[1]
    cos, sin = _rope_tables(seq)
    blocks_per_seq = seq // ROW_TILE
    return pl.pallas_call(
        _qkv_kernel,
        grid=(t // ROW_TILE,),
        in_specs=[pl.BlockSpec((ROW_TILE, d), lambda i: (i, 0)),
                  pl.BlockSpec((ROW_TILE, d), lambda i: (i, 0)),
                  pl.BlockSpec((1, d), lambda i: (0, 0)),
                  pl.BlockSpec((d, n), lambda i: (0, 0)),
                  pl.BlockSpec((1, n), lambda i: (0, 0)),
                  pl.BlockSpec((ROW_TILE, LANES), lambda i: (i % blocks_per_seq, 0)),
                  pl.BlockSpec((ROW_TILE, LANES), lambda i: (i % blocks_per_seq, 0))],
        out_specs=[pl.BlockSpec((ROW_TILE, d), lambda i: (i, 0)),
                   pl.BlockSpec((ROW_TILE, n), lambda i: (i, 0))],
        out_shape=[jax.ShapeDtypeStruct((t, d), f32), jax.ShapeDtypeStruct((t, n), f32)],
        compiler_params=_tc_params("parallel"),
        name="qkv",
    )(x2d, r2d, g.reshape(1, d), w.astype(bf16), b.reshape(1, n), cos, sin)


def _final_norm_kernel(x_ref, r_ref, g_ref, *rest):
    o_ref = rest[-1]
    o_ref[...] = _rms(x_ref[...] + r_ref[...], g_ref[...])


def _final_norm(x2d, r2d, g, part, n_parts, out_prev):
    t, d = x2d.shape
    first = part * (t // ROW_TILE)
    spec = pl.BlockSpec((ROW_TILE, d), lambda i: (i, 0))
    in_specs = [spec, spec, pl.BlockSpec((1, d), lambda i: (0, 0))]
    args = [x2d, r2d, g.reshape(1, d)]
    aliases = {}
    if out_prev is not None:
        in_specs.append(pl.BlockSpec(memory_space=pl.ANY))
        args.append(out_prev)
        aliases = {3: 0}
    return pl.pallas_call(
        _final_norm_kernel,
        grid=(t // ROW_TILE,),
        in_specs=in_specs,
        out_specs=pl.BlockSpec((ROW_TILE, d), lambda i: (first + i, 0)),
        out_shape=jax.ShapeDtypeStruct((n_parts * t, d), f32),
        input_output_aliases=aliases,
        compiler_params=_tc_params("parallel"),
        name="final_norm",
    )(*args)


def _shift_rows(v, d, row):
    if d % SUBLANES == 0:
        return jnp.concatenate([jnp.zeros((d, v.shape[1]), v.dtype), v[:v.shape[0] - d]], axis=0)
    return jnp.where(row >= d, pltpu.roll(v, d, axis=0), 0.0)


def _mixer_kernel(z_ref, x_ref, apr_ref, api_ref, bre_ref, bim_ref, cre_ref, cim_ref, dvec_ref, wglu_ref,
                  dww_ref, dwb_ref, lng_ref, lnb_ref, wout_ref, sre_in, sim_in, tail_in,
                  o_ref, sre_out, sim_out, tail_out,
                  carry_re, carry_im, y_s, vbuf):
    tb = z_ref.shape[0]
    n_chunks = S5_LANES // LANES
    steps = [1 << s for s in range(SCAN_TILE.bit_length() - 1)]

    @pl.when(pl.program_id(1) == 0)
    def _():
        carry_re[...] = sre_in[...]
        carry_im[...] = sim_in[...]
        vbuf[0:CONV_HALO, :] = tail_in[...]

    u = z_ref[:, 0:S5_WIDTH]
    ub = u.astype(bf16)
    y_s[...] = u * dvec_ref[...]
    row = lax.broadcasted_iota(i32, (SCAN_TILE, LANES), 0)

    def chunk_body(j, carry):
        bu_re = _dot(ub, bre_ref[j])
        bu_im = _dot(ub, bim_ref[j])
        apr = apr_ref[j]
        api = api_ref[j]
        c_re = carry_re[j]
        c_im = carry_im[j]
        h_re_tiles, h_im_tiles = [], []
        for r in range(tb // SCAN_TILE):
            hr = bu_re[r * SCAN_TILE:(r + 1) * SCAN_TILE]
            hi = bu_im[r * SCAN_TILE:(r + 1) * SCAN_TILE]
            a_re, a_im = apr[0:1], api[0:1]
            first = row == 0
            hr = hr + jnp.where(first, a_re * c_re - a_im * c_im, 0.0)
            hi = hi + jnp.where(first, a_re * c_im + a_im * c_re, 0.0)
            for s, d in enumerate(steps):
                a_re, a_im = apr[s:s + 1], api[s:s + 1]
                sr = _shift_rows(hr, d, row)
                si = _shift_rows(hi, d, row)
                hr, hi = hr + (a_re * sr - a_im * si), hi + (a_re * si + a_im * sr)
            c_re = hr[SCAN_TILE - 1:SCAN_TILE]
            c_im = hi[SCAN_TILE - 1:SCAN_TILE]
            h_re_tiles.append(hr.astype(bf16))
            h_im_tiles.append(hi.astype(bf16))
        carry_re[j] = c_re
        carry_im[j] = c_im
        h_re = jnp.concatenate(h_re_tiles, axis=0)
        h_im = jnp.concatenate(h_im_tiles, axis=0)
        y_s[...] += _dot(h_re, cre_ref[j]) - _dot(h_im, cim_ref[j])
        return carry

    lax.fori_loop(0, n_chunks, chunk_body, 0)
    y = _gelu(y_s[...])
    y_ssm = y * _sigmoid(_dot(y.astype(bf16), wglu_ref[...]))

    v = z_ref[:, S5_WIDTH:S5_WIDTH + CONV_WIDTH] * _sigmoid(z_ref[:, S5_WIDTH + CONV_WIDTH:S5_WIDTH + 2 * CONV_WIDTH])
    vbuf[CONV_HALO:CONV_HALO + tb, :] = v
    acc = jnp.broadcast_to(dwb_ref[...], (tb, CONV_WIDTH))
    for k in range(CONV_K):
        acc = acc + dww_ref[k:k + 1, :] * vbuf[pl.ds(CONV_HALO - (CONV_K - 1) + k, tb), :]
    vbuf[0:CONV_HALO, :] = vbuf[tb:tb + CONV_HALO, :]
    mu = jnp.mean(acc, axis=-1, keepdims=True)
    cen = acc - mu
    var = jnp.mean(cen * cen, axis=-1, keepdims=True)
    yn = cen * lax.rsqrt(var + NORM_EPS) * lng_ref[...] + lnb_ref[...]
    y_conv = yn * _sigmoid(yn)

    o_ref[...] = (x_ref[...] + _dot(y_ssm.astype(bf16), wout_ref[0:S5_WIDTH, :])
                  + _dot(y_conv.astype(bf16), wout_ref[S5_WIDTH:S5_WIDTH + CONV_WIDTH, :]))

    @pl.when(pl.program_id(1) == pl.num_programs(1) - 1)
    def _():
        sre_out[...] = carry_re[...]
        sim_out[...] = carry_im[...]
        tail_out[...] = vbuf[0:CONV_HALO, :]


def _s5_tables(lam_re, lam_im, log_step, b_re, b_im, c_re, c_im):
    n_chunks = S5_LANES // LANES
    g_per_chunk = LANES // S5_STATE
    lam = lax.complex(lam_re, lam_im)
    dt = jnp.exp(log_step)[:, None]
    a_bar = jnp.exp(lam * dt)
    b_bar = ((a_bar - 1.0) / lam)[..., None] * lax.complex(b_re, b_im)
    n_steps = SCAN_TILE.bit_length() - 1
    powers = jnp.stack([jnp.exp(lam * dt * float(1 << s)) for s in range(n_steps)]
                       + [jnp.ones_like(a_bar)] * (SUBLANES - n_steps))
    powers = powers.reshape(SUBLANES, n_chunks, LANES).transpose(1, 0, 2)
    eye = jnp.eye(S5_GROUPS, dtype=f32)

    def in_mat(b):
        m = jnp.einsum('gph,gk->ghkp', b, eye).reshape(S5_WIDTH, S5_LANES)
        return m.reshape(S5_WIDTH, n_chunks, LANES).transpose(1, 0, 2).astype(bf16)

    def out_mat(c):
        m = jnp.einsum('ghp,gk->gpkh', c, eye).reshape(S5_LANES, S5_WIDTH)
        return m.reshape(n_chunks, LANES, S5_WIDTH).astype(bf16)

    del g_per_chunk
    return (jnp.real(powers), jnp.imag(powers), in_mat(jnp.real(b_bar)), in_mat(jnp.imag(b_bar)),
            out_mat(c_re), out_mat(c_im))


def _mixer_state0():
    n_chunks = S5_LANES // LANES
    return (jnp.zeros((n_chunks, 1, LANES), f32), jnp.zeros((n_chunks, 1, LANES), f32),
            jnp.zeros((CONV_HALO, CONV_WIDTH), f32))


def _mixer(z, x, seq_idx, state, part, n_parts, s5_tabs, d_vec, w_glu, dw_w, dw_b, ln_g, ln_b, w_out):
    bsz, seq_all, zw = z.shape
    assert bsz == 1
    seq = seq_all // n_parts
    d = x.shape[-1]
    apr, api, bre, bim, cre, cim = s5_tabs
    n_chunks = S5_LANES // LANES
    tb = MIX_TILE
    first = part * (seq // tb)
    const2 = lambda b, n: (0, 0)
    const3 = lambda b, n: (0, 0, 0)
    state_specs = [pl.BlockSpec((n_chunks, 1, LANES), const3), pl.BlockSpec((n_chunks, 1, LANES), const3),
                   pl.BlockSpec((CONV_HALO, CONV_WIDTH), const2)]
    out, s_re, s_im, tail = pl.pallas_call(
        _mixer_kernel,
        grid=(bsz, seq // tb),
        in_specs=[pl.BlockSpec((None, tb, zw), lambda b, n: (b, first + n, 0)),
                  pl.BlockSpec((None, tb, d), lambda b, n: (seq_idx, first + n, 0)),
                  pl.BlockSpec((n_chunks, SUBLANES, LANES), const3),
                  pl.BlockSpec((n_chunks, SUBLANES, LANES), const3),
                  pl.BlockSpec((n_chunks, S5_WIDTH, LANES), const3),
                  pl.BlockSpec((n_chunks, S5_WIDTH, LANES), const3),
                  pl.BlockSpec((n_chunks, LANES, S5_WIDTH), const3),
                  pl.BlockSpec((n_chunks, LANES, S5_WIDTH), const3),
                  pl.BlockSpec((1, S5_WIDTH), const2),
                  pl.BlockSpec((S5_WIDTH, S5_WIDTH), const2),
                  pl.BlockSpec((CONV_K, CONV_WIDTH), const2),
                  pl.BlockSpec((1, CONV_WIDTH), const2),
                  pl.BlockSpec((1, CONV_WIDTH), const2),
                  pl.BlockSpec((1, CONV_WIDTH), const2),
                  pl.BlockSpec((S5_WIDTH + CONV_WIDTH, d), const2)] + state_specs,
        out_specs=[pl.BlockSpec((None, tb, d), lambda b, n: (b, n, 0))] + state_specs,
        out_shape=[jax.ShapeDtypeStruct((bsz, seq, d), f32),
                   jax.ShapeDtypeStruct((n_chunks, 1, LANES), f32),
                   jax.ShapeDtypeStruct((n_chunks, 1, LANES), f32),
                   jax.ShapeDtypeStruct((CONV_HALO, CONV_WIDTH), f32)],
        scratch_shapes=[pltpu.VMEM((n_chunks, 1, LANES), f32),
                        pltpu.VMEM((n_chunks, 1, LANES), f32),
                        pltpu.VMEM((tb, S5_WIDTH), f32),
                        pltpu.VMEM((tb + CONV_HALO, CONV_WIDTH), f32)],
        compiler_params=_tc_params("arbitrary", "arbitrary"),
        name="mixer",
    )(z, x, apr, api, bre, bim, cre, cim, d_vec.reshape(1, S5_WIDTH), w_glu.astype(bf16),
      dw_w, dw_b.reshape(1, -1), ln_g.reshape(1, -1), ln_b.reshape(1, -1), w_out.astype(bf16), *state)
    return out, (s_re, s_im, tail)


def _attn_kernel(q_ref, kvc_ref, kvp_ref, x_ref, sink_ref, wo_ref, o_ref, o_s):
    n = pl.program_id(1)
    qi = lax.broadcasted_iota(i32, (WINDOW, 2 * WINDOW), 0)
    si = lax.broadcasted_iota(i32, (WINDOW, 2 * WINDOW), 1)
    first_key = jnp.where(n > 0, 0, WINDOW)
    valid = (si > qi) & (si <= qi + WINDOW) & (si >= first_key)
    kv = jnp.concatenate([kvp_ref[...], kvc_ref[...]], axis=0).astype(bf16)
    for kh in range(N_KV_HEADS):
        k = kv[:, kh * HEAD_DIM:(kh + 1) * HEAD_DIM]
        v = kv[:, (N_KV_HEADS + kh) * HEAD_DIM:(N_KV_HEADS + kh + 1) * HEAD_DIM]
        for g in range(Q_PER_KV):
            h = kh * Q_PER_KV + g
            q = q_ref[:, h * HEAD_DIM:(h + 1) * HEAD_DIM].astype(bf16)
            s = lax.dot_general(q, k, (((1,), (1,)), ((), ())), preferred_element_type=f32)
            s = jnp.where(valid, s, -jnp.inf)
            sink = sink_ref[0:1, h:h + 1]
            m = jnp.maximum(jnp.max(s, axis=-1, keepdims=True), sink)
            p = jnp.exp(s - m)
            denom = jnp.sum(p, axis=-1, keepdims=True) + jnp.exp(sink - m)
            probs = (p / denom).astype(bf16)
            o_s[:, h * HEAD_DIM:(h + 1) * HEAD_DIM] = _dot(probs, v)
    o_ref[...] = x_ref[...] + _dot(o_s[...].astype(bf16), wo_ref[...])


def _attention(qkv, x, sinks, w_o):
    bsz, seq, _ = qkv.shape
    d = x.shape[-1]
    kv_block = Q_WIDTH // KV_WIDTH
    return pl.pallas_call(
        _attn_kernel,
        grid=(bsz, seq // WINDOW),
        in_specs=[pl.BlockSpec((None, WINDOW, Q_WIDTH), lambda b, n: (b, n, 0)),
                  pl.BlockSpec((None, WINDOW, KV_WIDTH), lambda b, n: (b, n, kv_block)),
                  pl.BlockSpec((None, WINDOW, KV_WIDTH), lambda b, n: (b, jnp.maximum(n - 1, 0), kv_block)),
                  pl.BlockSpec((None, WINDOW, d), lambda b, n: (b, n, 0)),
                  pl.BlockSpec((1, N_Q_HEADS), lambda b, n: (0, 0)),
                  pl.BlockSpec((Q_WIDTH, d), lambda b, n: (0, 0))],
        out_specs=pl.BlockSpec((None, WINDOW, d), lambda b, n: (b, n, 0)),
        out_shape=jax.ShapeDtypeStruct((bsz, seq, d), f32),
        scratch_shapes=[pltpu.VMEM((WINDOW, Q_WIDTH), f32)],
        compiler_params=_tc_params("parallel", "parallel"),
        name="attention",
    )(qkv, qkv, qkv, x, sinks.reshape(1, N_Q_HEADS), w_o.astype(bf16))


def _top16(s, ids=None):
    m_rows = s.shape[0]
    pos = lax.broadcasted_iota(i32, s.shape, 0).astype(f32)
    vals, outs = [], []
    for _ in range(PEER_TOPK):
        mx = jnp.max(s, axis=0, keepdims=True)
        first = jnp.min(jnp.where(s == mx, pos, float(m_rows)), axis=0, keepdims=True)
        hit = pos == first
        vals.append(mx)
        outs.append(first if ids is None else jnp.sum(jnp.where(hit, ids, 0.0), axis=0, keepdims=True))
        s = jnp.where(hit, -jnp.inf, s)
    return jnp.concatenate(vals, axis=0), jnp.concatenate(outs, axis=0)


def _pair_candidates(a0, a1, combine, fill):
    n = a0.shape[1]
    sub = lax.broadcasted_iota(i32, (SUBLANES, n), 0)
    pieces = [combine(a0[0:1], a1)]
    for i in range(1, SUBLANES):
        keep = PEER_TOPK // (i + 1)
        pieces.append(jnp.where(sub < keep, combine(a0[i:i + 1], a1[0:SUBLANES]), fill))
    pieces.append(combine(a0[SUBLANES:PEER_TOPK], a1[0:1]))
    return jnp.concatenate(pieces, axis=0)


def _topk_kernel(q_ref, keys_ref, ids_ref, gates_ref, ids_s, gates_s):
    def head_body(h, carry):
        tops = []
        for c in range(2):
            q = q_ref[2 * h + c]
            s = lax.dot_general(keys_ref[2 * h + c], q, (((1,), (1,)), ((), ())),
                                preferred_element_type=f32)
            tops.append(_top16(s))
        (s0, i0), (s1, i1) = tops
        cand_s = _pair_candidates(s0, s1, lambda a, b: a + b, -jnp.inf)
        cand_i = _pair_candidates(i0, i1, lambda a, b: a * float(PEER_N_KEYS) + b, 0.0)
        best_s, best_i = _top16(cand_s, cand_i)
        e = jnp.exp(best_s - best_s[0:1])
        off = pl.multiple_of(h * PEER_TOPK, PEER_TOPK)
        gates_s[pl.ds(off, PEER_TOPK), :] = e / jnp.sum(e, axis=0, keepdims=True)
        ids_s[pl.ds(off, PEER_TOPK), :] = best_i.astype(i32)
        return carry

    lax.fori_loop(0, PEER_HEADS, head_body, 0, unroll=4)
    ids_ref[...] = ids_s[...].T
    gates_ref[...] = gates_s[...].T


def _peer_topk(q3, sub_keys, chunk, n_chunks):
    nq, t_all, half = q3.shape
    keys = sub_keys.reshape(nq, PEER_N_KEYS, half).astype(bf16)
    tb = TOPK_TILE
    t = t_all // n_chunks
    first = chunk * (t // tb)
    return pl.pallas_call(
        _topk_kernel,
        grid=(t // tb,),
        in_specs=[pl.BlockSpec((nq, tb, half), lambda i: (0, first + i, 0)),
                  pl.BlockSpec((nq, PEER_N_KEYS, half), lambda i: (0, 0, 0))],
        out_specs=[pl.BlockSpec((tb, N_SLOTS), lambda i: (i, 0)),
                   pl.BlockSpec((tb, N_SLOTS), lambda i: (i, 0))],
        out_shape=[jax.ShapeDtypeStruct((t, N_SLOTS), i32), jax.ShapeDtypeStruct((t, N_SLOTS), f32)],
        scratch_shapes=[pltpu.VMEM((N_SLOTS, tb), i32), pltpu.VMEM((N_SLOTS, tb), f32)],
        compiler_params=_tc_params("parallel"),
        name="peer_topk",
    )(q3, keys)


def _gate_kernel(a_ref, g_ref, w_ref):
    w_ref[...] = _gelu(a_ref[...]) * g_ref[...]


def _peer_gate(a, gates):
    t, n = a.shape
    tm = 2048
    spec = pl.BlockSpec((tm, n), lambda i: (i, 0))
    return pl.pallas_call(
        _gate_kernel, grid=(t // tm,), in_specs=[spec, spec], out_specs=spec,
        out_shape=jax.ShapeDtypeStruct((t, n), f32),
        compiler_params=_tc_params("parallel"), name="peer_gate",
    )(a, gates)


def _sc_params():
    return pltpu.CompilerParams(needs_layout_passes=False)


def _lane_permute(x, idx):
    return x.at[idx].get(mode="promise_in_bounds")


def _halves_f32(pairs_bf16):
    word = plsc.bitcast(pairs_bf16, u32)
    return plsc.bitcast(word << 16, f32), plsc.bitcast(word & jnp.uint32(0xFFFF0000), f32)


def _peer_sc_stage(side, ids, tab, mode):
    t_total, side_w = side.shape
    dw = tab.shape[1]
    out_w = N_SLOTS if mode == "dot" else 2 * dw
    tpw = t_total // SC_WORKERS
    n_pairs = tpw // 2
    assert tpw * SC_WORKERS == t_total and n_pairs * 2 == tpw and N_BATCH >= 2
    n_grp = ROW_BATCH // SC_LANES
    mesh = plsc.VectorSubcoreMesh(core_axis_name="c", subcore_axis_name="s")

    @functools.partial(
        pl.kernel, mesh=mesh,
        out_type=jax.ShapeDtypeStruct((t_total, out_w), f32),
        scratch_types=[
            pltpu.VMEM((N_SLOTS,), i32), pltpu.VMEM((N_SLOTS,), i32),
            pltpu.VMEM((side_w,), side.dtype), pltpu.VMEM((side_w,), side.dtype),
            pltpu.VMEM((out_w,), f32), pltpu.VMEM((out_w,), f32),
            pltpu.VMEM((N_BATCH, ROW_BATCH, dw), u32),
            pltpu.SemaphoreType.DMA((N_BATCH,)), pltpu.SemaphoreType.DMA((2,)),
            pltpu.SemaphoreType.DMA((2,)), pltpu.SemaphoreType.DMA((2,)),
        ],
        compiler_params=_sc_params())
    def k(side_hbm, ids_hbm, tab_hbm, out_hbm, idx0, idx1, side0, side1, out0, out1, rows_v,
          sem_g, sem_i, sem_s, sem_o):
        idx_v, side_v, out_v = (idx0, idx1), (side0, side1), (out0, out1)
        wid = lax.axis_index("s") * SC_CORES + lax.axis_index("c")
        base = wid * tpw
        iota = lax.iota(i32, SC_LANES)

        def row_pairs(slot, r, off):
            return plsc.bitcast(rows_v[slot, r, pl.ds(off, SC_LANES)], bf16)

        def gather(p, b):
            first = b * ROW_BATCH if isinstance(b, int) else pl.multiple_of(b * ROW_BATCH, ROW_BATCH)
            return pltpu.make_async_copy(
                tab_hbm.at[idx_v[p].at[pl.ds(first, ROW_BATCH)]], rows_v.at[b], sem_g.at[b])

        def loads(t, p):
            return (pltpu.make_async_copy(ids_hbm.at[t], idx_v[p], sem_i.at[p]),
                    pltpu.make_async_copy(side_hbm.at[t], side_v[p], sem_s.at[p]))

        def store(t, p):
            return pltpu.make_async_copy(out_v[p], out_hbm.at[t], sem_o.at[p])

        def compute_dot(p, b):
            def group(g, carry):
                r0 = pl.multiple_of(g * SC_LANES, SC_LANES)

                def cbody(c, accs):
                    off = pl.multiple_of(c * LANES, LANES)
                    n_q = LANES // SC_LANES
                    xs = [plsc.bitcast(side_v[p][pl.ds(off + q * SC_LANES, SC_LANES)], bf16) for q in range(n_q)]
                    out = []
                    for j in range(SC_LANES):
                        t = [row_pairs(b, r0 + j, off + q * SC_LANES) * xs[q] for q in range(n_q)]
                        lo0, hi0 = _halves_f32((t[0] + t[1]) + (t[2] + t[3]))
                        lo1, hi1 = _halves_f32((t[4] + t[5]) + (t[6] + t[7]))
                        out.append(accs[j] + ((lo0 + hi0) + (lo1 + hi1)))
                    return tuple(out)
                accs = lax.fori_loop(0, dw // LANES, cbody,
                                     tuple(jnp.zeros((SC_LANES,), f32) for _ in range(SC_LANES)))
                vs = list(accs)
                dist = SC_LANES // 2
                while dist >= 1:
                    keep = (iota & dist) == 0
                    swap = iota ^ dist
                    vs = [jnp.where(keep, vs[k], vs[k + dist]) + _lane_permute(jnp.where(keep, vs[k + dist], vs[k]), swap)
                          for k in range(dist)]
                    dist //= 2
                out_v[p][pl.ds(pl.multiple_of(b * ROW_BATCH + r0, SC_LANES), SC_LANES)] = vs[0]
                return carry

            lax.fori_loop(0, n_grp, group, 0)

        def compute_wsum(p, b):
            def group(g, carry):
                r0 = pl.multiple_of(g * SC_LANES, SC_LANES)
                wv = side_v[p][pl.ds(pl.multiple_of(b * ROW_BATCH + r0, SC_LANES), SC_LANES)]
                sp = []
                for j in range(SC_LANES):
                    w = _lane_permute(wv, jnp.full((SC_LANES,), j, i32))
                    sp.append(plsc.pack(w, w, format=plsc.PackFormat.INTERLEAVED))

                @plsc.parallel_loop(0, dw, step=SC_LANES)
                def _(c):
                    off = pl.multiple_of(c, SC_LANES)
                    t = [row_pairs(b, r0 + j, off) * sp[j] for j in range(SC_LANES)]
                    quads = [_halves_f32((t[q] + t[q + 1]) + (t[q + 2] + t[q + 3])) for q in range(0, SC_LANES, 4)]
                    lo = (quads[0][0] + quads[1][0]) + (quads[2][0] + quads[3][0])
                    hi = (quads[0][1] + quads[1][1]) + (quads[2][1] + quads[3][1])
                    plsc.addupdate(out_v[p].at[pl.ds(off, SC_LANES)], lo)
                    plsc.addupdate(out_v[p].at[pl.ds(dw + off, SC_LANES)], hi)
                return carry

            lax.fori_loop(0, n_grp, group, 0)

        compute = compute_dot if mode == "dot" else compute_wsum

        for c in loads(base, 0):
            c.start()
        for c in loads(base, 0):
            c.wait()
        for b in range(N_BATCH - 1):
            gather(0, b).start()
        for c in loads(base + 1, 1):
            c.start()

        def pair_body(i2, carry):
            not_last = i2 < n_pairs - 1
            for p in (0, 1):
                t = base + 2 * i2 + p

                @pl.when(i2 > 0)
                def _():
                    store(t - 2, p).wait()

                if mode == "wsum":
                    @pl.loop(0, out_w, step=SC_LANES)
                    def _(c):
                        out_v[p][pl.ds(pl.multiple_of(c, SC_LANES), SC_LANES)] = jnp.zeros((SC_LANES,), f32)

                def batch_body(b, carry2):
                    @pl.when(b == 0)
                    def _():
                        gather(p, N_BATCH - 1).start()

                    def start_next():
                        @pl.when(b == 1)
                        def _():
                            for c in loads(t + 1, 1 - p):
                                c.wait()
                        gather(1 - p, b - 1).start()

                    pl.when(b > 0 if p == 0 else jnp.logical_and(b > 0, not_last))(start_next)
                    gather(p, b).wait()
                    compute(p, b)
                    return carry2

                lax.fori_loop(0, N_BATCH, batch_body, 0)
                store(t, p).start()

                @pl.when(not_last)
                def _():
                    for c in loads(t + 2, p):
                        c.start()
            return carry

        lax.fori_loop(0, n_pairs, pair_body, 0)
        store(base + tpw - 2, 0).wait()
        store(base + tpw - 1, 1).wait()

    return k(side, ids, tab)


def _peer(x2d, g, w_q, sub_keys, u_pairs, v_pairs):
    h_pairs, q3 = _peer_q(x2d, g, w_q)
    ids, gates = _peer_topk(q3, sub_keys, 0, 1)
    a = _peer_sc_stage(h_pairs, ids, u_pairs, "dot")
    w = _peer_gate(a, gates)
    return _peer_sc_stage(w, ids, v_pairs, "wsum")


def kernel(x, norm_mix, norm_ffn, norm_final, mix_w_in, s5_lambda_re, s5_lambda_im, s5_log_step, s5_b_re, s5_b_im, s5_c_re, s5_c_im, s5_d, s5_w_glu, conv_dw_w, conv_dw_b, conv_ln_g, conv_ln_b, mix_w_out, attn_w_qkv, attn_b_qkv, attn_sinks, attn_w_o, peer_w_q, peer_sub_keys, peer_u, peer_v):
    bsz, seq, d = x.shape
    tabs = _s5_tables(s5_lambda_re[0], s5_lambda_im[0], s5_log_step[0], s5_b_re[0], s5_b_im[0],
                      s5_c_re[0], s5_c_im[0])

    pairs = [(_pack_bf16_pairs(peer_u, l), _pack_bf16_pairs(peer_v, l)) for l in range(2)]
    tc = seq // PEER_CHUNKS
    x2d = x.reshape(bsz * seq, d)

    def peer(xc, l):
        return _peer(xc, norm_ffn[l], peer_w_q[l], peer_sub_keys[l], *pairs[l])

    def trunk(b):
        z = _mix_in(x2d, b, bsz, norm_mix[0], mix_w_in[0]).reshape(1, seq, -1)
        state = _mixer_state0()
        x1, p0 = [], []
        for c in range(PEER_CHUNKS):
            xc, state = _mixer(z, x, b, state, c, PEER_CHUNKS, tabs, s5_d[0], s5_w_glu[0], conv_dw_w[0],
                               conv_dw_b[0], conv_ln_g[0], conv_ln_b[0], mix_w_out[0])
            x1.append(xc.reshape(tc, d))
            p0.append(peer(x1[-1], 0))
        x2, qkv = _qkv(jnp.concatenate(x1, axis=0), jnp.concatenate(p0, axis=0), norm_mix[1],
                       attn_w_qkv[0], attn_b_qkv[0], seq)
        x3 = _attention(qkv.reshape(1, seq, -1), x2.reshape(1, seq, d), attn_sinks[0], attn_w_o[0]).reshape(seq, d)
        p1 = jnp.concatenate([peer(x3[c * tc:(c + 1) * tc], 1) for c in range(PEER_CHUNKS)], axis=0)
        return x3, p1

    out = None
    for b, (x3, p1) in enumerate([trunk(b) for b in range(bsz)]):
        out = _final_norm(x3, p1, norm_final, b, bsz, out)
    return out.reshape(bsz, seq, d)
```

```python
import functools

import jax
import jax.numpy as jnp
from jax import lax
from jax.experimental import pallas as pl
from jax.experimental.pallas import tpu as pltpu
from jax.experimental.pallas import tpu_sc as plsc

f32 = jnp.float32
bf16 = jnp.bfloat16
i32 = jnp.int32
u32 = jnp.uint32

D_MODEL = 1024
S5_WIDTH = 512
S5_GROUP = 16
S5_GROUPS = 32
S5_STATE = 64
S5_LANES = S5_GROUPS * S5_STATE
CONV_WIDTH = 512
CONV_K = 31
HEAD_DIM = 64
N_Q_HEADS = 16
N_KV_HEADS = 2
Q_PER_KV = 8
Q_WIDTH = N_Q_HEADS * HEAD_DIM
KV_WIDTH = 2 * N_KV_HEADS * HEAD_DIM
QKV_WIDTH = Q_WIDTH + KV_WIDTH
WINDOW = 128
ROPE_THETA = 500000.0
ROPE_DIM = 16
PEER_HEADS = 8
PEER_N_KEYS = 128
PEER_HALF = 128
PEER_TOPK = 16
N_SLOTS = PEER_HEADS * PEER_TOPK
NORM_EPS = 1e-6

LANES = 128
SUBLANES = 8
TC_VMEM_LIMIT = 48 * 1024 * 1024

SC_CORES = 2
SC_SUBCORES = 16
SC_LANES = 16
SC_WORKERS = SC_CORES * SC_SUBCORES
ROW_BATCH = 32
N_BATCH = N_SLOTS // ROW_BATCH

ROW_TILE = 512
MIX_TILE = 256
SCAN_TILE = 128
TOPK_TILE = 128
PEER_CHUNKS = 4
CONV_HALO = 32


def _tc_params(*sem):
    return pltpu.CompilerParams(dimension_semantics=sem, vmem_limit_bytes=TC_VMEM_LIMIT)


def _rms(x, g):
    return x * lax.rsqrt(jnp.mean(x * x, axis=-1, keepdims=True) + NORM_EPS) * g


def _sigmoid(x):
    return 1.0 / (1.0 + jnp.exp(-x))


def _gelu(x):
    return 0.5 * x * (1.0 + lax.erf(x * 0.7071067811865476))


def _dot(a, b):
    return jnp.dot(a, b, preferred_element_type=f32)


def _mix_in_kernel(x_ref, g_ref, w_ref, z_ref):
    h = _rms(x_ref[...], g_ref[...])
    z_ref[...] = _dot(h.astype(bf16), w_ref[...])


def _mix_in(x2d, part, n_parts, g, w):
    t_all, d = x2d.shape
    t = t_all // n_parts
    first = part * (t // ROW_TILE)
    n = w.shape[1]
    return pl.pallas_call(
        _mix_in_kernel,
        grid=(t // ROW_TILE,),
        in_specs=[pl.BlockSpec((ROW_TILE, d), lambda i: (first + i, 0)),
                  pl.BlockSpec((1, d), lambda i: (0, 0)),
                  pl.BlockSpec((d, n), lambda i: (0, 0))],
        out_specs=pl.BlockSpec((ROW_TILE, n), lambda i: (i, 0)),
        out_shape=jax.ShapeDtypeStruct((t, n), f32),
        compiler_params=_tc_params("parallel"),
        name="mix_in",
    )(x2d, g.reshape(1, d), w.astype(bf16))


def _bf16_pairs(a):
    bits = pltpu.bitcast(a.astype(bf16).astype(f32), u32)
    w = a.shape[1] // 2
    return (bits[:, :w] >> 16) | (bits[:, w:] & jnp.uint32(0xFFFF0000))


def _pack_kernel(a_ref, o_ref):
    o_ref[...] = _bf16_pairs(a_ref[...])


def _pack_bf16_pairs(tabs, layer):
    _, n, d = tabs.shape
    return pl.pallas_call(
        _pack_kernel,
        grid=(n // ROW_TILE,),
        in_specs=[pl.BlockSpec((None, ROW_TILE, d), lambda i: (layer, i, 0))],
        out_specs=pl.BlockSpec((ROW_TILE, d // 2), lambda i: (i, 0)),
        out_shape=jax.ShapeDtypeStruct((n, d // 2), u32),
        compiler_params=_tc_params("parallel"),
        name="pack_pairs",
    )(tabs)


def _peer_q_kernel(x_ref, g_ref, w_ref, h_ref, q_ref):
    h = _rms(x_ref[...], g_ref[...])
    h_ref[...] = _bf16_pairs(h)
    q = _dot(h.astype(bf16), w_ref[...])
    for c in range(2 * PEER_HEADS):
        q_ref[c] = q[:, c * PEER_HALF:(c + 1) * PEER_HALF].astype(bf16)


def _peer_q(x2d, part, n_parts, g, w_q):
    t_all, d = x2d.shape
    t = t_all // n_parts
    first = part * (t // ROW_TILE)
    nq = 2 * PEER_HEADS
    return pl.pallas_call(
        _peer_q_kernel,
        grid=(t // ROW_TILE,),
        in_specs=[pl.BlockSpec((ROW_TILE, d), lambda i: (first + i, 0)),
                  pl.BlockSpec((1, d), lambda i: (0, 0)),
                  pl.BlockSpec((d, nq * PEER_HALF), lambda i: (0, 0))],
        out_specs=[pl.BlockSpec((ROW_TILE, d // 2), lambda i: (i, 0)),
                   pl.BlockSpec((nq, ROW_TILE, PEER_HALF), lambda i: (0, i, 0))],
        out_shape=[jax.ShapeDtypeStruct((t, d // 2), u32),
                   jax.ShapeDtypeStruct((nq, t, PEER_HALF), bf16)],
        compiler_params=_tc_params("parallel"),
        name="peer_q",
    )(x2d, g.reshape(1, d), w_q.astype(bf16))


def _qkv_kernel(x_ref, r_ref, g_ref, w_ref, b_ref, cos_ref, sin_ref, xo_ref, qkv_ref):
    x = x_ref[...] + r_ref[...]
    xo_ref[...] = x
    h = _rms(x, g_ref[...])
    qkv = _dot(h.astype(bf16), w_ref[...]) + b_ref[...]
    cos = cos_ref[...]
    sin = sin_ref[...]
    lane = lax.broadcasted_iota(i32, cos.shape, 1)
    low = (lane % HEAD_DIM) < (ROPE_DIM // 2)
    n_rot = (Q_WIDTH + N_KV_HEADS * HEAD_DIM) // LANES
    for c in range(QKV_WIDTH // LANES):
        t = qkv[:, c * LANES:(c + 1) * LANES]
        if c < n_rot:
            partner = jnp.where(low, pltpu.roll(t, LANES - ROPE_DIM // 2, axis=1),
                                pltpu.roll(t, ROPE_DIM // 2, axis=1))
            t = t * cos + partner * sin
        if c < Q_WIDTH // LANES:
            t = t * (HEAD_DIM ** -0.5)
        qkv_ref[:, c * LANES:(c + 1) * LANES] = t.astype(bf16)


def _rope_tables(seq):
    inv = jnp.power(ROPE_THETA, -jnp.arange(0, ROPE_DIM, 2, dtype=f32) / ROPE_DIM)
    ang = jnp.arange(seq, dtype=f32)[:, None] * inv[None, :]
    cos, sin = jnp.cos(ang), jnp.sin(ang)
    pad = HEAD_DIM - ROPE_DIM
    cos_h = jnp.concatenate([cos, cos, jnp.ones((seq, pad), f32)], axis=1)
    sin_h = jnp.concatenate([-sin, sin, jnp.zeros((seq, pad), f32)], axis=1)
    return jnp.tile(cos_h, (1, LANES // HEAD_DIM)), jnp.tile(sin_h, (1, LANES // HEAD_DIM))


def _qkv(x2d, r2d, g, w, b, seq):
    t, d = x2d.shape
    n = w.shape[1]
    cos, sin = _rope_tables(seq)
    blocks_per_seq = seq // ROW_TILE
    return pl.pallas_call(
        _qkv_kernel,
        grid=(t // ROW_TILE,),
        in_specs=[pl.BlockSpec((ROW_TILE, d), lambda i: (i, 0)),
                  pl.BlockSpec((ROW_TILE, d), lambda i: (i, 0)),
                  pl.BlockSpec((1, d), lambda i: (0, 0)),
                  pl.BlockSpec((d, n), lambda i: (0, 0)),
                  pl.BlockSpec((1, n), lambda i: (0, 0)),
                  pl.BlockSpec((ROW_TILE, LANES), lambda i: (i % blocks_per_seq, 0)),
                  pl.BlockSpec((ROW_TILE, LANES), lambda i: (i % blocks_per_seq, 0))],
        out_specs=[pl.BlockSpec((ROW_TILE, d), lambda i: (i, 0)),
                   pl.BlockSpec((ROW_TILE, n), lambda i: (i, 0))],
        out_shape=[jax.ShapeDtypeStruct((t, d), f32), jax.ShapeDtypeStruct((t, n), bf16)],
        compiler_params=_tc_params("parallel"),
        name="qkv",
    )(x2d, r2d, g.reshape(1, d), w.astype(bf16), b.reshape(1, n), cos, sin)


def _final_norm_kernel(x_ref, r_ref, g_ref, *rest):
    o_ref = rest[-1]
    o_ref[...] = _rms(x_ref[...] + r_ref[...], g_ref[...])


def _final_norm(x2d, x_part, x_parts, r2d, g, part, n_parts, out_prev):
    t, d = r2d.shape
    assert x2d.shape[0] == x_parts * t
    first = part * (t // ROW_TILE)
    x_first = x_part * (t // ROW_TILE)
    spec = pl.BlockSpec((ROW_TILE, d), lambda i: (i, 0))
    in_specs = [pl.BlockSpec((ROW_TILE, d), lambda i: (x_first + i, 0)), spec, pl.BlockSpec((1, d), lambda i: (0, 0))]
    args = [x2d, r2d, g.reshape(1, d)]
    aliases = {}
    if out_prev is not None:
        in_specs.append(pl.BlockSpec(memory_space=pl.ANY))
        args.append(out_prev)
        aliases = {3: 0}
    return pl.pallas_call(
        _final_norm_kernel,
        grid=(t // ROW_TILE,),
        in_specs=in_specs,
        out_specs=pl.BlockSpec((ROW_TILE, d), lambda i: (first + i, 0)),
        out_shape=jax.ShapeDtypeStruct((n_parts * t, d), f32),
        input_output_aliases=aliases,
        compiler_params=_tc_params("parallel"),
        name="final_norm",
    )(*args)


def _shift_rows(v, d, row):
    if d % SUBLANES == 0:
        return jnp.concatenate([jnp.zeros((d, v.shape[1]), v.dtype), v[:v.shape[0] - d]], axis=0)
    return jnp.where(row >= d, pltpu.roll(v, d, axis=0), 0.0)


def _mixer_kernel(z_ref, x_ref, apr_ref, api_ref, bre_ref, bim_ref, cre_ref, cim_ref, dvec_ref, wglu_ref,
                  dww_ref, dwb_ref, lng_ref, lnb_ref, wout_ref, sre_in, sim_in, tail_in,
                  o_ref, sre_out, sim_out, tail_out,
                  carry_re, carry_im, y_s, vbuf):
    tb = z_ref.shape[0]
    n_chunks = S5_LANES // LANES
    steps = [1 << s for s in range(SCAN_TILE.bit_length() - 1)]

    @pl.when(pl.program_id(1) == 0)
    def _():
        carry_re[...] = sre_in[...]
        carry_im[...] = sim_in[...]
        vbuf[0:CONV_HALO, :] = tail_in[...]

    u = z_ref[:, 0:S5_WIDTH]
    ub = u.astype(bf16)
    y_s[...] = u * dvec_ref[...]
    row = lax.broadcasted_iota(i32, (SCAN_TILE, LANES), 0)

    def chunk_body(j, carry):
        bu_re = _dot(ub, bre_ref[j])
        bu_im = _dot(ub, bim_ref[j])
        apr = apr_ref[j]
        api = api_ref[j]
        c_re = carry_re[j]
        c_im = carry_im[j]
        h_re_tiles, h_im_tiles = [], []
        for r in range(tb // SCAN_TILE):
            hr = bu_re[r * SCAN_TILE:(r + 1) * SCAN_TILE]
            hi = bu_im[r * SCAN_TILE:(r + 1) * SCAN_TILE]
            a_re, a_im = apr[0:1], api[0:1]
            first = row == 0
            hr = hr + jnp.where(first, a_re * c_re - a_im * c_im, 0.0)
            hi = hi + jnp.where(first, a_re * c_im + a_im * c_re, 0.0)
            for s, d in enumerate(steps):
                a_re, a_im = apr[s:s + 1], api[s:s + 1]
                sr = _shift_rows(hr, d, row)
                si = _shift_rows(hi, d, row)
                hr, hi = hr + (a_re * sr - a_im * si), hi + (a_re * si + a_im * sr)
            c_re = hr[SCAN_TILE - 1:SCAN_TILE]
            c_im = hi[SCAN_TILE - 1:SCAN_TILE]
            h_re_tiles.append(hr.astype(bf16))
            h_im_tiles.append(hi.astype(bf16))
        carry_re[j] = c_re
        carry_im[j] = c_im
        h_re = jnp.concatenate(h_re_tiles, axis=0)
        h_im = jnp.concatenate(h_im_tiles, axis=0)
        y_s[...] += _dot(h_re, cre_ref[j]) - _dot(h_im, cim_ref[j])
        return carry

    lax.fori_loop(0, n_chunks, chunk_body, 0)
    y = _gelu(y_s[...])
    y_ssm = y * _sigmoid(_dot(y.astype(bf16), wglu_ref[...]))

    v = z_ref[:, S5_WIDTH:S5_WIDTH + CONV_WIDTH] * _sigmoid(z_ref[:, S5_WIDTH + CONV_WIDTH:S5_WIDTH + 2 * CONV_WIDTH])
    vbuf[CONV_HALO:CONV_HALO + tb, :] = v
    acc = jnp.broadcast_to(dwb_ref[...], (tb, CONV_WIDTH))
    for k in range(CONV_K):
        acc = acc + dww_ref[k:k + 1, :] * vbuf[pl.ds(CONV_HALO - (CONV_K - 1) + k, tb), :]
    vbuf[0:CONV_HALO, :] = vbuf[tb:tb + CONV_HALO, :]
    mu = jnp.mean(acc, axis=-1, keepdims=True)
    cen = acc - mu
    var = jnp.mean(cen * cen, axis=-1, keepdims=True)
    yn = cen * lax.rsqrt(var + NORM_EPS) * lng_ref[...] + lnb_ref[...]
    y_conv = yn * _sigmoid(yn)

    o_ref[...] = (x_ref[...] + _dot(y_ssm.astype(bf16), wout_ref[0:S5_WIDTH, :])
                  + _dot(y_conv.astype(bf16), wout_ref[S5_WIDTH:S5_WIDTH + CONV_WIDTH, :]))

    @pl.when(pl.program_id(1) == pl.num_programs(1) - 1)
    def _():
        sre_out[...] = carry_re[...]
        sim_out[...] = carry_im[...]
        tail_out[...] = vbuf[0:CONV_HALO, :]


def _s5_tables(lam_re, lam_im, log_step, b_re, b_im, c_re, c_im):
    n_chunks = S5_LANES // LANES
    g_per_chunk = LANES // S5_STATE
    lam = lax.complex(lam_re, lam_im)
    dt = jnp.exp(log_step)[:, None]
    a_bar = jnp.exp(lam * dt)
    b_bar = ((a_bar - 1.0) / lam)[..., None] * lax.complex(b_re, b_im)
    n_steps = SCAN_TILE.bit_length() - 1
    powers = jnp.stack([jnp.exp(lam * dt * float(1 << s)) for s in range(n_steps)]
                       + [jnp.ones_like(a_bar)] * (SUBLANES - n_steps))
    powers = powers.reshape(SUBLANES, n_chunks, LANES).transpose(1, 0, 2)
    eye = jnp.eye(S5_GROUPS, dtype=f32)

    def in_mat(b):
        m = jnp.einsum('gph,gk->ghkp', b, eye).reshape(S5_WIDTH, S5_LANES)
        return m.reshape(S5_WIDTH, n_chunks, LANES).transpose(1, 0, 2).astype(bf16)

    def out_mat(c):
        m = jnp.einsum('ghp,gk->gpkh', c, eye).reshape(S5_LANES, S5_WIDTH)
        return m.reshape(n_chunks, LANES, S5_WIDTH).astype(bf16)

    del g_per_chunk
    return (jnp.real(powers), jnp.imag(powers), in_mat(jnp.real(b_bar)), in_mat(jnp.imag(b_bar)),
            out_mat(c_re), out_mat(c_im))


def _mixer_state0():
    n_chunks = S5_LANES // LANES
    return (jnp.zeros((n_chunks, 1, LANES), f32), jnp.zeros((n_chunks, 1, LANES), f32),
            jnp.zeros((CONV_HALO, CONV_WIDTH), f32))


def _mixer(z, x, seq_idx, state, part, n_parts, s5_tabs, d_vec, w_glu, dw_w, dw_b, ln_g, ln_b, w_out):
    bsz, seq_all, zw = z.shape
    assert bsz == 1
    seq = seq_all // n_parts
    d = x.shape[-1]
    apr, api, bre, bim, cre, cim = s5_tabs
    n_chunks = S5_LANES // LANES
    tb = MIX_TILE
    first = part * (seq // tb)
    const2 = lambda b, n: (0, 0)
    const3 = lambda b, n: (0, 0, 0)
    state_specs = [pl.BlockSpec((n_chunks, 1, LANES), const3), pl.BlockSpec((n_chunks, 1, LANES), const3),
                   pl.BlockSpec((CONV_HALO, CONV_WIDTH), const2)]
    out, s_re, s_im, tail = pl.pallas_call(
        _mixer_kernel,
        grid=(bsz, seq // tb),
        in_specs=[pl.BlockSpec((None, tb, zw), lambda b, n: (b, first + n, 0)),
                  pl.BlockSpec((None, tb, d), lambda b, n: (seq_idx, first + n, 0)),
                  pl.BlockSpec((n_chunks, SUBLANES, LANES), const3),
                  pl.BlockSpec((n_chunks, SUBLANES, LANES), const3),
                  pl.BlockSpec((n_chunks, S5_WIDTH, LANES), const3),
                  pl.BlockSpec((n_chunks, S5_WIDTH, LANES), const3),
                  pl.BlockSpec((n_chunks, LANES, S5_WIDTH), const3),
                  pl.BlockSpec((n_chunks, LANES, S5_WIDTH), const3),
                  pl.BlockSpec((1, S5_WIDTH), const2),
                  pl.BlockSpec((S5_WIDTH, S5_WIDTH), const2),
                  pl.BlockSpec((CONV_K, CONV_WIDTH), const2),
                  pl.BlockSpec((1, CONV_WIDTH), const2),
                  pl.BlockSpec((1, CONV_WIDTH), const2),
                  pl.BlockSpec((1, CONV_WIDTH), const2),
                  pl.BlockSpec((S5_WIDTH + CONV_WIDTH, d), const2)] + state_specs,
        out_specs=[pl.BlockSpec((None, tb, d), lambda b, n: (b, n, 0))] + state_specs,
        out_shape=[jax.ShapeDtypeStruct((bsz, seq, d), f32),
                   jax.ShapeDtypeStruct((n_chunks, 1, LANES), f32),
                   jax.ShapeDtypeStruct((n_chunks, 1, LANES), f32),
                   jax.ShapeDtypeStruct((CONV_HALO, CONV_WIDTH), f32)],
        scratch_shapes=[pltpu.VMEM((n_chunks, 1, LANES), f32),
                        pltpu.VMEM((n_chunks, 1, LANES), f32),
                        pltpu.VMEM((tb, S5_WIDTH), f32),
                        pltpu.VMEM((tb + CONV_HALO, CONV_WIDTH), f32)],
        compiler_params=_tc_params("arbitrary", "arbitrary"),
        name="mixer",
    )(z, x, apr, api, bre, bim, cre, cim, d_vec.reshape(1, S5_WIDTH), w_glu.astype(bf16),
      dw_w, dw_b.reshape(1, -1), ln_g.reshape(1, -1), ln_b.reshape(1, -1), w_out.astype(bf16), *state)
    return out, (s_re, s_im, tail)


def _attn_kernel(q_ref, kvc_ref, kvp_ref, x_ref, sink_ref, wo_ref, o_ref, o_s):
    n = pl.program_id(1)
    qi = lax.broadcasted_iota(i32, (WINDOW, 2 * WINDOW), 0)
    si = lax.broadcasted_iota(i32, (WINDOW, 2 * WINDOW), 1)
    first_key = jnp.where(n > 0, 0, WINDOW)
    valid = (si > qi) & (si <= qi + WINDOW) & (si >= first_key)
    kv = jnp.concatenate([kvp_ref[...], kvc_ref[...]], axis=0).astype(bf16)
    for kh in range(N_KV_HEADS):
        k = kv[:, kh * HEAD_DIM:(kh + 1) * HEAD_DIM]
        v = kv[:, (N_KV_HEADS + kh) * HEAD_DIM:(N_KV_HEADS + kh + 1) * HEAD_DIM]
        for g in range(Q_PER_KV):
            h = kh * Q_PER_KV + g
            q = q_ref[:, h * HEAD_DIM:(h + 1) * HEAD_DIM].astype(bf16)
            s = lax.dot_general(q, k, (((1,), (1,)), ((), ())), preferred_element_type=f32)
            s = jnp.where(valid, s, -jnp.inf)
            sink = sink_ref[0:1, h:h + 1]
            m = jnp.maximum(jnp.max(s, axis=-1, keepdims=True), sink)
            p = jnp.exp(s - m)
            denom = jnp.sum(p, axis=-1, keepdims=True) + jnp.exp(sink - m)
            probs = (p / denom).astype(bf16)
            o_s[:, h * HEAD_DIM:(h + 1) * HEAD_DIM] = _dot(probs, v)
    o_ref[...] = x_ref[...] + _dot(o_s[...].astype(bf16), wo_ref[...])


def _attention(qkv, x, sinks, w_o):
    bsz, seq, _ = qkv.shape
    d = x.shape[-1]
    kv_block = Q_WIDTH // KV_WIDTH
    return pl.pallas_call(
        _attn_kernel,
        grid=(bsz, seq // WINDOW),
        in_specs=[pl.BlockSpec((None, WINDOW, Q_WIDTH), lambda b, n: (b, n, 0)),
                  pl.BlockSpec((None, WINDOW, KV_WIDTH), lambda b, n: (b, n, kv_block)),
                  pl.BlockSpec((None, WINDOW, KV_WIDTH), lambda b, n: (b, jnp.maximum(n - 1, 0), kv_block)),
                  pl.BlockSpec((None, WINDOW, d), lambda b, n: (b, n, 0)),
                  pl.BlockSpec((1, N_Q_HEADS), lambda b, n: (0, 0)),
                  pl.BlockSpec((Q_WIDTH, d), lambda b, n: (0, 0))],
        out_specs=pl.BlockSpec((None, WINDOW, d), lambda b, n: (b, n, 0)),
        out_shape=jax.ShapeDtypeStruct((bsz, seq, d), f32),
        scratch_shapes=[pltpu.VMEM((WINDOW, Q_WIDTH), f32)],
        compiler_params=_tc_params("parallel", "parallel"),
        name="attention",
    )(qkv, qkv, qkv, x, sinks.reshape(1, N_Q_HEADS), w_o.astype(bf16))


def _top16(s, ids=None):
    m_rows = s.shape[0]
    pos = lax.broadcasted_iota(i32, s.shape, 0).astype(f32)
    vals, outs = [], []
    for _ in range(PEER_TOPK):
        mx = jnp.max(s, axis=0, keepdims=True)
        first = jnp.min(jnp.where(s == mx, pos, float(m_rows)), axis=0, keepdims=True)
        hit = pos == first
        vals.append(mx)
        outs.append(first if ids is None else jnp.sum(jnp.where(hit, ids, 0.0), axis=0, keepdims=True))
        s = jnp.where(hit, -jnp.inf, s)
    return jnp.concatenate(vals, axis=0), jnp.concatenate(outs, axis=0)


def _pair_candidates(a0, a1, combine, fill):
    n = a0.shape[1]
    sub = lax.broadcasted_iota(i32, (SUBLANES, n), 0)
    pieces = [combine(a0[0:1], a1)]
    for i in range(1, SUBLANES):
        keep = PEER_TOPK // (i + 1)
        pieces.append(jnp.where(sub < keep, combine(a0[i:i + 1], a1[0:SUBLANES]), fill))
    pieces.append(combine(a0[SUBLANES:PEER_TOPK], a1[0:1]))
    return jnp.concatenate(pieces, axis=0)


def _topk_kernel(q_ref, keys_ref, ids_ref, gates_ref, ids_s, gates_s):
    def head_body(h, carry):
        tops = []
        for c in range(2):
            q = q_ref[2 * h + c]
            s = lax.dot_general(keys_ref[2 * h + c], q, (((1,), (1,)), ((), ())),
                                preferred_element_type=f32)
            tops.append(_top16(s))
        (s0, i0), (s1, i1) = tops
        cand_s = _pair_candidates(s0, s1, lambda a, b: a + b, -jnp.inf)
        cand_i = _pair_candidates(i0, i1, lambda a, b: a * float(PEER_N_KEYS) + b, 0.0)
        best_s, best_i = _top16(cand_s, cand_i)
        e = jnp.exp(best_s - best_s[0:1])
        off = pl.multiple_of(h * PEER_TOPK, PEER_TOPK)
        gates_s[pl.ds(off, PEER_TOPK), :] = e / jnp.sum(e, axis=0, keepdims=True)
        ids_s[pl.ds(off, PEER_TOPK), :] = best_i.astype(i32)
        return carry

    lax.fori_loop(0, PEER_HEADS, head_body, 0, unroll=4)
    ids_ref[...] = ids_s[...].T
    gates_ref[...] = gates_s[...].T


def _peer_topk(q3, sub_keys, chunk, n_chunks):
    nq, t_all, half = q3.shape
    keys = sub_keys.reshape(nq, PEER_N_KEYS, half).astype(bf16)
    tb = TOPK_TILE
    t = t_all // n_chunks
    first = chunk * (t // tb)
    return pl.pallas_call(
        _topk_kernel,
        grid=(t // tb,),
        in_specs=[pl.BlockSpec((nq, tb, half), lambda i: (0, first + i, 0)),
                  pl.BlockSpec((nq, PEER_N_KEYS, half), lambda i: (0, 0, 0))],
        out_specs=[pl.BlockSpec((tb, N_SLOTS), lambda i: (i, 0)),
                   pl.BlockSpec((tb, N_SLOTS), lambda i: (i, 0))],
        out_shape=[jax.ShapeDtypeStruct((t, N_SLOTS), i32), jax.ShapeDtypeStruct((t, N_SLOTS), f32)],
        scratch_shapes=[pltpu.VMEM((N_SLOTS, tb), i32), pltpu.VMEM((N_SLOTS, tb), f32)],
        compiler_params=_tc_params("parallel"),
        name="peer_topk",
    )(q3, keys)


def _gate_kernel(a_ref, g_ref, w_ref):
    w_ref[...] = _gelu(a_ref[...]) * g_ref[...]


def _peer_gate(a, gates):
    t, n = a.shape
    tm = 2048
    spec = pl.BlockSpec((tm, n), lambda i: (i, 0))
    return pl.pallas_call(
        _gate_kernel, grid=(t // tm,), in_specs=[spec, spec], out_specs=spec,
        out_shape=jax.ShapeDtypeStruct((t, n), f32),
        compiler_params=_tc_params("parallel"), name="peer_gate",
    )(a, gates)


def _sc_params():
    return pltpu.CompilerParams(needs_layout_passes=False)


def _lane_permute(x, idx):
    return x.at[idx].get(mode="promise_in_bounds")


def _halves_f32(pairs_bf16):
    word = plsc.bitcast(pairs_bf16, u32)
    return plsc.bitcast(word << 16, f32), plsc.bitcast(word & jnp.uint32(0xFFFF0000), f32)


def _peer_sc_stage(side, ids, tab, mode):
    t_total, side_w = side.shape
    dw = tab.shape[1]
    out_w = N_SLOTS if mode == "dot" else 2 * dw
    tpw = t_total // SC_WORKERS
    n_pairs = tpw // 2
    assert tpw * SC_WORKERS == t_total and n_pairs * 2 == tpw and N_BATCH >= 2
    n_grp = ROW_BATCH // SC_LANES
    mesh = plsc.VectorSubcoreMesh(core_axis_name="c", subcore_axis_name="s")

    @functools.partial(
        pl.kernel, mesh=mesh,
        out_type=jax.ShapeDtypeStruct((t_total, out_w), f32),
        scratch_types=[
            pltpu.VMEM((N_SLOTS,), i32), pltpu.VMEM((N_SLOTS,), i32),
            pltpu.VMEM((side_w,), side.dtype), pltpu.VMEM((side_w,), side.dtype),
            pltpu.VMEM((out_w,), f32), pltpu.VMEM((out_w,), f32),
            pltpu.VMEM((N_BATCH, ROW_BATCH, dw), u32),
            pltpu.SemaphoreType.DMA((N_BATCH,)), pltpu.SemaphoreType.DMA((2,)),
            pltpu.SemaphoreType.DMA((2,)), pltpu.SemaphoreType.DMA((2,)),
        ],
        compiler_params=_sc_params())
    def k(side_hbm, ids_hbm, tab_hbm, out_hbm, idx0, idx1, side0, side1, out0, out1, rows_v,
          sem_g, sem_i, sem_s, sem_o):
        idx_v, side_v, out_v = (idx0, idx1), (side0, side1), (out0, out1)
        wid = lax.axis_index("s") * SC_CORES + lax.axis_index("c")
        base = wid * tpw
        iota = lax.iota(i32, SC_LANES)

        def row_pairs(slot, r, off):
            return plsc.bitcast(rows_v[slot, r, pl.ds(off, SC_LANES)], bf16)

        def gather(p, b):
            first = b * ROW_BATCH if isinstance(b, int) else pl.multiple_of(b * ROW_BATCH, ROW_BATCH)
            return pltpu.make_async_copy(
                tab_hbm.at[idx_v[p].at[pl.ds(first, ROW_BATCH)]], rows_v.at[b], sem_g.at[b])

        def loads(t, p):
            return (pltpu.make_async_copy(ids_hbm.at[t], idx_v[p], sem_i.at[p]),
                    pltpu.make_async_copy(side_hbm.at[t], side_v[p], sem_s.at[p]))

        def store(t, p):
            return pltpu.make_async_copy(out_v[p], out_hbm.at[t], sem_o.at[p])

        def compute_dot(p, b):
            def group(g, carry):
                r0 = pl.multiple_of(g * SC_LANES, SC_LANES)

                def cbody(c, accs):
                    off = pl.multiple_of(c * LANES, LANES)
                    n_q = LANES // SC_LANES
                    xs = [plsc.bitcast(side_v[p][pl.ds(off + q * SC_LANES, SC_LANES)], bf16) for q in range(n_q)]
                    out = []
                    for j in range(SC_LANES):
                        t = [row_pairs(b, r0 + j, off + q * SC_LANES) * xs[q] for q in range(n_q)]
                        lo0, hi0 = _halves_f32((t[0] + t[1]) + (t[2] + t[3]))
                        lo1, hi1 = _halves_f32((t[4] + t[5]) + (t[6] + t[7]))
                        out.append(accs[j] + ((lo0 + hi0) + (lo1 + hi1)))
                    return tuple(out)
                accs = lax.fori_loop(0, dw // LANES, cbody,
                                     tuple(jnp.zeros((SC_LANES,), f32) for _ in range(SC_LANES)))
                vs = list(accs)
                dist = SC_LANES // 2
                while dist >= 1:
                    keep = (iota & dist) == 0
                    swap = iota ^ dist
                    vs = [jnp.where(keep, vs[k], vs[k + dist]) + _lane_permute(jnp.where(keep, vs[k + dist], vs[k]), swap)
                          for k in range(dist)]
                    dist //= 2
                out_v[p][pl.ds(pl.multiple_of(b * ROW_BATCH + r0, SC_LANES), SC_LANES)] = vs[0]
                return carry

            lax.fori_loop(0, n_grp, group, 0)

        def compute_wsum(p, b):
            def group(g, carry):
                r0 = pl.multiple_of(g * SC_LANES, SC_LANES)
                wv = side_v[p][pl.ds(pl.multiple_of(b * ROW_BATCH + r0, SC_LANES), SC_LANES)]
                sp = []
                for j in range(SC_LANES):
                    w = _lane_permute(wv, jnp.full((SC_LANES,), j, i32))
                    sp.append(plsc.pack(w, w, format=plsc.PackFormat.INTERLEAVED))

                @plsc.parallel_loop(0, dw, step=SC_LANES)
                def _(c):
                    off = pl.multiple_of(c, SC_LANES)
                    t = [row_pairs(b, r0 + j, off) * sp[j] for j in range(SC_LANES)]
                    quads = [_halves_f32((t[q] + t[q + 1]) + (t[q + 2] + t[q + 3])) for q in range(0, SC_LANES, 4)]
                    lo = (quads[0][0] + quads[1][0]) + (quads[2][0] + quads[3][0])
                    hi = (quads[0][1] + quads[1][1]) + (quads[2][1] + quads[3][1])
                    plsc.addupdate(out_v[p].at[pl.ds(off, SC_LANES)], lo)
                    plsc.addupdate(out_v[p].at[pl.ds(dw + off, SC_LANES)], hi)
                return carry

            lax.fori_loop(0, n_grp, group, 0)

        compute = compute_dot if mode == "dot" else compute_wsum

        for c in loads(base, 0):
            c.start()
        for c in loads(base, 0):
            c.wait()
        for b in range(N_BATCH - 1):
            gather(0, b).start()
        for c in loads(base + 1, 1):
            c.start()

        def pair_body(i2, carry):
            not_last = i2 < n_pairs - 1
            for p in (0, 1):
                t = base + 2 * i2 + p

                @pl.when(i2 > 0)
                def _():
                    store(t - 2, p).wait()

                if mode == "wsum":
                    @pl.loop(0, out_w, step=SC_LANES)
                    def _(c):
                        out_v[p][pl.ds(pl.multiple_of(c, SC_LANES), SC_LANES)] = jnp.zeros((SC_LANES,), f32)

                def batch_body(b, carry2):
                    @pl.when(b == 0)
                    def _():
                        gather(p, N_BATCH - 1).start()

                    def start_next():
                        @pl.when(b == 1)
                        def _():
                            for c in loads(t + 1, 1 - p):
                                c.wait()
                        gather(1 - p, b - 1).start()

                    pl.when(b > 0 if p == 0 else jnp.logical_and(b > 0, not_last))(start_next)
                    gather(p, b).wait()
                    compute(p, b)
                    return carry2

                lax.fori_loop(0, N_BATCH, batch_body, 0)
                store(t, p).start()

                @pl.when(not_last)
                def _():
                    for c in loads(t + 2, p):
                        c.start()
            return carry

        lax.fori_loop(0, n_pairs, pair_body, 0)
        store(base + tpw - 2, 0).wait()
        store(base + tpw - 1, 1).wait()

    return k(side, ids, tab)


def _peer(x2d, part, n_parts, g, w_q, sub_keys, u_pairs, v_pairs):
    h_pairs, q3 = _peer_q(x2d, part, n_parts, g, w_q)
    ids, gates = _peer_topk(q3, sub_keys, 0, 1)
    a = _peer_sc_stage(h_pairs, ids, u_pairs, "dot")
    w = _peer_gate(a, gates)
    return _peer_sc_stage(w, ids, v_pairs, "wsum")


def kernel(x, norm_mix, norm_ffn, norm_final, mix_w_in, s5_lambda_re, s5_lambda_im, s5_log_step, s5_b_re, s5_b_im, s5_c_re, s5_c_im, s5_d, s5_w_glu, conv_dw_w, conv_dw_b, conv_ln_g, conv_ln_b, mix_w_out, attn_w_qkv, attn_b_qkv, attn_sinks, attn_w_o, peer_w_q, peer_sub_keys, peer_u, peer_v):
    bsz, seq, d = x.shape
    tabs = _s5_tables(s5_lambda_re[0], s5_lambda_im[0], s5_log_step[0], s5_b_re[0], s5_b_im[0],
                      s5_c_re[0], s5_c_im[0])

    pairs = [(_pack_bf16_pairs(peer_u, l), _pack_bf16_pairs(peer_v, l)) for l in range(2)]
    tc = seq // PEER_CHUNKS
    x2d = x.reshape(bsz * seq, d)

    def peer(x_rows, part, n_parts, l):
        return _peer(x_rows, part, n_parts, norm_ffn[l], peer_w_q[l], peer_sub_keys[l], *pairs[l])

    def trunk(b):
        z = _mix_in(x2d, b, bsz, norm_mix[0], mix_w_in[0]).reshape(1, seq, -1)
        state = _mixer_state0()
        x1, p0 = [], []
        for c in range(PEER_CHUNKS):
            xc, state = _mixer(z, x, b, state, c, PEER_CHUNKS, tabs, s5_d[0], s5_w_glu[0], conv_dw_w[0],
                               conv_dw_b[0], conv_ln_g[0], conv_ln_b[0], mix_w_out[0])
            x1.append(xc.reshape(tc, d))
            p0.append(peer(x1[-1], 0, 1, 0))
        x2, qkv = _qkv(jnp.concatenate(x1, axis=0), jnp.concatenate(p0, axis=0), norm_mix[1],
                       attn_w_qkv[0], attn_b_qkv[0], seq)
        x3 = _attention(qkv.reshape(1, seq, -1), x2.reshape(1, seq, d), attn_sinks[0], attn_w_o[0]).reshape(seq, d)
        return x3, [peer(x3, c, PEER_CHUNKS, 1) for c in range(PEER_CHUNKS)]

    out = None
    for b, (x3, p1) in enumerate([trunk(b) for b in range(bsz)]):
        for c in range(PEER_CHUNKS):
            out = _final_norm(x3, c, PEER_CHUNKS, p1[c], norm_final, b * PEER_CHUNKS + c, bsz * PEER_CHUNKS, out)
    return out.reshape(bsz, seq, d)
```

```python
import functools

import jax
import jax.numpy as jnp
from jax import lax
from jax.experimental import pallas as pl
from jax.experimental.pallas import tpu as pltpu
from jax.experimental.pallas import tpu_sc as plsc

f32 = jnp.float32
bf16 = jnp.bfloat16
i32 = jnp.int32
u32 = jnp.uint32

D_MODEL = 1024
S5_WIDTH = 512
S5_GROUP = 16
S5_GROUPS = 32
S5_STATE = 64
S5_LANES = S5_GROUPS * S5_STATE
CONV_WIDTH = 512
CONV_K = 31
HEAD_DIM = 64
N_Q_HEADS = 16
N_KV_HEADS = 2
Q_PER_KV = 8
Q_WIDTH = N_Q_HEADS * HEAD_DIM
KV_WIDTH = 2 * N_KV_HEADS * HEAD_DIM
QKV_WIDTH = Q_WIDTH + KV_WIDTH
WINDOW = 128
ROPE_THETA = 500000.0
ROPE_DIM = 16
PEER_HEADS = 8
PEER_N_KEYS = 128
PEER_HALF = 128
PEER_TOPK = 16
N_SLOTS = PEER_HEADS * PEER_TOPK
NORM_EPS = 1e-6

LANES = 128
SUBLANES = 8
TC_VMEM_LIMIT = 48 * 1024 * 1024

SC_CORES = 2
SC_SUBCORES = 16
SC_LANES = 16
SC_WORKERS = SC_CORES * SC_SUBCORES
ROW_BATCH = 32
N_BATCH = N_SLOTS // ROW_BATCH

ROW_TILE = 512
MIX_TILE = 256
SCAN_TILE = 128
TOPK_TILE = 128
PEER_CHUNKS = 4
FIRST_PIECE = 1024
CONV_HALO = 32


def _tc_params(*sem):
    return pltpu.CompilerParams(dimension_semantics=sem, vmem_limit_bytes=TC_VMEM_LIMIT)


def _rms(x, g):
    return x * lax.rsqrt(jnp.mean(x * x, axis=-1, keepdims=True) + NORM_EPS) * g


def _sigmoid(x):
    return 1.0 / (1.0 + jnp.exp(-x))


def _gelu(x):
    return 0.5 * x * (1.0 + lax.erf(x * 0.7071067811865476))


def _dot(a, b):
    return jnp.dot(a, b, preferred_element_type=f32)


def _mix_in_kernel(x_ref, g_ref, w_ref, z_ref):
    h = _rms(x_ref[...], g_ref[...])
    z_ref[...] = _dot(h.astype(bf16), w_ref[...])


def _mix_in(x2d, part, n_parts, g, w):
    t_all, d = x2d.shape
    t = t_all // n_parts
    first = part * (t // ROW_TILE)
    n = w.shape[1]
    return pl.pallas_call(
        _mix_in_kernel,
        grid=(t // ROW_TILE,),
        in_specs=[pl.BlockSpec((ROW_TILE, d), lambda i: (first + i, 0)),
                  pl.BlockSpec((1, d), lambda i: (0, 0)),
                  pl.BlockSpec((d, n), lambda i: (0, 0))],
        out_specs=pl.BlockSpec((ROW_TILE, n), lambda i: (i, 0)),
        out_shape=jax.ShapeDtypeStruct((t, n), f32),
        compiler_params=_tc_params("parallel"),
        name="mix_in",
    )(x2d, g.reshape(1, d), w.astype(bf16))


def _bf16_pairs(a):
    bits = pltpu.bitcast(a.astype(bf16).astype(f32), u32)
    w = a.shape[1] // 2
    return (bits[:, :w] >> 16) | (bits[:, w:] & jnp.uint32(0xFFFF0000))


def _pack_kernel(a_ref, o_ref):
    o_ref[...] = _bf16_pairs(a_ref[...])


def _pack_bf16_pairs(tabs, layer):
    _, n, d = tabs.shape
    return pl.pallas_call(
        _pack_kernel,
        grid=(n // ROW_TILE,),
        in_specs=[pl.BlockSpec((None, ROW_TILE, d), lambda i: (layer, i, 0))],
        out_specs=pl.BlockSpec((ROW_TILE, d // 2), lambda i: (i, 0)),
        out_shape=jax.ShapeDtypeStruct((n, d // 2), u32),
        compiler_params=_tc_params("parallel"),
        name="pack_pairs",
    )(tabs)


def _peer_q_kernel(x_ref, g_ref, w_ref, h_ref, q_ref):
    h = _rms(x_ref[...], g_ref[...])
    h_ref[...] = _bf16_pairs(h)
    q = _dot(h.astype(bf16), w_ref[...])
    for c in range(2 * PEER_HEADS):
        q_ref[c] = q[:, c * PEER_HALF:(c + 1) * PEER_HALF].astype(bf16)


def _peer_q(x2d, row0, t, g, w_q):
    d = x2d.shape[1]
    assert row0 % ROW_TILE == 0 and t % ROW_TILE == 0
    first = row0 // ROW_TILE
    nq = 2 * PEER_HEADS
    return pl.pallas_call(
        _peer_q_kernel,
        grid=(t // ROW_TILE,),
        in_specs=[pl.BlockSpec((ROW_TILE, d), lambda i: (first + i, 0)),
                  pl.BlockSpec((1, d), lambda i: (0, 0)),
                  pl.BlockSpec((d, nq * PEER_HALF), lambda i: (0, 0))],
        out_specs=[pl.BlockSpec((ROW_TILE, d // 2), lambda i: (i, 0)),
                   pl.BlockSpec((nq, ROW_TILE, PEER_HALF), lambda i: (0, i, 0))],
        out_shape=[jax.ShapeDtypeStruct((t, d // 2), u32),
                   jax.ShapeDtypeStruct((nq, t, PEER_HALF), bf16)],
        compiler_params=_tc_params("parallel"),
        name="peer_q",
    )(x2d, g.reshape(1, d), w_q.astype(bf16))


def _qkv_kernel(x_ref, r_ref, g_ref, w_ref, b_ref, cos_ref, sin_ref, xo_ref, qkv_ref):
    x = x_ref[...] + r_ref[...]
    xo_ref[...] = x
    h = _rms(x, g_ref[...])
    qkv = _dot(h.astype(bf16), w_ref[...]) + b_ref[...]
    cos = cos_ref[...]
    sin = sin_ref[...]
    lane = lax.broadcasted_iota(i32, cos.shape, 1)
    low = (lane % HEAD_DIM) < (ROPE_DIM // 2)
    n_rot = (Q_WIDTH + N_KV_HEADS * HEAD_DIM) // LANES
    for c in range(QKV_WIDTH // LANES):
        t = qkv[:, c * LANES:(c + 1) * LANES]
        if c < n_rot:
            partner = jnp.where(low, pltpu.roll(t, LANES - ROPE_DIM // 2, axis=1),
                                pltpu.roll(t, ROPE_DIM // 2, axis=1))
            t = t * cos + partner * sin
        if c < Q_WIDTH // LANES:
            t = t * (HEAD_DIM ** -0.5)
        qkv_ref[:, c * LANES:(c + 1) * LANES] = t.astype(bf16)


def _rope_tables(seq):
    inv = jnp.power(ROPE_THETA, -jnp.arange(0, ROPE_DIM, 2, dtype=f32) / ROPE_DIM)
    ang = jnp.arange(seq, dtype=f32)[:, None] * inv[None, :]
    cos, sin = jnp.cos(ang), jnp.sin(ang)
    pad = HEAD_DIM - ROPE_DIM
    cos_h = jnp.concatenate([cos, cos, jnp.ones((seq, pad), f32)], axis=1)
    sin_h = jnp.concatenate([-sin, sin, jnp.zeros((seq, pad), f32)], axis=1)
    return jnp.tile(cos_h, (1, LANES // HEAD_DIM)), jnp.tile(sin_h, (1, LANES // HEAD_DIM))


def _qkv(x2d, r2d, g, w, b, seq):
    t, d = x2d.shape
    n = w.shape[1]
    cos, sin = _rope_tables(seq)
    blocks_per_seq = seq // ROW_TILE
    return pl.pallas_call(
        _qkv_kernel,
        grid=(t // ROW_TILE,),
        in_specs=[pl.BlockSpec((ROW_TILE, d), lambda i: (i, 0)),
                  pl.BlockSpec((ROW_TILE, d), lambda i: (i, 0)),
                  pl.BlockSpec((1, d), lambda i: (0, 0)),
                  pl.BlockSpec((d, n), lambda i: (0, 0)),
                  pl.BlockSpec((1, n), lambda i: (0, 0)),
                  pl.BlockSpec((ROW_TILE, LANES), lambda i: (i % blocks_per_seq, 0)),
                  pl.BlockSpec((ROW_TILE, LANES), lambda i: (i % blocks_per_seq, 0))],
        out_specs=[pl.BlockSpec((ROW_TILE, d), lambda i: (i, 0)),
                   pl.BlockSpec((ROW_TILE, n), lambda i: (i, 0))],
        out_shape=[jax.ShapeDtypeStruct((t, d), f32), jax.ShapeDtypeStruct((t, n), bf16)],
        compiler_params=_tc_params("parallel"),
        name="qkv",
    )(x2d, r2d, g.reshape(1, d), w.astype(bf16), b.reshape(1, n), cos, sin)


def _final_norm_kernel(x_ref, r_ref, g_ref, *rest):
    o_ref = rest[-1]
    o_ref[...] = _rms(x_ref[...] + r_ref[...], g_ref[...])


def _final_norm(x2d, x_part, x_parts, r2d, g, part, n_parts, out_prev):
    t, d = r2d.shape
    assert x2d.shape[0] == x_parts * t
    first = part * (t // ROW_TILE)
    x_first = x_part * (t // ROW_TILE)
    spec = pl.BlockSpec((ROW_TILE, d), lambda i: (i, 0))
    in_specs = [pl.BlockSpec((ROW_TILE, d), lambda i: (x_first + i, 0)), spec, pl.BlockSpec((1, d), lambda i: (0, 0))]
    args = [x2d, r2d, g.reshape(1, d)]
    aliases = {}
    if out_prev is not None:
        in_specs.append(pl.BlockSpec(memory_space=pl.ANY))
        args.append(out_prev)
        aliases = {3: 0}
    return pl.pallas_call(
        _final_norm_kernel,
        grid=(t // ROW_TILE,),
        in_specs=in_specs,
        out_specs=pl.BlockSpec((ROW_TILE, d), lambda i: (first + i, 0)),
        out_shape=jax.ShapeDtypeStruct((n_parts * t, d), f32),
        input_output_aliases=aliases,
        compiler_params=_tc_params("parallel"),
        name="final_norm",
    )(*args)


def _shift_rows(v, d, row):
    if d % SUBLANES == 0:
        return jnp.concatenate([jnp.zeros((d, v.shape[1]), v.dtype), v[:v.shape[0] - d]], axis=0)
    return jnp.where(row >= d, pltpu.roll(v, d, axis=0), 0.0)


def _mixer_kernel(z_ref, x_ref, apr_ref, api_ref, bre_ref, bim_ref, cre_ref, cim_ref, dvec_ref, wglu_ref,
                  dww_ref, dwb_ref, lng_ref, lnb_ref, wout_ref, sre_in, sim_in, tail_in,
                  o_ref, sre_out, sim_out, tail_out,
                  carry_re, carry_im, y_s, vbuf):
    tb = z_ref.shape[0]
    n_chunks = S5_LANES // LANES
    steps = [1 << s for s in range(SCAN_TILE.bit_length() - 1)]

    @pl.when(pl.program_id(1) == 0)
    def _():
        carry_re[...] = sre_in[...]
        carry_im[...] = sim_in[...]
        vbuf[0:CONV_HALO, :] = tail_in[...]

    u = z_ref[:, 0:S5_WIDTH]
    ub = u.astype(bf16)
    y_s[...] = u * dvec_ref[...]
    row = lax.broadcasted_iota(i32, (SCAN_TILE, LANES), 0)

    def chunk_body(j, carry):
        bu_re = _dot(ub, bre_ref[j])
        bu_im = _dot(ub, bim_ref[j])
        apr = apr_ref[j]
        api = api_ref[j]
        c_re = carry_re[j]
        c_im = carry_im[j]
        h_re_tiles, h_im_tiles = [], []
        for r in range(tb // SCAN_TILE):
            hr = bu_re[r * SCAN_TILE:(r + 1) * SCAN_TILE]
            hi = bu_im[r * SCAN_TILE:(r + 1) * SCAN_TILE]
            a_re, a_im = apr[0:1], api[0:1]
            first = row == 0
            hr = hr + jnp.where(first, a_re * c_re - a_im * c_im, 0.0)
            hi = hi + jnp.where(first, a_re * c_im + a_im * c_re, 0.0)
            for s, d in enumerate(steps):
                a_re, a_im = apr[s:s + 1], api[s:s + 1]
                sr = _shift_rows(hr, d, row)
                si = _shift_rows(hi, d, row)
                hr, hi = hr + (a_re * sr - a_im * si), hi + (a_re * si + a_im * sr)
            c_re = hr[SCAN_TILE - 1:SCAN_TILE]
            c_im = hi[SCAN_TILE - 1:SCAN_TILE]
            h_re_tiles.append(hr.astype(bf16))
            h_im_tiles.append(hi.astype(bf16))
        carry_re[j] = c_re
        carry_im[j] = c_im
        h_re = jnp.concatenate(h_re_tiles, axis=0)
        h_im = jnp.concatenate(h_im_tiles, axis=0)
        y_s[...] += _dot(h_re, cre_ref[j]) - _dot(h_im, cim_ref[j])
        return carry

    lax.fori_loop(0, n_chunks, chunk_body, 0)
    y = _gelu(y_s[...])
    y_ssm = y * _sigmoid(_dot(y.astype(bf16), wglu_ref[...]))

    v = z_ref[:, S5_WIDTH:S5_WIDTH + CONV_WIDTH] * _sigmoid(z_ref[:, S5_WIDTH + CONV_WIDTH:S5_WIDTH + 2 * CONV_WIDTH])
    vbuf[CONV_HALO:CONV_HALO + tb, :] = v
    acc = jnp.broadcast_to(dwb_ref[...], (tb, CONV_WIDTH))
    for k in range(CONV_K):
        acc = acc + dww_ref[k:k + 1, :] * vbuf[pl.ds(CONV_HALO - (CONV_K - 1) + k, tb), :]
    vbuf[0:CONV_HALO, :] = vbuf[tb:tb + CONV_HALO, :]
    mu = jnp.mean(acc, axis=-1, keepdims=True)
    cen = acc - mu
    var = jnp.mean(cen * cen, axis=-1, keepdims=True)
    yn = cen * lax.rsqrt(var + NORM_EPS) * lng_ref[...] + lnb_ref[...]
    y_conv = yn * _sigmoid(yn)

    o_ref[...] = (x_ref[...] + _dot(y_ssm.astype(bf16), wout_ref[0:S5_WIDTH, :])
                  + _dot(y_conv.astype(bf16), wout_ref[S5_WIDTH:S5_WIDTH + CONV_WIDTH, :]))

    @pl.when(pl.program_id(1) == pl.num_programs(1) - 1)
    def _():
        sre_out[...] = carry_re[...]
        sim_out[...] = carry_im[...]
        tail_out[...] = vbuf[0:CONV_HALO, :]


def _s5_tables(lam_re, lam_im, log_step, b_re, b_im, c_re, c_im):
    n_chunks = S5_LANES // LANES
    g_per_chunk = LANES // S5_STATE
    lam = lax.complex(lam_re, lam_im)
    dt = jnp.exp(log_step)[:, None]
    a_bar = jnp.exp(lam * dt)
    b_bar = ((a_bar - 1.0) / lam)[..., None] * lax.complex(b_re, b_im)
    n_steps = SCAN_TILE.bit_length() - 1
    powers = jnp.stack([jnp.exp(lam * dt * float(1 << s)) for s in range(n_steps)]
                       + [jnp.ones_like(a_bar)] * (SUBLANES - n_steps))
    powers = powers.reshape(SUBLANES, n_chunks, LANES).transpose(1, 0, 2)
    eye = jnp.eye(S5_GROUPS, dtype=f32)

    def in_mat(b):
        m = jnp.einsum('gph,gk->ghkp', b, eye).reshape(S5_WIDTH, S5_LANES)
        return m.reshape(S5_WIDTH, n_chunks, LANES).transpose(1, 0, 2).astype(bf16)

    def out_mat(c):
        m = jnp.einsum('ghp,gk->gpkh', c, eye).reshape(S5_LANES, S5_WIDTH)
        return m.reshape(n_chunks, LANES, S5_WIDTH).astype(bf16)

    del g_per_chunk
    return (jnp.real(powers), jnp.imag(powers), in_mat(jnp.real(b_bar)), in_mat(jnp.imag(b_bar)),
            out_mat(c_re), out_mat(c_im))


def _mixer_state0():
    n_chunks = S5_LANES // LANES
    return (jnp.zeros((n_chunks, 1, LANES), f32), jnp.zeros((n_chunks, 1, LANES), f32),
            jnp.zeros((CONV_HALO, CONV_WIDTH), f32))


def _mixer(z, x, seq_idx, state, part, n_parts, s5_tabs, d_vec, w_glu, dw_w, dw_b, ln_g, ln_b, w_out):
    bsz, seq_all, zw = z.shape
    assert bsz == 1
    seq = seq_all // n_parts
    d = x.shape[-1]
    apr, api, bre, bim, cre, cim = s5_tabs
    n_chunks = S5_LANES // LANES
    tb = MIX_TILE
    first = part * (seq // tb)
    const2 = lambda b, n: (0, 0)
    const3 = lambda b, n: (0, 0, 0)
    state_specs = [pl.BlockSpec((n_chunks, 1, LANES), const3), pl.BlockSpec((n_chunks, 1, LANES), const3),
                   pl.BlockSpec((CONV_HALO, CONV_WIDTH), const2)]
    out, s_re, s_im, tail = pl.pallas_call(
        _mixer_kernel,
        grid=(bsz, seq // tb),
        in_specs=[pl.BlockSpec((None, tb, zw), lambda b, n: (b, first + n, 0)),
                  pl.BlockSpec((None, tb, d), lambda b, n: (seq_idx, first + n, 0)),
                  pl.BlockSpec((n_chunks, SUBLANES, LANES), const3),
                  pl.BlockSpec((n_chunks, SUBLANES, LANES), const3),
                  pl.BlockSpec((n_chunks, S5_WIDTH, LANES), const3),
                  pl.BlockSpec((n_chunks, S5_WIDTH, LANES), const3),
                  pl.BlockSpec((n_chunks, LANES, S5_WIDTH), const3),
                  pl.BlockSpec((n_chunks, LANES, S5_WIDTH), const3),
                  pl.BlockSpec((1, S5_WIDTH), const2),
                  pl.BlockSpec((S5_WIDTH, S5_WIDTH), const2),
                  pl.BlockSpec((CONV_K, CONV_WIDTH), const2),
                  pl.BlockSpec((1, CONV_WIDTH), const2),
                  pl.BlockSpec((1, CONV_WIDTH), const2),
                  pl.BlockSpec((1, CONV_WIDTH), const2),
                  pl.BlockSpec((S5_WIDTH + CONV_WIDTH, d), const2)] + state_specs,
        out_specs=[pl.BlockSpec((None, tb, d), lambda b, n: (b, n, 0))] + state_specs,
        out_shape=[jax.ShapeDtypeStruct((bsz, seq, d), f32),
                   jax.ShapeDtypeStruct((n_chunks, 1, LANES), f32),
                   jax.ShapeDtypeStruct((n_chunks, 1, LANES), f32),
                   jax.ShapeDtypeStruct((CONV_HALO, CONV_WIDTH), f32)],
        scratch_shapes=[pltpu.VMEM((n_chunks, 1, LANES), f32),
                        pltpu.VMEM((n_chunks, 1, LANES), f32),
                        pltpu.VMEM((tb, S5_WIDTH), f32),
                        pltpu.VMEM((tb + CONV_HALO, CONV_WIDTH), f32)],
        compiler_params=_tc_params("arbitrary", "arbitrary"),
        name="mixer",
    )(z, x, apr, api, bre, bim, cre, cim, d_vec.reshape(1, S5_WIDTH), w_glu.astype(bf16),
      dw_w, dw_b.reshape(1, -1), ln_g.reshape(1, -1), ln_b.reshape(1, -1), w_out.astype(bf16), *state)
    return out, (s_re, s_im, tail)


def _attn_kernel(q_ref, kvc_ref, kvp_ref, x_ref, sink_ref, wo_ref, o_ref, o_s):
    n = pl.program_id(1)
    qi = lax.broadcasted_iota(i32, (WINDOW, 2 * WINDOW), 0)
    si = lax.broadcasted_iota(i32, (WINDOW, 2 * WINDOW), 1)
    first_key = jnp.where(n > 0, 0, WINDOW)
    valid = (si > qi) & (si <= qi + WINDOW) & (si >= first_key)
    kv = jnp.concatenate([kvp_ref[...], kvc_ref[...]], axis=0).astype(bf16)
    for kh in range(N_KV_HEADS):
        k = kv[:, kh * HEAD_DIM:(kh + 1) * HEAD_DIM]
        v = kv[:, (N_KV_HEADS + kh) * HEAD_DIM:(N_KV_HEADS + kh + 1) * HEAD_DIM]
        for g in range(Q_PER_KV):
            h = kh * Q_PER_KV + g
            q = q_ref[:, h * HEAD_DIM:(h + 1) * HEAD_DIM].astype(bf16)
            s = lax.dot_general(q, k, (((1,), (1,)), ((), ())), preferred_element_type=f32)
            s = jnp.where(valid, s, -jnp.inf)
            sink = sink_ref[0:1, h:h + 1]
            m = jnp.maximum(jnp.max(s, axis=-1, keepdims=True), sink)
            p = jnp.exp(s - m)
            denom = jnp.sum(p, axis=-1, keepdims=True) + jnp.exp(sink - m)
            probs = (p / denom).astype(bf16)
            o_s[:, h * HEAD_DIM:(h + 1) * HEAD_DIM] = _dot(probs, v)
    o_ref[...] = x_ref[...] + _dot(o_s[...].astype(bf16), wo_ref[...])


def _attention(qkv, x, sinks, w_o):
    bsz, seq, _ = qkv.shape
    d = x.shape[-1]
    kv_block = Q_WIDTH // KV_WIDTH
    return pl.pallas_call(
        _attn_kernel,
        grid=(bsz, seq // WINDOW),
        in_specs=[pl.BlockSpec((None, WINDOW, Q_WIDTH), lambda b, n: (b, n, 0)),
                  pl.BlockSpec((None, WINDOW, KV_WIDTH), lambda b, n: (b, n, kv_block)),
                  pl.BlockSpec((None, WINDOW, KV_WIDTH), lambda b, n: (b, jnp.maximum(n - 1, 0), kv_block)),
                  pl.BlockSpec((None, WINDOW, d), lambda b, n: (b, n, 0)),
                  pl.BlockSpec((1, N_Q_HEADS), lambda b, n: (0, 0)),
                  pl.BlockSpec((Q_WIDTH, d), lambda b, n: (0, 0))],
        out_specs=pl.BlockSpec((None, WINDOW, d), lambda b, n: (b, n, 0)),
        out_shape=jax.ShapeDtypeStruct((bsz, seq, d), f32),
        scratch_shapes=[pltpu.VMEM((WINDOW, Q_WIDTH), f32)],
        compiler_params=_tc_params("parallel", "parallel"),
        name="attention",
    )(qkv, qkv, qkv, x, sinks.reshape(1, N_Q_HEADS), w_o.astype(bf16))


def _top16(s, ids=None):
    m_rows = s.shape[0]
    pos = lax.broadcasted_iota(i32, s.shape, 0).astype(f32)
    vals, outs = [], []
    for _ in range(PEER_TOPK):
        mx = jnp.max(s, axis=0, keepdims=True)
        first = jnp.min(jnp.where(s == mx, pos, float(m_rows)), axis=0, keepdims=True)
        hit = pos == first
        vals.append(mx)
        outs.append(first if ids is None else jnp.sum(jnp.where(hit, ids, 0.0), axis=0, keepdims=True))
        s = jnp.where(hit, -jnp.inf, s)
    return jnp.concatenate(vals, axis=0), jnp.concatenate(outs, axis=0)


def _pair_candidates(a0, a1, combine, fill):
    n = a0.shape[1]
    sub = lax.broadcasted_iota(i32, (SUBLANES, n), 0)
    pieces = [combine(a0[0:1], a1)]
    for i in range(1, SUBLANES):
        keep = PEER_TOPK // (i + 1)
        pieces.append(jnp.where(sub < keep, combine(a0[i:i + 1], a1[0:SUBLANES]), fill))
    pieces.append(combine(a0[SUBLANES:PEER_TOPK], a1[0:1]))
    return jnp.concatenate(pieces, axis=0)


def _topk_kernel(q_ref, keys_ref, ids_ref, gates_ref, ids_s, gates_s):
    def head_body(h, carry):
        tops = []
        for c in range(2):
            q = q_ref[2 * h + c]
            s = lax.dot_general(keys_ref[2 * h + c], q, (((1,), (1,)), ((), ())),
                                preferred_element_type=f32)
            tops.append(_top16(s))
        (s0, i0), (s1, i1) = tops
        cand_s = _pair_candidates(s0, s1, lambda a, b: a + b, -jnp.inf)
        cand_i = _pair_candidates(i0, i1, lambda a, b: a * float(PEER_N_KEYS) + b, 0.0)
        best_s, best_i = _top16(cand_s, cand_i)
        e = jnp.exp(best_s - best_s[0:1])
        off = pl.multiple_of(h * PEER_TOPK, PEER_TOPK)
        gates_s[pl.ds(off, PEER_TOPK), :] = e / jnp.sum(e, axis=0, keepdims=True)
        ids_s[pl.ds(off, PEER_TOPK), :] = best_i.astype(i32)
        return carry

    lax.fori_loop(0, PEER_HEADS, head_body, 0, unroll=4)
    ids_ref[...] = ids_s[...].T
    gates_ref[...] = gates_s[...].T


def _peer_topk(q3, sub_keys, chunk, n_chunks):
    nq, t_all, half = q3.shape
    keys = sub_keys.reshape(nq, PEER_N_KEYS, half).astype(bf16)
    tb = TOPK_TILE
    t = t_all // n_chunks
    first = chunk * (t // tb)
    return pl.pallas_call(
        _topk_kernel,
        grid=(t // tb,),
        in_specs=[pl.BlockSpec((nq, tb, half), lambda i: (0, first + i, 0)),
                  pl.BlockSpec((nq, PEER_N_KEYS, half), lambda i: (0, 0, 0))],
        out_specs=[pl.BlockSpec((tb, N_SLOTS), lambda i: (i, 0)),
                   pl.BlockSpec((tb, N_SLOTS), lambda i: (i, 0))],
        out_shape=[jax.ShapeDtypeStruct((t, N_SLOTS), i32), jax.ShapeDtypeStruct((t, N_SLOTS), f32)],
        scratch_shapes=[pltpu.VMEM((N_SLOTS, tb), i32), pltpu.VMEM((N_SLOTS, tb), f32)],
        compiler_params=_tc_params("parallel"),
        name="peer_topk",
    )(q3, keys)


def _gate_kernel(a_ref, g_ref, w_ref):
    w_ref[...] = _gelu(a_ref[...]) * g_ref[...]


def _peer_gate(a, gates):
    t, n = a.shape
    tm = 2 * ROW_TILE
    assert t % tm == 0
    spec = pl.BlockSpec((tm, n), lambda i: (i, 0))
    return pl.pallas_call(
        _gate_kernel, grid=(t // tm,), in_specs=[spec, spec], out_specs=spec,
        out_shape=jax.ShapeDtypeStruct((t, n), f32),
        compiler_params=_tc_params("parallel"), name="peer_gate",
    )(a, gates)


def _sc_params():
    return pltpu.CompilerParams(needs_layout_passes=False)


def _lane_permute(x, idx):
    return x.at[idx].get(mode="promise_in_bounds")


def _halves_f32(pairs_bf16):
    word = plsc.bitcast(pairs_bf16, u32)
    return plsc.bitcast(word << 16, f32), plsc.bitcast(word & jnp.uint32(0xFFFF0000), f32)


def _peer_sc_stage(side, ids, tab, mode):
    t_total, side_w = side.shape
    dw = tab.shape[1]
    out_w = N_SLOTS if mode == "dot" else 2 * dw
    tpw = t_total // SC_WORKERS
    n_pairs = tpw // 2
    assert tpw * SC_WORKERS == t_total and n_pairs * 2 == tpw and N_BATCH >= 2
    n_grp = ROW_BATCH // SC_LANES
    mesh = plsc.VectorSubcoreMesh(core_axis_name="c", subcore_axis_name="s")

    @functools.partial(
        pl.kernel, mesh=mesh,
        out_type=jax.ShapeDtypeStruct((t_total, out_w), f32),
        scratch_types=[
            pltpu.VMEM((N_SLOTS,), i32), pltpu.VMEM((N_SLOTS,), i32),
            pltpu.VMEM((side_w,), side.dtype), pltpu.VMEM((side_w,), side.dtype),
            pltpu.VMEM((out_w,), f32), pltpu.VMEM((out_w,), f32),
            pltpu.VMEM((N_BATCH, ROW_BATCH, dw), u32),
            pltpu.SemaphoreType.DMA((N_BATCH,)), pltpu.SemaphoreType.DMA((2,)),
            pltpu.SemaphoreType.DMA((2,)), pltpu.SemaphoreType.DMA((2,)),
        ],
        compiler_params=_sc_params())
    def k(side_hbm, ids_hbm, tab_hbm, out_hbm, idx0, idx1, side0, side1, out0, out1, rows_v,
          sem_g, sem_i, sem_s, sem_o):
        idx_v, side_v, out_v = (idx0, idx1), (side0, side1), (out0, out1)
        wid = lax.axis_index("s") * SC_CORES + lax.axis_index("c")
        base = wid * tpw
        iota = lax.iota(i32, SC_LANES)

        def row_pairs(slot, r, off):
            return plsc.bitcast(rows_v[slot, r, pl.ds(off, SC_LANES)], bf16)

        def gather(p, b):
            first = b * ROW_BATCH if isinstance(b, int) else pl.multiple_of(b * ROW_BATCH, ROW_BATCH)
            return pltpu.make_async_copy(
                tab_hbm.at[idx_v[p].at[pl.ds(first, ROW_BATCH)]], rows_v.at[b], sem_g.at[b])

        def loads(t, p):
            return (pltpu.make_async_copy(ids_hbm.at[t], idx_v[p], sem_i.at[p]),
                    pltpu.make_async_copy(side_hbm.at[t], side_v[p], sem_s.at[p]))

        def store(t, p):
            return pltpu.make_async_copy(out_v[p], out_hbm.at[t], sem_o.at[p])

        def compute_dot(p, b):
            def group(g, carry):
                r0 = pl.multiple_of(g * SC_LANES, SC_LANES)

                def cbody(c, accs):
                    off = pl.multiple_of(c * LANES, LANES)
                    n_q = LANES // SC_LANES
                    xs = [plsc.bitcast(side_v[p][pl.ds(off + q * SC_LANES, SC_LANES)], bf16) for q in range(n_q)]
                    out = []
                    for j in range(SC_LANES):
                        t = [row_pairs(b, r0 + j, off + q * SC_LANES) * xs[q] for q in range(n_q)]
                        lo0, hi0 = _halves_f32((t[0] + t[1]) + (t[2] + t[3]))
                        lo1, hi1 = _halves_f32((t[4] + t[5]) + (t[6] + t[7]))
                        out.append(accs[j] + ((lo0 + hi0) + (lo1 + hi1)))
                    return tuple(out)
                accs = lax.fori_loop(0, dw // LANES, cbody,
                                     tuple(jnp.zeros((SC_LANES,), f32) for _ in range(SC_LANES)))
                vs = list(accs)
                dist = SC_LANES // 2
                while dist >= 1:
                    keep = (iota & dist) == 0
                    swap = iota ^ dist
                    vs = [jnp.where(keep, vs[k], vs[k + dist]) + _lane_permute(jnp.where(keep, vs[k + dist], vs[k]), swap)
                          for k in range(dist)]
                    dist //= 2
                out_v[p][pl.ds(pl.multiple_of(b * ROW_BATCH + r0, SC_LANES), SC_LANES)] = vs[0]
                return carry

            lax.fori_loop(0, n_grp, group, 0)

        def compute_wsum(p, b):
            def group(g, carry):
                r0 = pl.multiple_of(g * SC_LANES, SC_LANES)
                wv = side_v[p][pl.ds(pl.multiple_of(b * ROW_BATCH + r0, SC_LANES), SC_LANES)]
                sp = []
                for j in range(SC_LANES):
                    w = _lane_permute(wv, jnp.full((SC_LANES,), j, i32))
                    sp.append(plsc.pack(w, w, format=plsc.PackFormat.INTERLEAVED))

                @plsc.parallel_loop(0, dw, step=SC_LANES)
                def _(c):
                    off = pl.multiple_of(c, SC_LANES)
                    t = [row_pairs(b, r0 + j, off) * sp[j] for j in range(SC_LANES)]
                    quads = [_halves_f32((t[q] + t[q + 1]) + (t[q + 2] + t[q + 3])) for q in range(0, SC_LANES, 4)]
                    lo = (quads[0][0] + quads[1][0]) + (quads[2][0] + quads[3][0])
                    hi = (quads[0][1] + quads[1][1]) + (quads[2][1] + quads[3][1])
                    plsc.addupdate(out_v[p].at[pl.ds(off, SC_LANES)], lo)
                    plsc.addupdate(out_v[p].at[pl.ds(dw + off, SC_LANES)], hi)
                return carry

            lax.fori_loop(0, n_grp, group, 0)

        compute = compute_dot if mode == "dot" else compute_wsum

        for c in loads(base, 0):
            c.start()
        for c in loads(base, 0):
            c.wait()
        for b in range(N_BATCH - 1):
            gather(0, b).start()
        for c in loads(base + 1, 1):
            c.start()

        def pair_body(i2, carry):
            not_last = i2 < n_pairs - 1
            for p in (0, 1):
                t = base + 2 * i2 + p

                @pl.when(i2 > 0)
                def _():
                    store(t - 2, p).wait()

                if mode == "wsum":
                    @pl.loop(0, out_w, step=SC_LANES)
                    def _(c):
                        out_v[p][pl.ds(pl.multiple_of(c, SC_LANES), SC_LANES)] = jnp.zeros((SC_LANES,), f32)

                def batch_body(b, carry2):
                    @pl.when(b == 0)
                    def _():
                        gather(p, N_BATCH - 1).start()

                    def start_next():
                        @pl.when(b == 1)
                        def _():
                            for c in loads(t + 1, 1 - p):
                                c.wait()
                        gather(1 - p, b - 1).start()

                    pl.when(b > 0 if p == 0 else jnp.logical_and(b > 0, not_last))(start_next)
                    gather(p, b).wait()
                    compute(p, b)
                    return carry2

                lax.fori_loop(0, N_BATCH, batch_body, 0)
                store(t, p).start()

                @pl.when(not_last)
                def _():
                    for c in loads(t + 2, p):
                        c.start()
            return carry

        lax.fori_loop(0, n_pairs, pair_body, 0)
        store(base + tpw - 2, 0).wait()
        store(base + tpw - 1, 1).wait()

    return k(side, ids, tab)


def _peer(x2d, row0, rows, g, w_q, sub_keys, u_pairs, v_pairs):
    h_pairs, q3 = _peer_q(x2d, row0, rows, g, w_q)
    ids, gates = _peer_topk(q3, sub_keys, 0, 1)
    a = _peer_sc_stage(h_pairs, ids, u_pairs, "dot")
    w = _peer_gate(a, gates)
    return _peer_sc_stage(w, ids, v_pairs, "wsum")


def kernel(x, norm_mix, norm_ffn, norm_final, mix_w_in, s5_lambda_re, s5_lambda_im, s5_log_step, s5_b_re, s5_b_im, s5_c_re, s5_c_im, s5_d, s5_w_glu, conv_dw_w, conv_dw_b, conv_ln_g, conv_ln_b, mix_w_out, attn_w_qkv, attn_b_qkv, attn_sinks, attn_w_o, peer_w_q, peer_sub_keys, peer_u, peer_v):
    bsz, seq, d = x.shape
    tabs = _s5_tables(s5_lambda_re[0], s5_lambda_im[0], s5_log_step[0], s5_b_re[0], s5_b_im[0],
                      s5_c_re[0], s5_c_im[0])

    pairs = [(_pack_bf16_pairs(peer_u, l), _pack_bf16_pairs(peer_v, l)) for l in range(2)]
    tc = seq // PEER_CHUNKS
    x2d = x.reshape(bsz * seq, d)

    def peer(x_rows, row0, rows, l):
        return _peer(x_rows, row0, rows, norm_ffn[l], peer_w_q[l], peer_sub_keys[l], *pairs[l])

    def trunk(b):
        z = _mix_in(x2d, b, bsz, norm_mix[0], mix_w_in[0]).reshape(1, seq, -1)
        state = _mixer_state0()
        x1, p0 = [], []
        for c in range(PEER_CHUNKS):
            xc, state = _mixer(z, x, b, state, c, PEER_CHUNKS, tabs, s5_d[0], s5_w_glu[0], conv_dw_w[0],
                               conv_dw_b[0], conv_ln_g[0], conv_ln_b[0], mix_w_out[0])
            x1.append(xc.reshape(tc, d))
            if c == 0:
                p0 += [peer(x1[0], 0, FIRST_PIECE, 0), peer(x1[0], FIRST_PIECE, tc - FIRST_PIECE, 0)]
            else:
                p0.append(peer(x1[-1], 0, tc, 0))
        x2, qkv = _qkv(jnp.concatenate(x1, axis=0), jnp.concatenate(p0, axis=0), norm_mix[1],
                       attn_w_qkv[0], attn_b_qkv[0], seq)
        x3 = _attention(qkv.reshape(1, seq, -1), x2.reshape(1, seq, d), attn_sinks[0], attn_w_o[0]).reshape(seq, d)
        return x3, [peer(x3, c * tc, tc, 1) for c in range(PEER_CHUNKS)]

    out = None
    for b, (x3, p1) in enumerate([trunk(b) for b in range(bsz)]):
        for c in range(PEER_CHUNKS):
            out = _final_norm(x3, c, PEER_CHUNKS, p1[c], norm_final, b * PEER_CHUNKS + c, bsz * PEER_CHUNKS, out)
    return out.reshape(bsz, seq, d)
```

```python
import functools

import jax
import jax.numpy as jnp
from jax import lax
from jax.experimental import pallas as pl
from jax.experimental.pallas import tpu as pltpu
from jax.experimental.pallas import tpu_sc as plsc

f32 = jnp.float32
bf16 = jnp.bfloat16
i32 = jnp.int32
u32 = jnp.uint32

D_MODEL = 1024
S5_WIDTH = 512
S5_GROUP = 16
S5_GROUPS = 32
S5_STATE = 64
S5_LANES = S5_GROUPS * S5_STATE
CONV_WIDTH = 512
CONV_K = 31
HEAD_DIM = 64
N_Q_HEADS = 16
N_KV_HEADS = 2
Q_PER_KV = 8
Q_WIDTH = N_Q_HEADS * HEAD_DIM
KV_WIDTH = 2 * N_KV_HEADS * HEAD_DIM
QKV_WIDTH = Q_WIDTH + KV_WIDTH
WINDOW = 128
ROPE_THETA = 500000.0
ROPE_DIM = 16
PEER_HEADS = 8
PEER_N_KEYS = 128
PEER_HALF = 128
PEER_TOPK = 16
N_SLOTS = PEER_HEADS * PEER_TOPK
NORM_EPS = 1e-6

LANES = 128
SUBLANES = 8
TC_VMEM_LIMIT = 48 * 1024 * 1024

SC_CORES = 2
SC_SUBCORES = 16
SC_LANES = 16
SC_WORKERS = SC_CORES * SC_SUBCORES
ROW_BATCH = 32
N_BATCH = N_SLOTS // ROW_BATCH

ROW_TILE = 512
MIX_TILE = 256
SCAN_TILE = 128
TOPK_TILE = 128
PEER_CHUNKS = 4
FIRST_PIECE = 1024
CONV_HALO = 32


def _tc_params(*sem):
    return pltpu.CompilerParams(dimension_semantics=sem, vmem_limit_bytes=TC_VMEM_LIMIT)


def _rms(x, g):
    return x * lax.rsqrt(jnp.mean(x * x, axis=-1, keepdims=True) + NORM_EPS) * g


def _sigmoid(x):
    return 1.0 / (1.0 + jnp.exp(-x))


def _gelu(x):
    return 0.5 * x * (1.0 + lax.erf(x * 0.7071067811865476))


def _dot(a, b):
    return jnp.dot(a, b, preferred_element_type=f32)


def _mix_in_kernel(x_ref, g_ref, w_ref, z_ref):
    h = _rms(x_ref[...], g_ref[...])
    z_ref[...] = _dot(h.astype(bf16), w_ref[...])


def _mix_in(x2d, part, n_parts, g, w):
    t_all, d = x2d.shape
    t = t_all // n_parts
    first = part * (t // ROW_TILE)
    n = w.shape[1]
    return pl.pallas_call(
        _mix_in_kernel,
        grid=(t // ROW_TILE,),
        in_specs=[pl.BlockSpec((ROW_TILE, d), lambda i: (first + i, 0)),
                  pl.BlockSpec((1, d), lambda i: (0, 0)),
                  pl.BlockSpec((d, n), lambda i: (0, 0))],
        out_specs=pl.BlockSpec((ROW_TILE, n), lambda i: (i, 0)),
        out_shape=jax.ShapeDtypeStruct((t, n), f32),
        compiler_params=_tc_params("parallel"),
        name="mix_in",
    )(x2d, g.reshape(1, d), w.astype(bf16))


def _bf16_pairs(a):
    bits = pltpu.bitcast(a.astype(bf16).astype(f32), u32)
    w = a.shape[1] // 2
    return (bits[:, :w] >> 16) | (bits[:, w:] & jnp.uint32(0xFFFF0000))


def _pack_kernel(a_ref, o_ref):
    o_ref[...] = _bf16_pairs(a_ref[...])


def _pack_bf16_pairs(tabs, layer):
    _, n, d = tabs.shape
    return pl.pallas_call(
        _pack_kernel,
        grid=(n // ROW_TILE,),
        in_specs=[pl.BlockSpec((None, ROW_TILE, d), lambda i: (layer, i, 0))],
        out_specs=pl.BlockSpec((ROW_TILE, d // 2), lambda i: (i, 0)),
        out_shape=jax.ShapeDtypeStruct((n, d // 2), u32),
        compiler_params=_tc_params("parallel"),
        name="pack_pairs",
    )(tabs)


def _peer_q_kernel(x_ref, g_ref, w_ref, h_ref, q_ref):
    h = _rms(x_ref[...], g_ref[...])
    h_ref[...] = _bf16_pairs(h)
    q = _dot(h.astype(bf16), w_ref[...])
    for c in range(2 * PEER_HEADS):
        q_ref[c] = q[:, c * PEER_HALF:(c + 1) * PEER_HALF].astype(bf16)


def _peer_q(x2d, row0, t, g, w_q):
    d = x2d.shape[1]
    assert row0 % ROW_TILE == 0 and t % ROW_TILE == 0
    first = row0 // ROW_TILE
    nq = 2 * PEER_HEADS
    return pl.pallas_call(
        _peer_q_kernel,
        grid=(t // ROW_TILE,),
        in_specs=[pl.BlockSpec((ROW_TILE, d), lambda i: (first + i, 0)),
                  pl.BlockSpec((1, d), lambda i: (0, 0)),
                  pl.BlockSpec((d, nq * PEER_HALF), lambda i: (0, 0))],
        out_specs=[pl.BlockSpec((ROW_TILE, d // 2), lambda i: (i, 0)),
                   pl.BlockSpec((nq, ROW_TILE, PEER_HALF), lambda i: (0, i, 0))],
        out_shape=[jax.ShapeDtypeStruct((t, d // 2), u32),
                   jax.ShapeDtypeStruct((nq, t, PEER_HALF), bf16)],
        compiler_params=_tc_params("parallel"),
        name="peer_q",
    )(x2d, g.reshape(1, d), w_q.astype(bf16))


def _qkv_kernel(x_ref, r_ref, g_ref, w_ref, b_ref, cos_ref, sin_ref, xo_ref, qkv_ref):
    x = x_ref[...] + r_ref[...]
    xo_ref[...] = x
    h = _rms(x, g_ref[...])
    qkv = _dot(h.astype(bf16), w_ref[...]) + b_ref[...]
    cos = cos_ref[...]
    sin = sin_ref[...]
    lane = lax.broadcasted_iota(i32, cos.shape, 1)
    low = (lane % HEAD_DIM) < (ROPE_DIM // 2)
    n_rot = (Q_WIDTH + N_KV_HEADS * HEAD_DIM) // LANES
    for c in range(QKV_WIDTH // LANES):
        t = qkv[:, c * LANES:(c + 1) * LANES]
        if c < n_rot:
            partner = jnp.where(low, pltpu.roll(t, LANES - ROPE_DIM // 2, axis=1),
                                pltpu.roll(t, ROPE_DIM // 2, axis=1))
            t = t * cos + partner * sin
        if c < Q_WIDTH // LANES:
            t = t * (HEAD_DIM ** -0.5)
        qkv_ref[:, c * LANES:(c + 1) * LANES] = t.astype(bf16)


def _rope_tables(seq):
    inv = jnp.power(ROPE_THETA, -jnp.arange(0, ROPE_DIM, 2, dtype=f32) / ROPE_DIM)
    ang = jnp.arange(seq, dtype=f32)[:, None] * inv[None, :]
    cos, sin = jnp.cos(ang), jnp.sin(ang)
    pad = HEAD_DIM - ROPE_DIM
    cos_h = jnp.concatenate([cos, cos, jnp.ones((seq, pad), f32)], axis=1)
    sin_h = jnp.concatenate([-sin, sin, jnp.zeros((seq, pad), f32)], axis=1)
    return jnp.tile(cos_h, (1, LANES // HEAD_DIM)), jnp.tile(sin_h, (1, LANES // HEAD_DIM))


def _qkv(x2d, r2d, g, w, b, seq):
    t, d = x2d.shape
    n = w.shape[1]
    cos, sin = _rope_tables(seq)
    blocks_per_seq = seq // ROW_TILE
    return pl.pallas_call(
        _qkv_kernel,
        grid=(t // ROW_TILE,),
        in_specs=[pl.BlockSpec((ROW_TILE, d), lambda i: (i, 0)),
                  pl.BlockSpec((ROW_TILE, d), lambda i: (i, 0)),
                  pl.BlockSpec((1, d), lambda i: (0, 0)),
                  pl.BlockSpec((d, n), lambda i: (0, 0)),
                  pl.BlockSpec((1, n), lambda i: (0, 0)),
                  pl.BlockSpec((ROW_TILE, LANES), lambda i: (i % blocks_per_seq, 0)),
                  pl.BlockSpec((ROW_TILE, LANES), lambda i: (i % blocks_per_seq, 0))],
        out_specs=[pl.BlockSpec((ROW_TILE, d), lambda i: (i, 0)),
                   pl.BlockSpec((ROW_TILE, n), lambda i: (i, 0))],
        out_shape=[jax.ShapeDtypeStruct((t, d), f32), jax.ShapeDtypeStruct((t, n), bf16)],
        compiler_params=_tc_params("parallel"),
        name="qkv",
    )(x2d, r2d, g.reshape(1, d), w.astype(bf16), b.reshape(1, n), cos, sin)


def _final_norm_kernel(x_ref, r_ref, g_ref, *rest):
    o_ref = rest[-1]
    o_ref[...] = _rms(x_ref[...] + r_ref[...], g_ref[...])


def _final_norm(x2d, x_part, x_parts, r2d, g, part, n_parts, out_prev):
    t, d = r2d.shape
    assert x2d.shape[0] == x_parts * t
    first = part * (t // ROW_TILE)
    x_first = x_part * (t // ROW_TILE)
    spec = pl.BlockSpec((ROW_TILE, d), lambda i: (i, 0))
    in_specs = [pl.BlockSpec((ROW_TILE, d), lambda i: (x_first + i, 0)), spec, pl.BlockSpec((1, d), lambda i: (0, 0))]
    args = [x2d, r2d, g.reshape(1, d)]
    aliases = {}
    if out_prev is not None:
        in_specs.append(pl.BlockSpec(memory_space=pl.ANY))
        args.append(out_prev)
        aliases = {3: 0}
    return pl.pallas_call(
        _final_norm_kernel,
        grid=(t // ROW_TILE,),
        in_specs=in_specs,
        out_specs=pl.BlockSpec((ROW_TILE, d), lambda i: (first + i, 0)),
        out_shape=jax.ShapeDtypeStruct((n_parts * t, d), f32),
        input_output_aliases=aliases,
        compiler_params=_tc_params("parallel"),
        name="final_norm",
    )(*args)


def _shift_rows(v, d, row):
    if d % SUBLANES == 0:
        return jnp.concatenate([jnp.zeros((d, v.shape[1]), v.dtype), v[:v.shape[0] - d]], axis=0)
    return jnp.where(row >= d, pltpu.roll(v, d, axis=0), 0.0)


def _mixer_kernel(z_ref, x_ref, apr_ref, api_ref, bre_ref, bim_ref, cre_ref, cim_ref, dvec_ref, wglu_ref,
                  dww_ref, dwb_ref, lng_ref, lnb_ref, wout_ref, sre_in, sim_in, tail_in,
                  o_ref, sre_out, sim_out, tail_out,
                  carry_re, carry_im, y_s, vbuf):
    tb = z_ref.shape[0]
    n_chunks = S5_LANES // LANES
    steps = [1 << s for s in range(SCAN_TILE.bit_length() - 1)]

    @pl.when(pl.program_id(1) == 0)
    def _():
        carry_re[...] = sre_in[...]
        carry_im[...] = sim_in[...]
        vbuf[0:CONV_HALO, :] = tail_in[...]

    u = z_ref[:, 0:S5_WIDTH]
    ub = u.astype(bf16)
    y_s[...] = u * dvec_ref[...]
    row = lax.broadcasted_iota(i32, (SCAN_TILE, LANES), 0)

    def chunk_body(j, carry):
        bu_re = _dot(ub, bre_ref[j])
        bu_im = _dot(ub, bim_ref[j])
        apr = apr_ref[j]
        api = api_ref[j]
        c_re = carry_re[j]
        c_im = carry_im[j]
        h_re_tiles, h_im_tiles = [], []
        for r in range(tb // SCAN_TILE):
            hr = bu_re[r * SCAN_TILE:(r + 1) * SCAN_TILE]
            hi = bu_im[r * SCAN_TILE:(r + 1) * SCAN_TILE]
            a_re, a_im = apr[0:1], api[0:1]
            first = row == 0
            hr = hr + jnp.where(first, a_re * c_re - a_im * c_im, 0.0)
            hi = hi + jnp.where(first, a_re * c_im + a_im * c_re, 0.0)
            for s, d in enumerate(steps):
                a_re, a_im = apr[s:s + 1], api[s:s + 1]
                sr = _shift_rows(hr, d, row)
                si = _shift_rows(hi, d, row)
                hr, hi = hr + (a_re * sr - a_im * si), hi + (a_re * si + a_im * sr)
            c_re = hr[SCAN_TILE - 1:SCAN_TILE]
            c_im = hi[SCAN_TILE - 1:SCAN_TILE]
            h_re_tiles.append(hr.astype(bf16))
            h_im_tiles.append(hi.astype(bf16))
        carry_re[j] = c_re
        carry_im[j] = c_im
        h_re = jnp.concatenate(h_re_tiles, axis=0)
        h_im = jnp.concatenate(h_im_tiles, axis=0)
        y_s[...] += _dot(h_re, cre_ref[j]) - _dot(h_im, cim_ref[j])
        return carry

    lax.fori_loop(0, n_chunks, chunk_body, 0)
    y = _gelu(y_s[...])
    y_ssm = y * _sigmoid(_dot(y.astype(bf16), wglu_ref[...]))

    v = z_ref[:, S5_WIDTH:S5_WIDTH + CONV_WIDTH] * _sigmoid(z_ref[:, S5_WIDTH + CONV_WIDTH:S5_WIDTH + 2 * CONV_WIDTH])
    vbuf[CONV_HALO:CONV_HALO + tb, :] = v
    acc = jnp.broadcast_to(dwb_ref[...], (tb, CONV_WIDTH))
    for k in range(CONV_K):
        acc = acc + dww_ref[k:k + 1, :] * vbuf[pl.ds(CONV_HALO - (CONV_K - 1) + k, tb), :]
    vbuf[0:CONV_HALO, :] = vbuf[tb:tb + CONV_HALO, :]
    mu = jnp.mean(acc, axis=-1, keepdims=True)
    cen = acc - mu
    var = jnp.mean(cen * cen, axis=-1, keepdims=True)
    yn = cen * lax.rsqrt(var + NORM_EPS) * lng_ref[...] + lnb_ref[...]
    y_conv = yn * _sigmoid(yn)

    o_ref[...] = (x_ref[...] + _dot(y_ssm.astype(bf16), wout_ref[0:S5_WIDTH, :])
                  + _dot(y_conv.astype(bf16), wout_ref[S5_WIDTH:S5_WIDTH + CONV_WIDTH, :]))

    @pl.when(pl.program_id(1) == pl.num_programs(1) - 1)
    def _():
        sre_out[...] = carry_re[...]
        sim_out[...] = carry_im[...]
        tail_out[...] = vbuf[0:CONV_HALO, :]


def _s5_tables(lam_re, lam_im, log_step, b_re, b_im, c_re, c_im):
    n_chunks = S5_LANES // LANES
    g_per_chunk = LANES // S5_STATE
    lam = lax.complex(lam_re, lam_im)
    dt = jnp.exp(log_step)[:, None]
    a_bar = jnp.exp(lam * dt)
    b_bar = ((a_bar - 1.0) / lam)[..., None] * lax.complex(b_re, b_im)
    n_steps = SCAN_TILE.bit_length() - 1
    powers = jnp.stack([jnp.exp(lam * dt * float(1 << s)) for s in range(n_steps)]
                       + [jnp.ones_like(a_bar)] * (SUBLANES - n_steps))
    powers = powers.reshape(SUBLANES, n_chunks, LANES).transpose(1, 0, 2)
    eye = jnp.eye(S5_GROUPS, dtype=f32)

    def in_mat(b):
        m = jnp.einsum('gph,gk->ghkp', b, eye).reshape(S5_WIDTH, S5_LANES)
        return m.reshape(S5_WIDTH, n_chunks, LANES).transpose(1, 0, 2).astype(bf16)

    def out_mat(c):
        m = jnp.einsum('ghp,gk->gpkh', c, eye).reshape(S5_LANES, S5_WIDTH)
        return m.reshape(n_chunks, LANES, S5_WIDTH).astype(bf16)

    del g_per_chunk
    return (jnp.real(powers), jnp.imag(powers), in_mat(jnp.real(b_bar)), in_mat(jnp.imag(b_bar)),
            out_mat(c_re), out_mat(c_im))


def _mixer_state0():
    n_chunks = S5_LANES // LANES
    return (jnp.zeros((n_chunks, 1, LANES), f32), jnp.zeros((n_chunks, 1, LANES), f32),
            jnp.zeros((CONV_HALO, CONV_WIDTH), f32))


def _mixer(z, x, seq_idx, state, part, n_parts, s5_tabs, d_vec, w_glu, dw_w, dw_b, ln_g, ln_b, w_out):
    bsz, seq_all, zw = z.shape
    assert bsz == 1
    seq = seq_all // n_parts
    d = x.shape[-1]
    apr, api, bre, bim, cre, cim = s5_tabs
    n_chunks = S5_LANES // LANES
    tb = MIX_TILE
    first = part * (seq // tb)
    const2 = lambda b, n: (0, 0)
    const3 = lambda b, n: (0, 0, 0)
    state_specs = [pl.BlockSpec((n_chunks, 1, LANES), const3), pl.BlockSpec((n_chunks, 1, LANES), const3),
                   pl.BlockSpec((CONV_HALO, CONV_WIDTH), const2)]
    out, s_re, s_im, tail = pl.pallas_call(
        _mixer_kernel,
        grid=(bsz, seq // tb),
        in_specs=[pl.BlockSpec((None, tb, zw), lambda b, n: (b, first + n, 0)),
                  pl.BlockSpec((None, tb, d), lambda b, n: (seq_idx, first + n, 0)),
                  pl.BlockSpec((n_chunks, SUBLANES, LANES), const3),
                  pl.BlockSpec((n_chunks, SUBLANES, LANES), const3),
                  pl.BlockSpec((n_chunks, S5_WIDTH, LANES), const3),
                  pl.BlockSpec((n_chunks, S5_WIDTH, LANES), const3),
                  pl.BlockSpec((n_chunks, LANES, S5_WIDTH), const3),
                  pl.BlockSpec((n_chunks, LANES, S5_WIDTH), const3),
                  pl.BlockSpec((1, S5_WIDTH), const2),
                  pl.BlockSpec((S5_WIDTH, S5_WIDTH), const2),
                  pl.BlockSpec((CONV_K, CONV_WIDTH), const2),
                  pl.BlockSpec((1, CONV_WIDTH), const2),
                  pl.BlockSpec((1, CONV_WIDTH), const2),
                  pl.BlockSpec((1, CONV_WIDTH), const2),
                  pl.BlockSpec((S5_WIDTH + CONV_WIDTH, d), const2)] + state_specs,
        out_specs=[pl.BlockSpec((None, tb, d), lambda b, n: (b, n, 0))] + state_specs,
        out_shape=[jax.ShapeDtypeStruct((bsz, seq, d), f32),
                   jax.ShapeDtypeStruct((n_chunks, 1, LANES), f32),
                   jax.ShapeDtypeStruct((n_chunks, 1, LANES), f32),
                   jax.ShapeDtypeStruct((CONV_HALO, CONV_WIDTH), f32)],
        scratch_shapes=[pltpu.VMEM((n_chunks, 1, LANES), f32),
                        pltpu.VMEM((n_chunks, 1, LANES), f32),
                        pltpu.VMEM((tb, S5_WIDTH), f32),
                        pltpu.VMEM((tb + CONV_HALO, CONV_WIDTH), f32)],
        compiler_params=_tc_params("arbitrary", "arbitrary"),
        name="mixer",
    )(z, x, apr, api, bre, bim, cre, cim, d_vec.reshape(1, S5_WIDTH), w_glu.astype(bf16),
      dw_w, dw_b.reshape(1, -1), ln_g.reshape(1, -1), ln_b.reshape(1, -1), w_out.astype(bf16), *state)
    return out, (s_re, s_im, tail)


def _attn_kernel(q_ref, kvc_ref, kvp_ref, x_ref, sink_ref, wo_ref, o_ref, o_s):
    n = pl.program_id(1)
    qi = lax.broadcasted_iota(i32, (WINDOW, 2 * WINDOW), 0)
    si = lax.broadcasted_iota(i32, (WINDOW, 2 * WINDOW), 1)
    first_key = jnp.where(n > 0, 0, WINDOW)
    valid = (si > qi) & (si <= qi + WINDOW) & (si >= first_key)
    kv = jnp.concatenate([kvp_ref[...], kvc_ref[...]], axis=0).astype(bf16)
    for kh in range(N_KV_HEADS):
        k = kv[:, kh * HEAD_DIM:(kh + 1) * HEAD_DIM]
        v = kv[:, (N_KV_HEADS + kh) * HEAD_DIM:(N_KV_HEADS + kh + 1) * HEAD_DIM]
        for g in range(Q_PER_KV):
            h = kh * Q_PER_KV + g
            q = q_ref[:, h * HEAD_DIM:(h + 1) * HEAD_DIM].astype(bf16)
            s = lax.dot_general(q, k, (((1,), (1,)), ((), ())), preferred_element_type=f32)
            s = jnp.where(valid, s, -jnp.inf)
            sink = sink_ref[0:1, h:h + 1]
            m = jnp.maximum(jnp.max(s, axis=-1, keepdims=True), sink)
            p = jnp.exp(s - m)
            denom = jnp.sum(p, axis=-1, keepdims=True) + jnp.exp(sink - m)
            probs = (p / denom).astype(bf16)
            o_s[:, h * HEAD_DIM:(h + 1) * HEAD_DIM] = _dot(probs, v)
    o_ref[...] = x_ref[...] + _dot(o_s[...].astype(bf16), wo_ref[...])


def _attention(qkv, x, sinks, w_o):
    bsz, seq, _ = qkv.shape
    d = x.shape[-1]
    kv_block = Q_WIDTH // KV_WIDTH
    return pl.pallas_call(
        _attn_kernel,
        grid=(bsz, seq // WINDOW),
        in_specs=[pl.BlockSpec((None, WINDOW, Q_WIDTH), lambda b, n: (b, n, 0)),
                  pl.BlockSpec((None, WINDOW, KV_WIDTH), lambda b, n: (b, n, kv_block)),
                  pl.BlockSpec((None, WINDOW, KV_WIDTH), lambda b, n: (b, jnp.maximum(n - 1, 0), kv_block)),
                  pl.BlockSpec((None, WINDOW, d), lambda b, n: (b, n, 0)),
                  pl.BlockSpec((1, N_Q_HEADS), lambda b, n: (0, 0)),
                  pl.BlockSpec((Q_WIDTH, d), lambda b, n: (0, 0))],
        out_specs=pl.BlockSpec((None, WINDOW, d), lambda b, n: (b, n, 0)),
        out_shape=jax.ShapeDtypeStruct((bsz, seq, d), f32),
        scratch_shapes=[pltpu.VMEM((WINDOW, Q_WIDTH), f32)],
        compiler_params=_tc_params("parallel", "parallel"),
        name="attention",
    )(qkv, qkv, qkv, x, sinks.reshape(1, N_Q_HEADS), w_o.astype(bf16))


def _top16(s, ids=None):
    m_rows = s.shape[0]
    pos = lax.broadcasted_iota(i32, s.shape, 0).astype(f32)
    vals, outs = [], []
    for _ in range(PEER_TOPK):
        mx = jnp.max(s, axis=0, keepdims=True)
        first = jnp.min(jnp.where(s == mx, pos, float(m_rows)), axis=0, keepdims=True)
        hit = pos == first
        vals.append(mx)
        outs.append(first if ids is None else jnp.sum(jnp.where(hit, ids, 0.0), axis=0, keepdims=True))
        s = jnp.where(hit, -jnp.inf, s)
    return jnp.concatenate(vals, axis=0), jnp.concatenate(outs, axis=0)


def _pair_candidates(a0, a1, combine, fill):
    n = a0.shape[1]
    sub = lax.broadcasted_iota(i32, (SUBLANES, n), 0)
    pieces = [combine(a0[0:1], a1)]
    for i in range(1, SUBLANES):
        keep = PEER_TOPK // (i + 1)
        pieces.append(jnp.where(sub < keep, combine(a0[i:i + 1], a1[0:SUBLANES]), fill))
    pieces.append(combine(a0[SUBLANES:PEER_TOPK], a1[0:1]))
    return jnp.concatenate(pieces, axis=0)


def _topk_kernel(q_ref, keys_ref, ids_ref, gates_ref, ids_s, gates_s):
    def head_body(h, carry):
        tops = []
        for c in range(2):
            q = q_ref[2 * h + c]
            s = lax.dot_general(keys_ref[2 * h + c], q, (((1,), (1,)), ((), ())),
                                preferred_element_type=f32)
            tops.append(_top16(s))
        (s0, i0), (s1, i1) = tops
        cand_s = _pair_candidates(s0, s1, lambda a, b: a + b, -jnp.inf)
        cand_i = _pair_candidates(i0, i1, lambda a, b: a * float(PEER_N_KEYS) + b, 0.0)
        best_s, best_i = _top16(cand_s, cand_i)
        e = jnp.exp(best_s - best_s[0:1])
        off = pl.multiple_of(h * PEER_TOPK, PEER_TOPK)
        gates_s[pl.ds(off, PEER_TOPK), :] = e / jnp.sum(e, axis=0, keepdims=True)
        ids_s[pl.ds(off, PEER_TOPK), :] = best_i.astype(i32)
        return carry

    lax.fori_loop(0, PEER_HEADS, head_body, 0, unroll=8)
    ids_ref[...] = ids_s[...].T
    gates_ref[...] = gates_s[...].T


def _peer_topk(q3, sub_keys, chunk, n_chunks):
    nq, t_all, half = q3.shape
    keys = sub_keys.reshape(nq, PEER_N_KEYS, half).astype(bf16)
    tb = TOPK_TILE
    t = t_all // n_chunks
    first = chunk * (t // tb)
    return pl.pallas_call(
        _topk_kernel,
        grid=(t // tb,),
        in_specs=[pl.BlockSpec((nq, tb, half), lambda i: (0, first + i, 0)),
                  pl.BlockSpec((nq, PEER_N_KEYS, half), lambda i: (0, 0, 0))],
        out_specs=[pl.BlockSpec((tb, N_SLOTS), lambda i: (i, 0)),
                   pl.BlockSpec((tb, N_SLOTS), lambda i: (i, 0))],
        out_shape=[jax.ShapeDtypeStruct((t, N_SLOTS), i32), jax.ShapeDtypeStruct((t, N_SLOTS), f32)],
        scratch_shapes=[pltpu.VMEM((N_SLOTS, tb), i32), pltpu.VMEM((N_SLOTS, tb), f32)],
        compiler_params=_tc_params("parallel"),
        name="peer_topk",
    )(q3, keys)


def _gate_kernel(a_ref, g_ref, w_ref):
    w_ref[...] = _gelu(a_ref[...]) * g_ref[...]


def _peer_gate(a, gates):
    t, n = a.shape
    tm = 2 * ROW_TILE
    assert t % tm == 0
    spec = pl.BlockSpec((tm, n), lambda i: (i, 0))
    return pl.pallas_call(
        _gate_kernel, grid=(t // tm,), in_specs=[spec, spec], out_specs=spec,
        out_shape=jax.ShapeDtypeStruct((t, n), f32),
        compiler_params=_tc_params("parallel"), name="peer_gate",
    )(a, gates)


def _sc_params():
    return pltpu.CompilerParams(needs_layout_passes=False)


def _lane_permute(x, idx):
    return x.at[idx].get(mode="promise_in_bounds")


def _halves_f32(pairs_bf16):
    word = plsc.bitcast(pairs_bf16, u32)
    return plsc.bitcast(word << 16, f32), plsc.bitcast(word & jnp.uint32(0xFFFF0000), f32)


def _peer_sc_stage(side, ids, tab, mode):
    t_total, side_w = side.shape
    dw = tab.shape[1]
    out_w = N_SLOTS if mode == "dot" else 2 * dw
    tpw = t_total // SC_WORKERS
    n_pairs = tpw // 2
    assert tpw * SC_WORKERS == t_total and n_pairs * 2 == tpw and N_BATCH >= 2
    n_grp = ROW_BATCH // SC_LANES
    mesh = plsc.VectorSubcoreMesh(core_axis_name="c", subcore_axis_name="s")

    @functools.partial(
        pl.kernel, mesh=mesh,
        out_type=jax.ShapeDtypeStruct((t_total, out_w), f32),
        scratch_types=[
            pltpu.VMEM((N_SLOTS,), i32), pltpu.VMEM((N_SLOTS,), i32),
            pltpu.VMEM((side_w,), side.dtype), pltpu.VMEM((side_w,), side.dtype),
            pltpu.VMEM((out_w,), f32), pltpu.VMEM((out_w,), f32),
            pltpu.VMEM((N_BATCH, ROW_BATCH, dw), u32),
            pltpu.SemaphoreType.DMA((N_BATCH,)), pltpu.SemaphoreType.DMA((2,)),
            pltpu.SemaphoreType.DMA((2,)), pltpu.SemaphoreType.DMA((2,)),
        ],
        compiler_params=_sc_params())
    def k(side_hbm, ids_hbm, tab_hbm, out_hbm, idx0, idx1, side0, side1, out0, out1, rows_v,
          sem_g, sem_i, sem_s, sem_o):
        idx_v, side_v, out_v = (idx0, idx1), (side0, side1), (out0, out1)
        wid = lax.axis_index("s") * SC_CORES + lax.axis_index("c")
        base = wid * tpw
        iota = lax.iota(i32, SC_LANES)

        def row_pairs(slot, r, off):
            return plsc.bitcast(rows_v[slot, r, pl.ds(off, SC_LANES)], bf16)

        def gather(p, b):
            first = b * ROW_BATCH if isinstance(b, int) else pl.multiple_of(b * ROW_BATCH, ROW_BATCH)
            return pltpu.make_async_copy(
                tab_hbm.at[idx_v[p].at[pl.ds(first, ROW_BATCH)]], rows_v.at[b], sem_g.at[b])

        def loads(t, p):
            return (pltpu.make_async_copy(ids_hbm.at[t], idx_v[p], sem_i.at[p]),
                    pltpu.make_async_copy(side_hbm.at[t], side_v[p], sem_s.at[p]))

        def store(t, p):
            return pltpu.make_async_copy(out_v[p], out_hbm.at[t], sem_o.at[p])

        def compute_dot(p, b):
            def group(g, carry):
                r0 = pl.multiple_of(g * SC_LANES, SC_LANES)

                def cbody(c, accs):
                    off = pl.multiple_of(c * LANES, LANES)
                    n_q = LANES // SC_LANES
                    xs = [plsc.bitcast(side_v[p][pl.ds(off + q * SC_LANES, SC_LANES)], bf16) for q in range(n_q)]
                    out = []
                    for j in range(SC_LANES):
                        t = [row_pairs(b, r0 + j, off + q * SC_LANES) * xs[q] for q in range(n_q)]
                        lo0, hi0 = _halves_f32((t[0] + t[1]) + (t[2] + t[3]))
                        lo1, hi1 = _halves_f32((t[4] + t[5]) + (t[6] + t[7]))
                        out.append(accs[j] + ((lo0 + hi0) + (lo1 + hi1)))
                    return tuple(out)
                accs = lax.fori_loop(0, dw // LANES, cbody,
                                     tuple(jnp.zeros((SC_LANES,), f32) for _ in range(SC_LANES)))
                vs = list(accs)
                dist = SC_LANES // 2
                while dist >= 1:
                    keep = (iota & dist) == 0
                    swap = iota ^ dist
                    vs = [jnp.where(keep, vs[k], vs[k + dist]) + _lane_permute(jnp.where(keep, vs[k + dist], vs[k]), swap)
                          for k in range(dist)]
                    dist //= 2
                out_v[p][pl.ds(pl.multiple_of(b * ROW_BATCH + r0, SC_LANES), SC_LANES)] = vs[0]
                return carry

            lax.fori_loop(0, n_grp, group, 0)

        def compute_wsum(p, b):
            def group(g, carry):
                r0 = pl.multiple_of(g * SC_LANES, SC_LANES)
                wv = side_v[p][pl.ds(pl.multiple_of(b * ROW_BATCH + r0, SC_LANES), SC_LANES)]
                sp = []
                for j in range(SC_LANES):
                    w = _lane_permute(wv, jnp.full((SC_LANES,), j, i32))
                    sp.append(plsc.pack(w, w, format=plsc.PackFormat.INTERLEAVED))

                @plsc.parallel_loop(0, dw, step=SC_LANES)
                def _(c):
                    off = pl.multiple_of(c, SC_LANES)
                    t = [row_pairs(b, r0 + j, off) * sp[j] for j in range(SC_LANES)]
                    quads = [_halves_f32((t[q] + t[q + 1]) + (t[q + 2] + t[q + 3])) for q in range(0, SC_LANES, 4)]
                    lo = (quads[0][0] + quads[1][0]) + (quads[2][0] + quads[3][0])
                    hi = (quads[0][1] + quads[1][1]) + (quads[2][1] + quads[3][1])
                    plsc.addupdate(out_v[p].at[pl.ds(off, SC_LANES)], lo)
                    plsc.addupdate(out_v[p].at[pl.ds(dw + off, SC_LANES)], hi)
                return carry

            lax.fori_loop(0, n_grp, group, 0)

        compute = compute_dot if mode == "dot" else compute_wsum

        for c in loads(base, 0):
            c.start()
        for c in loads(base, 0):
            c.wait()
        for b in range(N_BATCH - 1):
            gather(0, b).start()
        for c in loads(base + 1, 1):
            c.start()

        def pair_body(i2, carry):
            not_last = i2 < n_pairs - 1
            for p in (0, 1):
                t = base + 2 * i2 + p

                @pl.when(i2 > 0)
                def _():
                    store(t - 2, p).wait()

                if mode == "wsum":
                    @pl.loop(0, out_w, step=SC_LANES)
                    def _(c):
                        out_v[p][pl.ds(pl.multiple_of(c, SC_LANES), SC_LANES)] = jnp.zeros((SC_LANES,), f32)

                def batch_body(b, carry2):
                    @pl.when(b == 0)
                    def _():
                        gather(p, N_BATCH - 1).start()

                    def start_next():
                        @pl.when(b == 1)
                        def _():
                            for c in loads(t + 1, 1 - p):
                                c.wait()
                        gather(1 - p, b - 1).start()

                    pl.when(b > 0 if p == 0 else jnp.logical_and(b > 0, not_last))(start_next)
                    gather(p, b).wait()
                    compute(p, b)
                    return carry2

                lax.fori_loop(0, N_BATCH, batch_body, 0)
                store(t, p).start()

                @pl.when(not_last)
                def _():
                    for c in loads(t + 2, p):
                        c.start()
            return carry

        lax.fori_loop(0, n_pairs, pair_body, 0)
        store(base + tpw - 2, 0).wait()
        store(base + tpw - 1, 1).wait()

    return k(side, ids, tab)


def _peer(x2d, row0, rows, g, w_q, sub_keys, u_pairs, v_pairs):
    h_pairs, q3 = _peer_q(x2d, row0, rows, g, w_q)
    ids, gates = _peer_topk(q3, sub_keys, 0, 1)
    a = _peer_sc_stage(h_pairs, ids, u_pairs, "dot")
    w = _peer_gate(a, gates)
    return _peer_sc_stage(w, ids, v_pairs, "wsum")


def kernel(x, norm_mix, norm_ffn, norm_final, mix_w_in, s5_lambda_re, s5_lambda_im, s5_log_step, s5_b_re, s5_b_im, s5_c_re, s5_c_im, s5_d, s5_w_glu, conv_dw_w, conv_dw_b, conv_ln_g, conv_ln_b, mix_w_out, attn_w_qkv, attn_b_qkv, attn_sinks, attn_w_o, peer_w_q, peer_sub_keys, peer_u, peer_v):
    bsz, seq, d = x.shape
    tabs = _s5_tables(s5_lambda_re[0], s5_lambda_im[0], s5_log_step[0], s5_b_re[0], s5_b_im[0],
                      s5_c_re[0], s5_c_im[0])

    pairs = [(_pack_bf16_pairs(peer_u, l), _pack_bf16_pairs(peer_v, l)) for l in range(2)]
    tc = seq // PEER_CHUNKS
    x2d = x.reshape(bsz * seq, d)

    def peer(x_rows, row0, rows, l):
        return _peer(x_rows, row0, rows, norm_ffn[l], peer_w_q[l], peer_sub_keys[l], *pairs[l])

    def trunk(b):
        z = _mix_in(x2d, b, bsz, norm_mix[0], mix_w_in[0]).reshape(1, seq, -1)
        state = _mixer_state0()
        x1, p0 = [], []
        for c in range(PEER_CHUNKS):
            xc, state = _mixer(z, x, b, state, c, PEER_CHUNKS, tabs, s5_d[0], s5_w_glu[0], conv_dw_w[0],
                               conv_dw_b[0], conv_ln_g[0], conv_ln_b[0], mix_w_out[0])
            x1.append(xc.reshape(tc, d))
            if c == 0:
                p0 += [peer(x1[0], 0, FIRST_PIECE, 0), peer(x1[0], FIRST_PIECE, tc - FIRST_PIECE, 0)]
            else:
                p0.append(peer(x1[-1], 0, tc, 0))
        x2, qkv = _qkv(jnp.concatenate(x1, axis=0), jnp.concatenate(p0, axis=0), norm_mix[1],
                       attn_w_qkv[0], attn_b_qkv[0], seq)
        x3 = _attention(qkv.reshape(1, seq, -1), x2.reshape(1, seq, d), attn_sinks[0], attn_w_o[0]).reshape(seq, d)
        return x3, [peer(x3, c * tc, tc, 1) for c in range(PEER_CHUNKS)]

    out = None
    for b, (x3, p1) in enumerate([trunk(b) for b in range(bsz)]):
        for c in range(PEER_CHUNKS):
            out = _final_norm(x3, c, PEER_CHUNKS, p1[c], norm_final, b * PEER_CHUNKS + c, bsz * PEER_CHUNKS, out)
    return out.reshape(bsz, seq, d)
```
